```python
import math
import jax, jax.numpy as jnp
from jax import lax
import numpy as np

D_MODEL = 1024
BATCH = 2
SEQ = 8192
DEPTH = 4
DEC_BATCH = 128
DEC_SEQ = 4
PAST_LEN = 2048
PAGE_SIZE = 128

N_A = DEPTH // 2
N_B = DEPTH - N_A
D_INNER = 2 * D_MODEL
SSM_HEADDIM = 64
SSM_HEADS = D_INNER // SSM_HEADDIM
SSM_GROUPS = 4
HEADS_PER_GROUP = SSM_HEADS // SSM_GROUPS
D_STATE = 128
CONV_W = 4
CONV_DIM = D_INNER + 2 * SSM_GROUPS * D_STATE
SSD_CHUNK = 128
N_HEADS = 16
HEAD_DIM = 64
KV_HEADS = 4
CMP_BLOCK = 32
SEL_BLOCK = 64
N_SEL = 16
WINDOW = 512
Q_BLOCK = 128
N_KV_ROWS = 4
FORCE_SCORE = 1.0e4
PEER_HEADS = 8
N_KEYS = 128
N_EXPERTS = N_KEYS * N_KEYS
PEER_TOPK = 16
PEER_QDIM = 256
PEER_BLOCK = 256
DEEPNORM_ALPHA = (2 * DEPTH) ** 0.25
DEEPNORM_BETA = (8 * DEPTH) ** -0.25
LN_EPS = 1e-5

kernel_name = 'yoco_mamba2_nsa_peer_decoder_step'


def layer_norm(x, g, b):
    xf = x.astype(jnp.float32)
    mu = jnp.mean(xf, axis=-1, keepdims=True)
    var = jnp.mean(jnp.square(xf - mu), axis=-1, keepdims=True)
    return ((xf - mu) * lax.rsqrt(var + LN_EPS) * g + b).astype(x.dtype)


def masked_softmax(s, mask):
    s = jnp.where(mask, s, -jnp.inf)
    m = jnp.max(s, axis=-1, keepdims=True)
    m = jnp.where(jnp.isfinite(m), m, 0.0)
    e = jnp.exp(s - m)
    return e / jnp.maximum(jnp.sum(e, axis=-1, keepdims=True), 1e-30)


def alibi_slopes():
    return jnp.exp2(-8.0 * (jnp.arange(N_HEADS, dtype=jnp.float32) + 1.0) / N_HEADS)


def ssd_scan(xs, dt, A, Bm, Cm, h0, chunk):
    b, L, G, Hg, P = xs.shape
    N = Bm.shape[-1]
    nc = L // chunk
    f32 = jnp.float32
    xs = xs.astype(f32).reshape(b, nc, chunk, G, Hg, P)
    dt = dt.reshape(b, nc, chunk, G, Hg)
    Bm = Bm.astype(f32).reshape(b, nc, chunk, G, N)
    Cm = Cm.astype(f32).reshape(b, nc, chunk, G, N)
    a_cum = jnp.cumsum(dt * A, axis=2)
    causal = jnp.tril(jnp.ones((chunk, chunk), bool))[:, :, None, None]
    seg = a_cum[:, :, :, None] - a_cum[:, :, None, :]
    decay = jnp.exp(jnp.where(causal, seg, -jnp.inf))
    cb = jnp.einsum('bctgn,bcsgn->bctsg', Cm, Bm)
    mix = cb[..., None] * decay * dt[:, :, None]
    y_diag = jnp.einsum('bctsgh,bcsghp->bctghp', mix, xs)
    w_end = jnp.exp(a_cum[:, :, -1:] - a_cum) * dt
    states = jnp.einsum('bclgn,bclgh,bclghp->bcghpn', Bm, w_end, xs)
    chunk_decay = jnp.exp(a_cum[:, :, -1])

    def step(h, inp):
        st, dec = inp
        return dec[..., None, None] * h + st, h

    h_final, h_prev = lax.scan(step, h0, (jnp.moveaxis(states, 1, 0), jnp.moveaxis(chunk_decay, 1, 0)))
    h_prev = jnp.moveaxis(h_prev, 0, 1)
    y_off = jnp.einsum('bctgn,bctgh,bcghpn->bctghp', Cm, jnp.exp(a_cum), h_prev)
    return (y_diag + y_off).reshape(b, L, G, Hg, P), h_final


def mamba_mixer(x, conv_buf, h0, w_in, conv_w, conv_b, dt_bias, a_log, d_skip, norm_w, w_out):
    b, L, _ = x.shape
    f32 = jnp.float32
    zxbcdt = x @ w_in
    z = zxbcdt[..., :D_INNER]
    xbc = zxbcdt[..., D_INNER:D_INNER + CONV_DIM]
    dt_raw = zxbcdt[..., D_INNER + CONV_DIM:]
    xbc_full = jnp.concatenate([conv_buf.astype(x.dtype), xbc], axis=1)
    new_conv = xbc_full[:, -(CONV_W - 1):]
    conv = xbc_full[:, 0:L] * conv_w[0]
    for k in range(1, CONV_W):
        conv = conv + xbc_full[:, k:k + L] * conv_w[k]
    conv = jax.nn.silu(conv + conv_b)
    xs = conv[..., :D_INNER].reshape(b, L, SSM_GROUPS, HEADS_PER_GROUP, SSM_HEADDIM)
    Bm = conv[..., D_INNER:D_INNER + SSM_GROUPS * D_STATE].reshape(b, L, SSM_GROUPS, D_STATE)
    Cm = conv[..., D_INNER + SSM_GROUPS * D_STATE:].reshape(b, L, SSM_GROUPS, D_STATE)
    dt = jax.nn.softplus(dt_raw.astype(f32) + dt_bias.astype(f32)).reshape(b, L, SSM_GROUPS, HEADS_PER_GROUP)
    A = -jnp.exp(a_log.astype(f32)).reshape(SSM_GROUPS, HEADS_PER_GROUP)
    chunk = SSD_CHUNK if L % SSD_CHUNK == 0 else L
    h0 = h0.astype(f32).reshape(b, SSM_GROUPS, HEADS_PER_GROUP, SSM_HEADDIM, D_STATE)
    y, h_new = ssd_scan(xs, dt, A, Bm, Cm, h0, chunk)
    y = y + d_skip.astype(f32).reshape(SSM_GROUPS, HEADS_PER_GROUP)[:, :, None] * xs.astype(f32)
    y = y.reshape(b, L, D_INNER) * jax.nn.silu(z.astype(f32))
    yg = y.reshape(b, L, SSM_GROUPS, D_INNER // SSM_GROUPS)
    yg = yg * lax.rsqrt(jnp.mean(yg * yg, axis=-1, keepdims=True) + 1e-5)
    y = (yg.reshape(b, L, D_INNER) * norm_w).astype(x.dtype)
    h_new = h_new.reshape(b, SSM_HEADS, SSM_HEADDIM, D_STATE).astype(x.dtype)
    return y @ w_out, new_conv, h_new


def peer_ffn(x, w_q, sub_keys, u_tab, v_tab):
    b, L, D = x.shape
    T = b * L
    nb = -(-T // PEER_BLOCK)
    half = PEER_QDIM // 2
    xf = jnp.pad(x.reshape(T, D), ((0, nb * PEER_BLOCK - T), (0, 0))).reshape(nb, PEER_BLOCK, D)

    def block(xb):
        q = (xb @ w_q).reshape(PEER_BLOCK, PEER_HEADS, 2, half)
        s1 = jnp.einsum('thd,kd->thk', q[:, :, 0], sub_keys[0]).astype(jnp.float32)
        s2 = jnp.einsum('thd,kd->thk', q[:, :, 1], sub_keys[1]).astype(jnp.float32)
        v1, i1 = lax.top_k(s1, PEER_TOPK)
        v2, i2 = lax.top_k(s2, PEER_TOPK)
        cand = (v1[..., :, None] + v2[..., None, :]).reshape(PEER_BLOCK, PEER_HEADS, PEER_TOPK * PEER_TOPK)
        cidx = (i1[..., :, None] * N_KEYS + i2[..., None, :]).reshape(PEER_BLOCK, PEER_HEADS, PEER_TOPK * PEER_TOPK)
        sc, pos = lax.top_k(cand, PEER_TOPK)
        idx = jnp.take_along_axis(cidx, pos, axis=-1)
        g = jax.nn.softmax(sc, axis=-1)
        act = jax.nn.gelu(jnp.einsum('thkd,td->thk', u_tab[idx], xb).astype(jnp.float32))
        coef = (g * act).astype(v_tab.dtype)
        return jnp.einsum('thk,thkd->td', coef, v_tab[idx])

    out = lax.map(block, xf)
    return out.reshape(nb * PEER_BLOCK, D)[:T].reshape(b, L, D).astype(x.dtype)


def shared_kv(x, w_kv):
    b, L, _ = x.shape
    return (x @ w_kv).reshape(b, L, N_KV_ROWS + 2, KV_HEADS, HEAD_DIM)


def nsa_shared(rows, w_cmp, pos_cmp):
    b, Lk = rows.shape[:2]
    n_sel = -(-Lk // SEL_BLOCK)
    Lp = n_sel * SEL_BLOCK
    rows = jnp.pad(rows, ((0, 0), (0, Lp - Lk), (0, 0), (0, 0), (0, 0)))
    n_cmp = Lp // CMP_BLOCK
    blk = rows[:, :, :2].reshape(b, n_cmp, CMP_BLOCK, 2, KV_HEADS, HEAD_DIM)
    blk = blk + jnp.transpose(pos_cmp, (1, 0, 2))[:, :, None, :]
    cmp = jnp.einsum('bnlcgd,clde->bncge', blk, w_cmp)
    c_end = jnp.arange(n_cmp) * CMP_BLOCK + CMP_BLOCK - 1
    sel = rows[:, :, 2:].reshape(b, n_sel, SEL_BLOCK, 2, KV_HEADS, HEAD_DIM)
    ksel = jnp.transpose(sel[:, :, :, 0], (0, 3, 1, 2, 4))
    vsel = jnp.transpose(sel[:, :, :, 1], (0, 3, 1, 2, 4))
    return (cmp[:, :, 0], cmp[:, :, 1], c_end, ksel, vsel)


def nsa_query(x, w_in):
    b, L, _ = x.shape
    h = x @ w_in
    q = h[..., :N_HEADS * HEAD_DIM].reshape(b, L, N_HEADS, HEAD_DIM)
    gates = jax.nn.sigmoid(h[..., N_HEADS * HEAD_DIM:].astype(jnp.float32)).reshape(b, L, N_HEADS, 3)
    return q, gates


def attend_block(q, gates, t_pos, kc, vc, c_end, ksel, vsel, wk, wv, w_pos, slopes):
    f32 = jnp.float32
    b, nq = q.shape[:2]
    hg = N_HEADS // KV_HEADS
    qg = q.reshape(b, nq, KV_HEADS, hg, HEAD_DIM)
    sl = slopes.reshape(KV_HEADS, hg)[:, :, None, None]
    scale = HEAD_DIM ** -0.5
    tq = t_pos[:, None]
    dist_c = tq - c_end[None, :]
    s = jnp.einsum('bqghd,bngd->bghqn', qg, kc).astype(f32) * scale - sl * dist_c.astype(f32)
    p_cmp = masked_softmax(s, dist_c >= 0)
    o_cmp = jnp.einsum('bghqn,bngd->bqghd', p_cmp.astype(vc.dtype), vc)
    n_sel = ksel.shape[2]
    imp = p_cmp.sum(axis=2).reshape(b, KV_HEADS, nq, n_sel, SEL_BLOCK // CMP_BLOCK).sum(-1)
    blk = jnp.arange(n_sel)[None, :]
    cur = tq // SEL_BLOCK
    forced = (blk == 0) | (blk == cur) | (blk == cur - 1)
    valid = blk * SEL_BLOCK <= tq
    score = jnp.where(forced, FORCE_SCORE, jnp.where(valid, imp, -1.0))
    n_top = min(N_SEL, n_sel)
    _, idx = lax.top_k(score, n_top)
    take = jax.vmap(jax.vmap(lambda blocks, ix: blocks[ix]))
    ks = take(ksel, idx).reshape(b, KV_HEADS, nq, n_top * SEL_BLOCK, HEAD_DIM)
    vs = take(vsel, idx).reshape(b, KV_HEADS, nq, n_top * SEL_BLOCK, HEAD_DIM)
    pos = (idx[..., None] * SEL_BLOCK + jnp.arange(SEL_BLOCK)).reshape(b, KV_HEADS, nq, n_top * SEL_BLOCK)
    dist_s = (tq - pos)[:, :, None]
    s = jnp.einsum('bqghd,bgqkd->bghqk', qg, ks).astype(f32) * scale - sl * dist_s.astype(f32)
    p_sel = masked_softmax(s, dist_s >= 0)
    o_sel = jnp.einsum('bghqk,bgqkd->bqghd', p_sel.astype(vs.dtype), vs)
    dist_w = tq - w_pos[None, :]
    mask_w = (dist_w >= 0) & (dist_w < WINDOW) & (w_pos[None, :] >= 0)
    s = jnp.einsum('bqghd,bkgd->bghqk', qg, wk).astype(f32) * scale - sl * dist_w.astype(f32)
    p_win = masked_softmax(s, mask_w)
    o_win = jnp.einsum('bghqk,bkgd->bqghd', p_win.astype(wv.dtype), wv)
    g = gates.reshape(b, nq, KV_HEADS, hg, 3).astype(o_cmp.dtype)
    o = g[..., 0:1] * o_cmp + g[..., 1:2] * o_sel + g[..., 2:3] * o_win
    return o.reshape(b, nq, N_HEADS * HEAD_DIM)


def nsa_prompt(x, w_in, w_out, shared, win_pad, slopes):
    b, L, _ = x.shape
    q, gates = nsa_query(x, w_in)
    nq = L // Q_BLOCK
    qb = jnp.moveaxis(q.reshape(b, nq, Q_BLOCK, N_HEADS, HEAD_DIM), 1, 0)
    gb = jnp.moveaxis(gates.reshape(b, nq, Q_BLOCK, N_HEADS, 3), 1, 0)
    kc, vc, c_end, ksel, vsel = shared

    def body(args):
        qi, gi, i = args
        q0 = i * Q_BLOCK
        t_pos = q0 + jnp.arange(Q_BLOCK)
        w = lax.dynamic_slice_in_dim(win_pad, q0, WINDOW + Q_BLOCK, axis=1)
        w_pos = q0 - WINDOW + jnp.arange(WINDOW + Q_BLOCK)
        return attend_block(qi, gi, t_pos, kc, vc, c_end, ksel, vsel, w[:, :, 0], w[:, :, 1], w_pos, slopes)

    o = lax.map(body, (qb, gb, jnp.arange(nq)))
    o = jnp.moveaxis(o, 0, 1).reshape(b, L, N_HEADS * HEAD_DIM)
    return o @ w_out


def setup_inputs(seed: int = 0) -> dict:
    key = jax.random.key(seed)
    ks = iter(jax.random.split(key, 32))
    f32 = jnp.float32

    def nrm(shape, scale):
        return jax.random.normal(next(ks), shape, f32) * scale

    n_pages = PAST_LEN // PAGE_SIZE
    n_used = DEC_BATCH * n_pages
    n_pool = n_used + max(1, n_used // 4)
    win_cache = min(WINDOW, PAST_LEN)
    x_prompt = nrm((BATCH, SEQ, D_MODEL), 1.0)
    x_sample = nrm((DEC_BATCH, DEC_SEQ, D_MODEL), 1.0)
    cache_kv_pages = nrm((n_pool, PAGE_SIZE, N_KV_ROWS, KV_HEADS, HEAD_DIM), 1.0)
    cache_win = nrm((DEC_BATCH, win_cache, 2, KV_HEADS, HEAD_DIM), 1.0)
    state_ssm = nrm((N_A, DEC_BATCH, SSM_HEADS, SSM_HEADDIM, D_STATE), 0.1)
    state_conv = nrm((N_A, DEC_BATCH, CONV_W - 1, CONV_DIM), 1.0)
    page_table = jax.random.permutation(next(ks), n_pool)[:n_used].reshape(DEC_BATCH, n_pages).astype(jnp.int32)
    ln_g = 1.0 + nrm((DEPTH, 2, D_MODEL), 0.01)
    ln_b = nrm((DEPTH, 2, D_MODEL), 0.01)
    m_w_in = nrm((N_A, D_MODEL, 2 * D_INNER + 2 * SSM_GROUPS * D_STATE + SSM_HEADS), D_MODEL ** -0.5)
    m_conv_w = nrm((N_A, CONV_W, CONV_DIM), CONV_W ** -0.5)
    m_conv_b = nrm((N_A, CONV_DIM), 0.01)
    dt0 = jnp.exp(jax.random.uniform(next(ks), (N_A, SSM_HEADS), f32, math.log(1e-3), math.log(1e-1)))
    m_dt_bias = dt0 + jnp.log(-jnp.expm1(-dt0))
    m_a_log = jnp.log(jax.random.uniform(next(ks), (N_A, SSM_HEADS), f32, 1.0, 16.0))
    m_d_skip = 1.0 + nrm((N_A, SSM_HEADS), 0.01)
    m_norm_w = 1.0 + nrm((N_A, D_INNER), 0.01)
    m_w_out = nrm((N_A, D_INNER, D_MODEL), D_INNER ** -0.5 * DEEPNORM_BETA)
    col_scale = jnp.repeat(jnp.array([1.0, DEEPNORM_BETA, 1.0, DEEPNORM_BETA, 1.0, DEEPNORM_BETA], f32), KV_HEADS * HEAD_DIM)
    w_kv_shared = nrm((D_MODEL, (N_KV_ROWS + 2) * KV_HEADS * HEAD_DIM), D_MODEL ** -0.5) * col_scale[None, :]
    w_cmp = nrm((2, CMP_BLOCK, HEAD_DIM, HEAD_DIM), (CMP_BLOCK * HEAD_DIM) ** -0.5)
    pos_cmp = nrm((2, CMP_BLOCK, HEAD_DIM), 0.02)
    nsa_w_in = nrm((N_B, D_MODEL, N_HEADS * HEAD_DIM + 3 * N_HEADS), D_MODEL ** -0.5)
    nsa_w_out = nrm((N_B, N_HEADS * HEAD_DIM, D_MODEL), (N_HEADS * HEAD_DIM) ** -0.5 * DEEPNORM_BETA)
    peer_w_q = nrm((DEPTH, D_MODEL, PEER_HEADS * PEER_QDIM), D_MODEL ** -0.5)
    peer_sub_keys = nrm((DEPTH, 2, N_KEYS, PEER_QDIM // 2), (PEER_QDIM // 2) ** -0.5)
    peer_u = nrm((DEPTH, N_EXPERTS, D_MODEL), D_MODEL ** -0.5 * DEEPNORM_BETA)
    peer_v = nrm((DEPTH, N_EXPERTS, D_MODEL), DEEPNORM_BETA * PEER_HEADS ** -0.5)
    return {'x_prompt': x_prompt, 'x_sample': x_sample, 'cache_kv_pages': cache_kv_pages,
            'cache_win': cache_win, 'state_ssm': state_ssm, 'state_conv': state_conv,
            'page_table': page_table, 'ln_g': ln_g, 'ln_b': ln_b, 'm_w_in': m_w_in,
            'm_conv_w': m_conv_w, 'm_conv_b': m_conv_b, 'm_dt_bias': m_dt_bias, 'm_a_log': m_a_log,
            'm_d_skip': m_d_skip, 'm_norm_w': m_norm_w, 'm_w_out': m_w_out, 'w_kv_shared': w_kv_shared,
            'w_cmp': w_cmp, 'pos_cmp': pos_cmp, 'nsa_w_in': nsa_w_in, 'nsa_w_out': nsa_w_out,
            'peer_w_q': peer_w_q, 'peer_sub_keys': peer_sub_keys, 'peer_u': peer_u, 'peer_v': peer_v}


def reference(x_prompt, x_sample, cache_kv_pages, cache_win, state_ssm, state_conv, page_table,
              ln_g, ln_b, m_w_in, m_conv_w, m_conv_b, m_dt_bias, m_a_log, m_d_skip, m_norm_w, m_w_out,
              w_kv_shared, w_cmp, pos_cmp, nsa_w_in, nsa_w_out, peer_w_q, peer_sub_keys, peer_u, peer_v):
    slopes = alibi_slopes()
    bp, seq = x_prompt.shape[:2]
    bs, dec_seq = x_sample.shape[:2]
    past_len = page_table.shape[1] * cache_kv_pages.shape[1]
    win_cache = cache_win.shape[1]
    xp, xs = x_prompt, x_sample
    ssm_p, ssm_s, conv_p, conv_s = [], [], [], []
    for i in range(DEPTH):
        if i < N_A:
            mp = (m_w_in[i], m_conv_w[i], m_conv_b[i], m_dt_bias[i], m_a_log[i], m_d_skip[i], m_norm_w[i], m_w_out[i])
            conv0 = jnp.zeros((bp, CONV_W - 1, CONV_DIM), xp.dtype)
            h0 = jnp.zeros((bp, SSM_HEADS, SSM_HEADDIM, D_STATE), jnp.float32)
            yp, cp, hp = mamba_mixer(xp, conv0, h0, *mp)
            ys, cs, hs = mamba_mixer(xs, state_conv[i], state_ssm[i], *mp)
            ssm_p.append(hp)
            ssm_s.append(hs)
            conv_p.append(cp)
            conv_s.append(cs)
        else:
            if i == N_A:
                kv_p = shared_kv(xp, w_kv_shared)
                kv_s = shared_kv(xs, w_kv_shared)
                rows_p = kv_p[:, :, :N_KV_ROWS]
                win_p = kv_p[:, :, N_KV_ROWS:]
                rows_s = kv_s[:, :, :N_KV_ROWS]
                win_s = kv_s[:, :, N_KV_ROWS:]
                past = cache_kv_pages[page_table].reshape(bs, past_len, N_KV_ROWS, KV_HEADS, HEAD_DIM)
                shared_p = nsa_shared(rows_p, w_cmp, pos_cmp)
                shared_s = nsa_shared(jnp.concatenate([past.astype(rows_s.dtype), rows_s], axis=1), w_cmp, pos_cmp)
                win_pad_p = jnp.pad(win_p, ((0, 0), (WINDOW, 0), (0, 0), (0, 0), (0, 0)))
                win_full_s = jnp.concatenate([cache_win.astype(win_s.dtype), win_s], axis=1)
            j = i - N_A
            yp = nsa_prompt(xp, nsa_w_in[j], nsa_w_out[j], shared_p, win_pad_p, slopes)
            q_s, g_s = nsa_query(xs, nsa_w_in[j])
            t_pos = past_len + jnp.arange(dec_seq)
            w_pos = past_len - win_cache + jnp.arange(win_cache + dec_seq)
            ys = attend_block(q_s, g_s, t_pos, *shared_s, win_full_s[:, :, 0], win_full_s[:, :, 1], w_pos, slopes) @ nsa_w_out[j]
        xp = layer_norm(DEEPNORM_ALPHA * xp + yp, ln_g[i, 0], ln_b[i, 0])
        xs = layer_norm(DEEPNORM_ALPHA * xs + ys, ln_g[i, 0], ln_b[i, 0])
        xp = layer_norm(DEEPNORM_ALPHA * xp + peer_ffn(xp, peer_w_q[i], peer_sub_keys[i], peer_u[i], peer_v[i]), ln_g[i, 1], ln_b[i, 1])
        xs = layer_norm(DEEPNORM_ALPHA * xs + peer_ffn(xs, peer_w_q[i], peer_sub_keys[i], peer_u[i], peer_v[i]), ln_g[i, 1], ln_b[i, 1])
    return (xp, xs, rows_p, rows_s, win_p[:, -min(WINDOW, seq):], win_full_s[:, -win_cache:],
            jnp.stack(ssm_p), jnp.stack(ssm_s), jnp.stack(conv_p), jnp.stack(conv_s))
```

```python
import functools
import math

import jax
import jax.numpy as jnp
from jax import lax
from jax.experimental import pallas as pl
from jax.experimental.pallas import tpu as pltpu

F32 = jnp.float32
BF16 = jnp.bfloat16
I32 = jnp.int32

DEPTH_ALPHA_POW = 0.25
LN_EPS = 1e-5
SEL_BLOCK = 64
N_SEL = 16
WINDOW = 512
FORCE_SCORE = 1.0e4
PEER_TOPK = 16
SSD_CHUNK = 128
Q_BLOCK = 128

LANES = 128
SUBLANES = 8
VMEM_LIMIT_BYTES = 56 * 1024 * 1024

_NT = (((1,), (1,)), ((), ()))
_TN = (((0,), (0,)), ((), ()))
_HI = lax.Precision.HIGHEST


def _cparams(*sem):
    return pltpu.CompilerParams(dimension_semantics=sem, vmem_limit_bytes=VMEM_LIMIT_BYTES)


def _pick(n, cands):
    for c in cands:
        if n % c == 0:
            return c
    raise ValueError(f"no tile in {cands} divides {n}")


def _pad_cols(w, n):
    return jnp.pad(w, ((0, 0), (0, n - w.shape[1])))


def _mm_kernel(x_ref, w_ref, o_ref):
    o_ref[...] = jnp.dot(x_ref[...].astype(BF16), w_ref[...],
                         preferred_element_type=F32).astype(o_ref.dtype)


def _mm_bias_kernel(x_ref, xb_ref, w_ref, o_ref):
    x = (x_ref[...] + xb_ref[...]).astype(BF16)
    o_ref[...] = jnp.dot(x, w_ref[...], preferred_element_type=F32).astype(o_ref.dtype)


def _matmul(x, w, *, xbias=None, out_dtype=F32, tm_cap=512):
    M, K = x.shape
    N = w.shape[1]
    tm = _pick(M, tuple(t for t in (512, 256, 128, 64, 32, 16, 8) if t <= tm_cap))
    tn = _pick(N, (1024, 512, 256, 128))
    x_spec = pl.BlockSpec((tm, K), lambda j, i: (i, 0))
    w_spec = pl.BlockSpec((K, tn), lambda j, i: (0, j))
    o_spec = pl.BlockSpec((tm, tn), lambda j, i: (i, j))
    if xbias is None:
        body, specs, args = _mm_kernel, [x_spec, w_spec], (x, w)
    else:
        b_spec = pl.BlockSpec((1, K), lambda j, i: (0, 0))
        body, specs, args = _mm_bias_kernel, [x_spec, b_spec, w_spec], (x, xbias, w)
    return pl.pallas_call(
        body, grid=(N // tn, M // tm), in_specs=specs, out_specs=o_spec,
        out_shape=jax.ShapeDtypeStruct((M, N), out_dtype),
        compiler_params=_cparams("parallel", "parallel"))(*args)


def _deepnorm_ln(v, g, b):
    mu = jnp.mean(v, axis=-1, keepdims=True)
    d = v - mu
    var = jnp.mean(d * d, axis=-1, keepdims=True)
    return d * lax.rsqrt(var + LN_EPS) * g + b


def _mm_ln_kernel(x_ref, w_ref, r_ref, g_ref, b_ref, o_ref, *, alpha):
    y = jnp.dot(x_ref[...].astype(BF16), w_ref[...], preferred_element_type=F32)
    o_ref[...] = _deepnorm_ln(alpha * r_ref[...] + y, g_ref[...], b_ref[...])


def _matmul_ln(x, w, resid, g, b, alpha):
    M, K = x.shape
    D = w.shape[1]
    tm = _pick(M, (512, 256, 128, 64, 32, 16, 8))
    row = lambda i: (i, 0)
    fixed = lambda i: (0, 0)
    return pl.pallas_call(
        functools.partial(_mm_ln_kernel, alpha=alpha), grid=(M // tm,),
        in_specs=[pl.BlockSpec((tm, K), row), pl.BlockSpec((K, D), fixed),
                  pl.BlockSpec((tm, D), row), pl.BlockSpec((1, D), fixed), pl.BlockSpec((1, D), fixed)],
        out_specs=pl.BlockSpec((tm, D), row),
        out_shape=jax.ShapeDtypeStruct((M, D), F32),
        compiler_params=_cparams("parallel"))(x, w, resid, g.reshape(1, D), b.reshape(1, D))


def _softplus(x):
    return jnp.maximum(x, 0.0) + jnp.log1p(jnp.exp(-jnp.abs(x)))


def _silu(x):
    return x * jax.nn.sigmoid(x)


def _ssd_kernel(z_ref, xbc_ref, dtr_ref, cbuf_ref, h0_ref, cw_ref, cb_ref, dtb_ref, alog_ref,
                dsk_ref, nw_ref, y_ref, hn_ref, xf_scr, h_scr, *, Lc, H, P, N, G, n_valid, conv_w):
    c = pl.program_id(1)
    DI = H * P
    HG = H // G
    GP = HG * P
    tail = xf_scr.shape[0] - Lc

    @pl.when(c == 0)
    def _():
        xf_scr[0:tail, :] = cbuf_ref[0]
        h_scr[...] = h0_ref[0]

    xf_scr[tail:tail + Lc, :] = xbc_ref[0]
    conv = cb_ref[...]
    for k in range(conv_w):
        conv = conv + xf_scr[pl.ds(tail - (conv_w - 1) + k, Lc), :] * cw_ref[k:k + 1, :]
    conv = _silu(conv)
    xf_scr[0:tail, :] = xf_scr[pl.ds(Lc, tail), :]

    xs = conv[:, :DI]
    Bm = conv[:, DI:DI + G * N]
    Cm = conv[:, DI + G * N:]

    lane = lax.broadcasted_iota(I32, (Lc, LANES), 1)
    rowi = lax.broadcasted_iota(I32, (Lc, LANES), 0)
    dt = jnp.where((lane < H) & (rowi < n_valid), _softplus(dtr_ref[0] + dtb_ref[...]), 0.0)
    a = dt * (-jnp.exp(alog_ref[...]))
    ti = lax.broadcasted_iota(I32, (Lc, Lc), 0)
    si = lax.broadcasted_iota(I32, (Lc, Lc), 1)
    causal = ti >= si
    tril = jnp.where(causal, 1.0, 0.0).astype(F32)
    eye = jnp.where(lax.broadcasted_iota(I32, (LANES, LANES), 0) ==
                    lax.broadcasted_iota(I32, (LANES, LANES), 1), 1.0, 0.0).astype(F32)
    acum = jnp.dot(tril, a, precision=_HI, preferred_element_type=F32)
    acum_t = lax.dot_general(eye, acum, _NT, precision=_HI, preferred_element_type=F32)
    dt_t = lax.dot_general(eye, dt, _NT, precision=_HI, preferred_element_type=F32)
    alast = acum[Lc - 1:Lc, :]
    wend = jnp.exp(alast - acum) * dt
    ea = jnp.exp(acum)
    edec = jnp.exp(alast)
    dsk = dsk_ref[...]
    first_half = lax.broadcasted_iota(I32, (Lc, 2 * P), 1) < P

    def pair(v, h):
        rows = v.shape[0]
        return jnp.where(first_half[:rows], jnp.broadcast_to(v[:, h:h + 1], (rows, 2 * P)),
                         jnp.broadcast_to(v[:, h + 1:h + 2], (rows, 2 * P)))

    y_parts = []
    for g in range(G):
        Bg = Bm[:, g * N:(g + 1) * N].astype(BF16)
        Cg = Cm[:, g * N:(g + 1) * N].astype(BF16)
        cb = lax.dot_general(Cg, Bg, _NT, preferred_element_type=F32)
        Sg = h_scr[g * GP:(g + 1) * GP, :]
        yoff = lax.dot_general(Cg, Sg.astype(BF16), _NT, preferred_element_type=F32)
        xw_parts = []
        for pr in range(HG // 2):
            h = g * HG + 2 * pr
            xs_pair = xs[:, h * P:(h + 2) * P]
            xs_pair_b = xs_pair.astype(BF16)
            outs = []
            for hh in (h, h + 1):
                seg = acum[:, hh:hh + 1] - acum_t[hh:hh + 1, :]
                dec = jnp.exp(jnp.where(causal, seg, -jnp.inf))
                mix = cb * dec * dt_t[hh:hh + 1, :]
                outs.append(jnp.dot(mix.astype(BF16), xs_pair_b, preferred_element_type=F32))
            ydiag = jnp.where(first_half, outs[0], outs[1])
            y_pair = ydiag + yoff[:, pr * 2 * P:(pr + 1) * 2 * P] * pair(ea, h) + pair(dsk, h) * xs_pair
            y_parts.append(y_pair)
            xw_parts.append(xs_pair * pair(wend, h))
        xw = jnp.concatenate(xw_parts, axis=1).astype(BF16)
        states = lax.dot_general(xw, Bg, _TN, preferred_element_type=F32)
        for hl in range(HG):
            hh = g * HG + hl
            r0 = g * GP + hl * P
            h_scr[r0:r0 + P, :] = edec[:, hh:hh + 1] * Sg[hl * P:(hl + 1) * P, :] + states[hl * P:(hl + 1) * P, :]

    z = z_ref[0]
    gn = DI // G
    outs = []
    for g in range(G):
        parts = y_parts[g * (HG // 2):(g + 1) * (HG // 2)]
        yg = jnp.concatenate(parts, axis=1) if len(parts) > 1 else parts[0]
        yg = yg * _silu(z[:, g * gn:(g + 1) * gn])
        ms = jnp.mean(yg * yg, axis=-1, keepdims=True)
        outs.append(yg * lax.rsqrt(ms + 1e-5))
    y_ref[0] = jnp.concatenate(outs, axis=1) * nw_ref[...]

    @pl.when(c == pl.num_programs(1) - 1)
    def _():
        hn_ref[0] = h_scr[...]


def _ssd(z, xbc, dtr, cbuf, h0, conv_w, conv_b, dt_bias, a_log, d_skip, norm_w, *, Lc, n_valid, G):
    B, L, DI = z.shape
    CD = xbc.shape[-1]
    _, H, P, N = h0.shape
    KW = conv_w.shape[0]
    assert P * 2 == LANES and (H // G) % 2 == 0 and H <= LANES and L % Lc == 0
    padl = lambda v: jnp.pad(v.astype(F32).reshape(1, -1), ((0, 0), (0, LANES - H)))
    seq = lambda b, c: (b, c, 0)
    per_b = lambda b, c: (b, 0, 0)
    fixed = lambda b, c: (0, 0)
    kern = functools.partial(_ssd_kernel, Lc=Lc, H=H, P=P, N=N, G=G, n_valid=n_valid, conv_w=KW)
    y, hn = pl.pallas_call(
        kern, grid=(B, L // Lc),
        in_specs=[pl.BlockSpec((1, Lc, DI), seq), pl.BlockSpec((1, Lc, CD), seq),
                  pl.BlockSpec((1, Lc, LANES), seq), pl.BlockSpec((1, SUBLANES, CD), per_b),
                  pl.BlockSpec((1, H * P, N), per_b),
                  pl.BlockSpec((KW, CD), fixed), pl.BlockSpec((1, CD), fixed),
                  pl.BlockSpec((1, LANES), fixed), pl.BlockSpec((1, LANES), fixed),
                  pl.BlockSpec((1, LANES), fixed), pl.BlockSpec((1, DI), fixed)],
        out_specs=[pl.BlockSpec((1, Lc, DI), seq), pl.BlockSpec((1, H * P, N), per_b)],
        out_shape=[jax.ShapeDtypeStruct((B, L, DI), F32), jax.ShapeDtypeStruct((B, H * P, N), F32)],
        scratch_shapes=[pltpu.VMEM((SUBLANES + Lc, CD), F32), pltpu.VMEM((H * P, N), F32)],
        compiler_params=_cparams("parallel", "arbitrary"),
    )(z, xbc, dtr, cbuf, h0.reshape(B, H * P, N), conv_w, conv_b.reshape(1, CD),
      padl(dt_bias), padl(a_log), padl(d_skip), norm_w.reshape(1, DI))
    return y, hn.reshape(B, H, P, N)


def _mamba_layer(X, dims, state_conv_i, state_ssm_i, params, ln_g, ln_b, alpha):
    Bp, Lp, Bs, Ls = dims
    w_in, conv_w, conv_b, dt_bias, a_log, d_skip, norm_w, w_out = params
    Tp = Bp * Lp
    DI = norm_w.shape[0]
    CD = conv_w.shape[1]
    _, H, P, N = state_ssm_i.shape
    G = (CD - DI) // (2 * N)
    KW = conv_w.shape[0]
    z = _matmul(X, w_in[:, :DI].astype(BF16))
    xbc = _matmul(X, w_in[:, DI:DI + CD].astype(BF16))
    dtr = _matmul(X, _pad_cols(w_in[:, DI + CD:], LANES).astype(BF16))
    args = (conv_w, conv_b, dt_bias, a_log, d_skip, norm_w)

    xbc_p = xbc[:Tp].reshape(Bp, Lp, CD)
    Lc = SSD_CHUNK if Lp % SSD_CHUNK == 0 else Lp
    yp, hp = _ssd(z[:Tp].reshape(Bp, Lp, DI), xbc_p, dtr[:Tp].reshape(Bp, Lp, LANES),
                  jnp.zeros((Bp, SUBLANES, CD), F32), jnp.zeros((Bp, H, P, N), F32), *args,
                  Lc=Lc, n_valid=Lc, G=G)
    conv_p = xbc_p[:, Lp - (KW - 1):]

    Lsp = -(-Ls // SUBLANES) * SUBLANES
    pad_rows = lambda v: jnp.pad(v.reshape(Bs, Ls, -1), ((0, 0), (0, Lsp - Ls), (0, 0)))
    T = Tp + Bs * Ls
    xbc_s = xbc[Tp:T].reshape(Bs, Ls, CD)
    cbuf_s = jnp.pad(state_conv_i, ((0, 0), (SUBLANES - (KW - 1), 0), (0, 0)))
    ys, hs = _ssd(pad_rows(z[Tp:T]), pad_rows(xbc_s), pad_rows(dtr[Tp:T]), cbuf_s, state_ssm_i, *args,
                  Lc=Lsp, n_valid=Ls, G=G)
    conv_s = jnp.concatenate([state_conv_i, xbc_s], axis=1)[:, -(KW - 1):]

    y = jnp.concatenate([yp.reshape(Tp, DI), ys[:, :Ls].reshape(Bs * Ls, DI),
                         jnp.zeros((X.shape[0] - T, DI), F32)], axis=0)
    X = _matmul_ln(y, w_out.astype(BF16), X, ln_g, ln_b, alpha)
    return X, hp, hs, conv_p, conv_s


def _topk_rows(s, k):
    n = s.shape[0]
    rows = lax.broadcasted_iota(I32, s.shape, 0)
    vals, idxs = [], []
    for _ in range(k):
        m = jnp.max(s, axis=0, keepdims=True)
        i = jnp.min(jnp.where(s == m, rows, n), axis=0, keepdims=True)
        vals.append(m)
        idxs.append(i)
        s = jnp.where(rows == i, -jnp.inf, s)
    return vals, idxs


def _peer_topk_kernel(q_ref, k_ref, i1_ref, i2_ref, g_ref, *, heads, half, topk, n_keys):
    k1 = k_ref[0]
    k2 = k_ref[1]
    tb = q_ref.shape[0]
    cand_ab = [(a, b) for a in range(topk) for b in range(topk // (a + 1))]
    n_pad = -len(cand_ab) % SUBLANES
    codes, gates = [], []
    for h in range(heads):
        q1 = q_ref[:, (2 * h) * half:(2 * h + 1) * half]
        q2 = q_ref[:, (2 * h + 1) * half:(2 * h + 2) * half]
        s1 = lax.dot_general(k1, q1, _NT, preferred_element_type=F32)
        s2 = lax.dot_general(k2, q2, _NT, preferred_element_type=F32)
        v1, i1 = _topk_rows(s1, topk)
        v2, i2 = _topk_rows(s2, topk)
        cand = jnp.concatenate([v1[a] + v2[b] for a, b in cand_ab] +
                               [jnp.full((n_pad, tb), -jnp.inf, F32)], axis=0)
        code = jnp.concatenate([i1[a] * n_keys + i2[b] for a, b in cand_ab] +
                               [jnp.zeros((n_pad, tb), I32)], axis=0)
        sc, pos = _topk_rows(cand, topk)
        crow = lax.broadcasted_iota(I32, cand.shape, 0)
        codes += [jnp.max(jnp.where(crow == p, code, -1), axis=0, keepdims=True) for p in pos]
        ex = jnp.exp(jnp.concatenate(sc, axis=0) - sc[0])
        gates.append(ex / jnp.sum(ex, axis=0, keepdims=True))
    code_t = jnp.concatenate(codes, axis=0).T
    i1_ref[...] = code_t // n_keys
    i2_ref[...] = code_t % n_keys
    g_ref[...] = jnp.concatenate(gates, axis=0).T


def _peer_w_kernel(i1_ref, i2_ref, g_ref, w_ref, *, n_keys):
    sub = lax.broadcasted_iota(I32, (n_keys, i1_ref.shape[1]), 0)

    def body(t, carry):
        i1 = i1_ref[pl.ds(t, 1), :]
        i2 = i2_ref[pl.ds(t, 1), :]
        g = g_ref[pl.ds(t, 1), :]
        onehot1 = jnp.where(sub == i1, 1.0, 0.0).astype(BF16)
        gated2 = jnp.where(sub == i2, g, 0.0).astype(BF16)
        w_ref[t] = lax.dot_general(onehot1, gated2, _NT, preferred_element_type=F32)
        return carry

    lax.fori_loop(0, i1_ref.shape[0], body, 0)


def _peer_dense_kernel(x_ref, u_ref, v_ref, w_ref, g_ref, b_ref, o_ref, acc_ref, *, alpha, nj):
    k = pl.program_id(1)

    @pl.when(k == 0)
    def _():
        acc_ref[...] = jnp.zeros_like(acc_ref)

    act = lax.dot_general(x_ref[...].astype(BF16), u_ref[...], _NT, preferred_element_type=F32)
    w = jnp.concatenate([w_ref[:, j, :] for j in range(nj)], axis=1)
    coef = (jax.nn.gelu(act) * w).astype(BF16)
    acc_ref[...] += jnp.dot(coef, v_ref[...], preferred_element_type=F32)

    @pl.when(k == pl.num_programs(1) - 1)
    def _():
        o_ref[...] = _deepnorm_ln(alpha * x_ref[...] + acc_ref[...], g_ref[...], b_ref[...])


def _peer_layer(X, w_q, sub_keys, u_tab, v_tab, ln_g, ln_b, alpha):
    T, D = X.shape
    _, n_keys, half = sub_keys.shape
    heads = w_q.shape[1] // (2 * half)
    nsel = heads * PEER_TOPK
    assert n_keys == LANES and nsel == LANES and half % LANES == 0
    q = _matmul(X, w_q.astype(BF16))

    tb = LANES
    tok = lambda i: (i, 0)
    sel_shape = jax.ShapeDtypeStruct((T, nsel), I32)
    i1, i2, gate = pl.pallas_call(
        functools.partial(_peer_topk_kernel, heads=heads, half=half, topk=PEER_TOPK, n_keys=n_keys),
        grid=(T // tb,),
        in_specs=[pl.BlockSpec((tb, q.shape[1]), tok), pl.BlockSpec((2, n_keys, half), lambda i: (0, 0, 0))],
        out_specs=[pl.BlockSpec((tb, nsel), tok)] * 3,
        out_shape=[sel_shape, sel_shape, jax.ShapeDtypeStruct((T, nsel), F32)],
        compiler_params=_cparams("parallel"))(q, sub_keys)

    w = pl.pallas_call(
        functools.partial(_peer_w_kernel, n_keys=n_keys), grid=(T // tb,),
        in_specs=[pl.BlockSpec((tb, nsel), tok)] * 3,
        out_specs=pl.BlockSpec((tb, n_keys, n_keys), lambda i: (i, 0, 0)),
        out_shape=jax.ShapeDtypeStruct((T, n_keys, n_keys), F32),
        compiler_params=_cparams("parallel"))(i1, i2, gate)

    td = _pick(T, (512, 256, 128))
    nj = SUBLANES
    et = nj * n_keys
    return pl.pallas_call(
        functools.partial(_peer_dense_kernel, alpha=alpha, nj=nj),
        grid=(T // td, n_keys // nj),
        in_specs=[pl.BlockSpec((td, D), lambda i, k: (i, 0)),
                  pl.BlockSpec((et, D), lambda i, k: (k, 0)),
                  pl.BlockSpec((et, D), lambda i, k: (k, 0)),
                  pl.BlockSpec((td, nj, n_keys), lambda i, k: (i, k, 0)),
                  pl.BlockSpec((1, D), lambda i, k: (0, 0)), pl.BlockSpec((1, D), lambda i, k: (0, 0))],
        out_specs=pl.BlockSpec((td, D), lambda i, k: (i, 0)),
        out_shape=jax.ShapeDtypeStruct((T, D), F32),
        scratch_shapes=[pltpu.VMEM((td, D), F32)],
        compiler_params=_cparams("parallel", "arbitrary"),
    )(X, u_tab.astype(BF16), v_tab.astype(BF16), w, ln_g.reshape(1, D), ln_b.reshape(1, D))


def _gather_kernel(pt_ref, page_ref, new_ref, kc_ref, vc_ref, ks_ref, vs_ref, *, n_pages, cw):
    del pt_ref
    p = pl.program_id(1)
    outs = (kc_ref, vc_ref, ks_ref, vs_ref)

    @pl.when(p < n_pages)
    def _():
        blk = page_ref[0]
        for r, ref in enumerate(outs):
            ref[0] = blk[:, r * cw:(r + 1) * cw].astype(ref.dtype)

    @pl.when(p == n_pages)
    def _():
        new = new_ref[0]
        ps = page_ref.shape[1]
        for r, ref in enumerate(outs):
            rows = jnp.concatenate([new[:, r * cw:(r + 1) * cw],
                                    jnp.zeros((ps - new.shape[0], cw), F32)], axis=0)
            ref[0] = rows.astype(ref.dtype)


def _gather_pages(pages, page_table, new_rows):
    NP, PS, C = pages.shape
    B, n_pages = page_table.shape
    cw = C // 4
    Lp = (n_pages + 1) * PS
    out_spec = pl.BlockSpec((1, PS, cw), lambda b, p, pt: (b, p, 0))
    return pl.pallas_call(
        functools.partial(_gather_kernel, n_pages=n_pages, cw=cw),
        grid_spec=pltpu.PrefetchScalarGridSpec(
            num_scalar_prefetch=1, grid=(B, n_pages + 1),
            in_specs=[pl.BlockSpec((1, PS, C), lambda b, p, pt: (pt[b, jnp.minimum(p, n_pages - 1)], 0, 0)),
                      pl.BlockSpec((1, new_rows.shape[1], C), lambda b, p, pt: (b, 0, 0))],
            out_specs=[out_spec] * 4),
        out_shape=[jax.ShapeDtypeStruct((B, Lp, cw), F32), jax.ShapeDtypeStruct((B, Lp, cw), F32),
                   jax.ShapeDtypeStruct((B, Lp, cw), BF16), jax.ShapeDtypeStruct((B, Lp, cw), BF16)],
        compiler_params=_cparams("parallel", "arbitrary"))(page_table, pages, new_rows)


def _compress(x, w_c, pos_c, nc_pad):
    B, L, C = x.shape
    blk, dh, _ = w_c.shape
    G = C // dh
    n = L // blk
    w_bd = jnp.einsum('lde,gh->lgdhe', w_c, jnp.eye(G, dtype=w_c.dtype)).reshape(blk * C, C)
    xbias = jnp.broadcast_to(pos_c[:, None, :], (blk, G, dh)).reshape(1, blk * C)
    out = _matmul(x.reshape(B * n, blk * C), w_bd.astype(BF16), xbias=xbias, tm_cap=128)
    return jnp.pad(out.reshape(B, n, C), ((0, 0), (0, nc_pad - n), (0, 0)))


def _masked_softmax(s, mask):
    s = jnp.where(mask, s, -jnp.inf)
    m = jnp.max(s, axis=-1, keepdims=True)
    m = jnp.where(m == -jnp.inf, 0.0, m)
    e = jnp.exp(s - m)
    return e / jnp.maximum(jnp.sum(e, axis=-1, keepdims=True), 1e-30)


def _split3_dot(x, m):
    x1 = x.astype(BF16)
    r1 = x - x1.astype(F32)
    x2 = r1.astype(BF16)
    x3 = (r1 - x2.astype(F32)).astype(BF16)
    d = lambda a: jnp.dot(a, m, preferred_element_type=F32)
    return d(x1) + d(x2) + d(x3)


def _nsa_kernel(q_ref, gt_ref, kc_ref, vc_ref, ks_ref, vs_ref, kw_ref, vw_ref, ex_ref, o_ref, *,
                nq, t0, tk, n_cmp, cmp_blk, n_sel, kvh, hg, dh, slopes, win_rows, w0, causal_tiles):
    i = pl.program_id(1)
    q0 = t0 + i * nq
    R = hg * nq
    NC = kc_ref.shape[1]
    Lk = ks_ref.shape[1]
    NSP = ex_ref.shape[0]
    scale = dh ** -0.5
    gates = jax.nn.sigmoid(gt_ref[0])

    def tq_like(shape):
        return q0 + lax.broadcasted_iota(I32, shape, 0)

    if causal_tiles:
        wstart = pl.multiple_of(jnp.maximum(q0 - WINDOW, 0), nq)
    else:
        wstart = 0

    for g in range(kvh):
        c0 = (g // 2) * 2 * dh
        lo = (g % 2) * dh
        qg = jnp.concatenate([q_ref[0, :, (g * hg + h) * 2 * dh:(g * hg + h + 1) * 2 * dh] for h in range(hg)],
                             axis=0)
        qg = (qg * scale).astype(BF16)

        def heads_softmax(s_all, dist, mask):
            ps = []
            for h in range(hg):
                s_h = s_all[h * nq:(h + 1) * nq] - slopes[g * hg + h] * dist.astype(F32)
                ps.append(_masked_softmax(s_h, mask))
            return ps

        kct = kc_ref[0, :, c0:c0 + 2 * dh].astype(BF16)
        vct = vc_ref[0, :, c0:c0 + 2 * dh].astype(BF16)
        s_all = lax.dot_general(qg, kct, _NT, preferred_element_type=F32)
        n_idx = lax.broadcasted_iota(I32, (nq, NC), 1)
        dist_c = tq_like((nq, NC)) - (n_idx * cmp_blk + (cmp_blk - 1))
        p_cmp = heads_softmax(s_all, dist_c, (dist_c >= 0) & (n_idx < n_cmp))
        o_cmp = jnp.dot(jnp.concatenate(p_cmp, axis=0).astype(BF16), vct, preferred_element_type=F32)

        psum = p_cmp[0]
        for h in range(1, hg):
            psum = psum + p_cmp[h]
        per_sel = SEL_BLOCK // cmp_blk
        pair = jnp.where(lax.broadcasted_iota(I32, (NC, NSP), 0) // per_sel ==
                         lax.broadcasted_iota(I32, (NC, NSP), 1), 1.0, 0.0).astype(BF16)
        imp = _split3_dot(psum, pair)
        blk = lax.broadcasted_iota(I32, (nq, NSP), 1)
        tqs = tq_like((nq, NSP))
        cur = tqs // SEL_BLOCK
        forced = (blk == 0) | (blk == cur) | (blk == cur - 1)
        score = jnp.where(forced, FORCE_SCORE, jnp.where(blk * SEL_BLOCK <= tqs, imp, -1.0))
        score = jnp.where(blk < n_sel, score, -2.0)
        selm = jnp.zeros((nq, NSP), F32)
        for _ in range(min(N_SEL, n_sel)):
            m = jnp.max(score, axis=-1, keepdims=True)
            idx = jnp.min(jnp.where(score == m, blk, NSP), axis=-1, keepdims=True)
            hit = blk == idx
            selm = jnp.where(hit, 1.0, selm)
            score = jnp.where(hit, -jnp.inf, score)
        selb = selm.astype(BF16)

        def sel_tile(kt, carry):
            m_run, l_run, acc = carry
            k0 = pl.multiple_of(kt * tk, tk)
            ktile = ks_ref[0, pl.ds(k0, tk), c0:c0 + 2 * dh]
            vtile = vs_ref[0, pl.ds(k0, tk), c0:c0 + 2 * dh]
            s_t = lax.dot_general(qg, ktile, _NT, preferred_element_type=F32)
            dist = tq_like((nq, tk)) - (k0 + lax.broadcasted_iota(I32, (nq, tk), 1))
            selx = jnp.dot(selb, ex_ref[:, pl.ds(k0, tk)], preferred_element_type=F32)
            mask = (dist >= 0) & (selx > 0.5)
            distf = dist.astype(F32)
            m_new, l_new, alphas, ps = [], [], [], []
            for h in range(hg):
                rows = slice(h * nq, (h + 1) * nq)
                s_h = jnp.where(mask, s_t[rows] - slopes[g * hg + h] * distf, -jnp.inf)
                m_h = jnp.maximum(m_run[rows], jnp.max(s_h, axis=-1, keepdims=True))
                m_safe = jnp.where(m_h == -jnp.inf, 0.0, m_h)
                a_h = jnp.exp(m_run[rows] - m_safe)
                p_h = jnp.exp(s_h - m_safe)
                m_new.append(m_h)
                l_new.append(a_h * l_run[rows] + jnp.sum(p_h, axis=-1, keepdims=True))
                alphas.append(a_h)
                ps.append(p_h)
            pv = jnp.dot(jnp.concatenate(ps, axis=0).astype(BF16), vtile, preferred_element_type=F32)
            acc = jnp.concatenate(alphas, axis=0) * acc + pv
            return jnp.concatenate(m_new, axis=0), jnp.concatenate(l_new, axis=0), acc

        init = (jnp.full((R, 1), -jnp.inf, F32), jnp.zeros((R, 1), F32), jnp.zeros((R, 2 * dh), F32))
        if causal_tiles:
            n_tiles = (q0 + nq + tk - 1) // tk
        else:
            n_tiles = Lk // tk
        _, l_fin, acc = lax.fori_loop(0, n_tiles, sel_tile, init)
        o_sel = acc / jnp.maximum(l_fin, 1e-30)

        kwt = kw_ref[0, pl.ds(wstart, win_rows), c0:c0 + 2 * dh]
        vwt = vw_ref[0, pl.ds(wstart, win_rows), c0:c0 + 2 * dh]
        s_all = lax.dot_general(qg, kwt, _NT, preferred_element_type=F32)
        dist_w = tq_like((nq, win_rows)) - (w0 + wstart + lax.broadcasted_iota(I32, (nq, win_rows), 1))
        p_win = heads_softmax(s_all, dist_w, (dist_w >= 0) & (dist_w < WINDOW))
        o_win = jnp.dot(jnp.concatenate(p_win, axis=0).astype(BF16), vwt, preferred_element_type=F32)

        lane = lax.broadcasted_iota(I32, (nq, 2 * dh), 1)
        keep = (lane >= lo) & (lane < lo + dh)
        for h in range(hg):
            hd = g * hg + h
            rows = slice(h * nq, (h + 1) * nq)
            o = (gates[:, 3 * hd:3 * hd + 1] * o_cmp[rows] + gates[:, 3 * hd + 1:3 * hd + 2] * o_sel[rows] +
                 gates[:, 3 * hd + 2:3 * hd + 3] * o_win[rows])
            o_ref[0, :, hd * 2 * dh:(hd + 1) * 2 * dh] = jnp.where(keep, o, 0.0)


def _nsa_attend(q, gt, kc, vc, ks, vs, kw, vw, cols, *, nq, t0, tk, n_cmp, cmp_blk, n_sel, kvh, dh,
                win_rows, w0, causal_tiles):
    B, Lq, QW = q.shape
    H = QW // (2 * dh)
    hg = H // kvh
    cw = kvh * dh
    Lk = ks.shape[1]
    NC = kc.shape[1]
    nsp = LANES
    assert n_sel <= nsp and Lk % tk == 0 and Lq % nq == 0 and dh * 2 == LANES
    expand = (lax.broadcasted_iota(I32, (nsp, Lk), 1) // SEL_BLOCK ==
              lax.broadcasted_iota(I32, (nsp, Lk), 0)).astype(BF16)
    slopes = tuple(2.0 ** (-8.0 * (h + 1) / H) for h in range(H))
    kern = functools.partial(_nsa_kernel, nq=nq, t0=t0, tk=tk, n_cmp=n_cmp, cmp_blk=cmp_blk, n_sel=n_sel,
                             kvh=kvh, hg=hg, dh=dh, slopes=slopes, win_rows=win_rows, w0=w0,
                             causal_tiles=causal_tiles)
    qmap = lambda b, i: (b, i, 0)
    kv_spec = lambda arr, c: pl.BlockSpec((1, arr.shape[1], cw), lambda b, i, c=c: (b, 0, c))
    return pl.pallas_call(
        kern, grid=(B, Lq // nq),
        in_specs=[pl.BlockSpec((1, nq, QW), qmap), pl.BlockSpec((1, nq, LANES), qmap),
                  kv_spec(kc, cols[0]), kv_spec(vc, cols[1]), kv_spec(ks, cols[2]), kv_spec(vs, cols[3]),
                  kv_spec(kw, cols[4]), kv_spec(vw, cols[5]),
                  pl.BlockSpec((nsp, Lk), lambda b, i: (0, 0))],
        out_specs=pl.BlockSpec((1, nq, QW), qmap),
        out_shape=jax.ShapeDtypeStruct((B, Lq, QW), F32),
        compiler_params=_cparams("parallel", "arbitrary"))(q, gt, kc, vc, ks, vs, kw, vw, expand)


def _head_tiles(w, H, kvh, dh, axis):
    hg = H // kvh
    parts = []
    for h in range(H):
        sl = [slice(None)] * w.ndim
        sl[axis] = slice(h * dh, (h + 1) * dh)
        piece = w[tuple(sl)]
        zero = jnp.zeros_like(piece)
        parts += [zero, piece] if ((h // hg) % 2) else [piece, zero]
    return jnp.concatenate(parts, axis=axis)


def kernel(x_prompt, x_sample, cache_kv_pages, cache_win, state_ssm, state_conv, page_table, ln_g, ln_b, m_w_in, m_conv_w, m_conv_b, m_dt_bias, m_a_log, m_d_skip, m_norm_w, m_w_out, w_kv_shared, w_cmp, pos_cmp, nsa_w_in, nsa_w_out, peer_w_q, peer_sub_keys, peer_u, peer_v):
    Bp, Lp, D = x_prompt.shape
    Bs, Ls, _ = x_sample.shape
    depth = ln_g.shape[0]
    n_a = m_w_in.shape[0]
    alpha = (2 * depth) ** DEPTH_ALPHA_POW
    Tp, Ts = Bp * Lp, Bs * Ls
    T = Tp + Ts
    Tpad = -(-T // 512) * 512
    X = jnp.concatenate([x_prompt.reshape(Tp, D), x_sample.reshape(Ts, D), jnp.zeros((Tpad - T, D), F32)], axis=0)
    dims = (Bp, Lp, Bs, Ls)

    NP, PS, n_rows, kvh, dh = cache_kv_pages.shape
    cw = kvh * dh
    n_pages = page_table.shape[1]
    past_len = n_pages * PS
    win_cache = cache_win.shape[1]
    cmp_blk = w_cmp.shape[1]
    H = nsa_w_in.shape[2] // (dh + 3)

    ssm_p, ssm_s, conv_p, conv_s = [], [], [], []
    for i in range(depth):
        if i < n_a:
            mp = (m_w_in[i], m_conv_w[i], m_conv_b[i], m_dt_bias[i], m_a_log[i], m_d_skip[i], m_norm_w[i], m_w_out[i])
            X, hp, hs, cp, cs = _mamba_layer(X, dims, state_conv[i], state_ssm[i], mp, ln_g[i, 0], ln_b[i, 0], alpha)
            ssm_p.append(hp)
            ssm_s.append(hs)
            conv_p.append(cp)
            conv_s.append(cs)
        else:
            if i == n_a:
                kv = _matmul(X, w_kv_shared.astype(BF16))
                kv_p = kv[:Tp].reshape(Bp, Lp, 6 * cw)
                kv_s = kv[Tp:T].reshape(Bs, Ls, 6 * cw)
                kvb_p = kv_p.astype(BF16)
                n_sel_p = -(-Lp // SEL_BLOCK)
                Lpp = n_sel_p * SEL_BLOCK
                padp = lambda v: jnp.pad(v, ((0, 0), (0, Lpp - Lp), (0, 0)))
                nc_p = -(-(Lpp // cmp_blk) // LANES) * LANES
                kc_p = _compress(padp(kv_p[:, :, 0:cw]), w_cmp[0], pos_cmp[0], nc_p)
                vc_p = _compress(padp(kv_p[:, :, cw:2 * cw]), w_cmp[1], pos_cmp[1], nc_p)
                tk_p = _pick(Lp, (512, 256, 128))
                new_rows = jnp.pad(kv_s[:, :, :4 * cw], ((0, 0), (0, SUBLANES - Ls), (0, 0)))
                kcr_s, vcr_s, ks_s, vs_s = _gather_pages(cache_kv_pages.reshape(NP, PS, n_rows * cw), page_table, new_rows)
                Lk_s = ks_s.shape[1]
                n_sel_s = -(-(past_len + Ls) // SEL_BLOCK)
                nc_s = -(-(Lk_s // cmp_blk) // LANES) * LANES
                kc_s = _compress(kcr_s, w_cmp[0], pos_cmp[0], nc_s)
                vc_s = _compress(vcr_s, w_cmp[1], pos_cmp[1], nc_s)
                win_full_s = jnp.concatenate([cache_win, kv_s[:, :, 4 * cw:].reshape(Bs, Ls, 2, kvh, dh)], axis=1)
                Lw = win_cache + Ls
                Lwp = -(-Lw // 16) * 16
                wflat = jnp.pad(win_full_s.reshape(Bs, Lw, 2 * cw), ((0, 0), (0, Lwp - Lw), (0, 0))).astype(BF16)
            j = i - n_a
            wq = _head_tiles(nsa_w_in[j][:, :H * dh], H, kvh, dh, axis=1).astype(BF16)
            wg = _pad_cols(nsa_w_in[j][:, H * dh:], LANES).astype(BF16)
            wo = _head_tiles(nsa_w_out[j], H, kvh, dh, axis=0).astype(BF16)
            q = _matmul(X, wq)
            gt = _matmul(X, wg)
            QW = q.shape[1]
            o_p = _nsa_attend(q[:Tp].reshape(Bp, Lp, QW), gt[:Tp].reshape(Bp, Lp, LANES),
                              kc_p, vc_p, kvb_p, kvb_p, kvb_p, kvb_p, (0, 0, 2, 3, 4, 5),
                              nq=Q_BLOCK, t0=0, tk=tk_p, n_cmp=Lpp // cmp_blk, cmp_blk=cmp_blk, n_sel=n_sel_p,
                              kvh=kvh, dh=dh, win_rows=min(WINDOW + Q_BLOCK, Lp), w0=0, causal_tiles=True)
            pad_q = lambda v: jnp.pad(v.reshape(Bs, Ls, -1), ((0, 0), (0, SUBLANES - Ls), (0, 0)))
            o_s = _nsa_attend(pad_q(q[Tp:T]), pad_q(gt[Tp:T]), kc_s, vc_s, ks_s, vs_s, wflat, wflat,
                              (0, 0, 0, 0, 0, 1),
                              nq=SUBLANES, t0=past_len, tk=Lk_s, n_cmp=-(-(past_len + Ls) // SEL_BLOCK) * (SEL_BLOCK // cmp_blk),
                              cmp_blk=cmp_blk, n_sel=n_sel_s, kvh=kvh, dh=dh, win_rows=Lwp,
                              w0=past_len - win_cache, causal_tiles=False)
            o = jnp.concatenate([o_p.reshape(Tp, QW), o_s[:, :Ls].reshape(Ts, QW), jnp.zeros((Tpad - T, QW), F32)], axis=0)
            X = _matmul_ln(o, wo, X, ln_g[i, 0], ln_b[i, 0], alpha)
        X = _peer_layer(X, peer_w_q[i], peer_sub_keys[i], peer_u[i], peer_v[i], ln_g[i, 1], ln_b[i, 1], alpha)

    rows_p = kv_p[:, :, :4 * cw].reshape(Bp, Lp, n_rows, kvh, dh)
    rows_s = kv_s[:, :, :4 * cw].reshape(Bs, Ls, n_rows, kvh, dh)
    win_p = kv_p[:, :, 4 * cw:].reshape(Bp, Lp, 2, kvh, dh)[:, -min(WINDOW, Lp):]
    return (X[:Tp].reshape(Bp, Lp, D), X[Tp:T].reshape(Bs, Ls, D), rows_p, rows_s, win_p,
            win_full_s[:, -win_cache:], jnp.stack(ssm_p), jnp.stack(ssm_s), jnp.stack(conv_p), jnp.stack(conv_s))
```

```python
import functools
import math

import jax
import jax.numpy as jnp
from jax import lax
from jax.experimental import pallas as pl
from jax.experimental.pallas import tpu as pltpu

F32 = jnp.float32
BF16 = jnp.bfloat16
I32 = jnp.int32

DEPTH_ALPHA_POW = 0.25
LN_EPS = 1e-5
SEL_BLOCK = 64
N_SEL = 16
WINDOW = 512
FORCE_SCORE = 1.0e4
PEER_TOPK = 16
SSD_CHUNK = 128
Q_BLOCK = 128
GELU_C0 = math.sqrt(2.0 / math.pi)
GELU_C1 = 0.044715 * GELU_C0
LOG2E = 1.0 / math.log(2.0)
W_TOKENS_PER_ITER = 16

LANES = 128
SUBLANES = 8
VMEM_LIMIT_BYTES = 56 * 1024 * 1024

_NT = (((1,), (1,)), ((), ()))
_TN = (((0,), (0,)), ((), ()))
_HI = lax.Precision.HIGHEST


def _cparams(*sem):
    return pltpu.CompilerParams(dimension_semantics=sem, vmem_limit_bytes=VMEM_LIMIT_BYTES)


def _pick(n, cands):
    for c in cands:
        if n % c == 0:
            return c
    raise ValueError(f"no tile in {cands} divides {n}")


def _pad_cols(w, n):
    return jnp.pad(w, ((0, 0), (0, n - w.shape[1])))


def _mm_kernel(x_ref, w_ref, o_ref):
    o_ref[...] = jnp.dot(x_ref[...].astype(BF16), w_ref[...],
                         preferred_element_type=F32).astype(o_ref.dtype)


def _mm_bias_kernel(x_ref, xb_ref, w_ref, o_ref):
    x = (x_ref[...] + xb_ref[...]).astype(BF16)
    o_ref[...] = jnp.dot(x, w_ref[...], preferred_element_type=F32).astype(o_ref.dtype)


def _matmul(x, w, *, xbias=None, out_dtype=F32, tm_cap=512):
    M, K = x.shape
    N = w.shape[1]
    tm = _pick(M, tuple(t for t in (512, 256, 128, 64, 32, 16, 8) if t <= tm_cap))
    tn = _pick(N, (1024, 512, 256, 128))
    x_spec = pl.BlockSpec((tm, K), lambda j, i: (i, 0))
    w_spec = pl.BlockSpec((K, tn), lambda j, i: (0, j))
    o_spec = pl.BlockSpec((tm, tn), lambda j, i: (i, j))
    if xbias is None:
        body, specs, args = _mm_kernel, [x_spec, w_spec], (x, w)
    else:
        b_spec = pl.BlockSpec((1, K), lambda j, i: (0, 0))
        body, specs, args = _mm_bias_kernel, [x_spec, b_spec, w_spec], (x, xbias, w)
    return pl.pallas_call(
        body, grid=(N // tn, M // tm), in_specs=specs, out_specs=o_spec,
        out_shape=jax.ShapeDtypeStruct((M, N), out_dtype),
        compiler_params=_cparams("parallel", "parallel"), name="proj")(*args)


def _deepnorm_ln(v, g, b):
    mu = jnp.mean(v, axis=-1, keepdims=True)
    d = v - mu
    var = jnp.mean(d * d, axis=-1, keepdims=True)
    return d * lax.rsqrt(var + LN_EPS) * g + b


def _mm_ln_kernel(x_ref, w_ref, r_ref, g_ref, b_ref, o_ref, *, alpha):
    y = jnp.dot(x_ref[...].astype(BF16), w_ref[...], preferred_element_type=F32)
    o_ref[...] = _deepnorm_ln(alpha * r_ref[...] + y, g_ref[...], b_ref[...])


def _matmul_ln(x, w, resid, g, b, alpha):
    M, K = x.shape
    D = w.shape[1]
    tm = _pick(M, (512, 256, 128, 64, 32, 16, 8))
    row = lambda i: (i, 0)
    fixed = lambda i: (0, 0)
    return pl.pallas_call(
        functools.partial(_mm_ln_kernel, alpha=alpha), grid=(M // tm,),
        in_specs=[pl.BlockSpec((tm, K), row), pl.BlockSpec((K, D), fixed),
                  pl.BlockSpec((tm, D), row), pl.BlockSpec((1, D), fixed), pl.BlockSpec((1, D), fixed)],
        out_specs=pl.BlockSpec((tm, D), row),
        out_shape=jax.ShapeDtypeStruct((M, D), F32),
        compiler_params=_cparams("parallel"), name="proj_ln")(x, w, resid, g.reshape(1, D), b.reshape(1, D))


def _softplus(x):
    return jnp.maximum(x, 0.0) + jnp.log1p(jnp.exp(-jnp.abs(x)))


def _silu(x):
    return x * jax.nn.sigmoid(x)


def _ssd_kernel(z_ref, xbc_ref, dtr_ref, cbuf_ref, h0_ref, cw_ref, cb_ref, dtb_ref, alog_ref,
                dsk_ref, nw_ref, y_ref, hn_ref, xf_scr, h_scr, *, Lc, H, P, N, G, n_valid, conv_w):
    c = pl.program_id(1)
    DI = H * P
    HG = H // G
    GP = HG * P
    tail = xf_scr.shape[0] - Lc

    @pl.when(c == 0)
    def _():
        xf_scr[0:tail, :] = cbuf_ref[0]
        h_scr[...] = h0_ref[0]

    xf_scr[tail:tail + Lc, :] = xbc_ref[0]
    conv = cb_ref[...]
    for k in range(conv_w):
        conv = conv + xf_scr[pl.ds(tail - (conv_w - 1) + k, Lc), :] * cw_ref[k:k + 1, :]
    conv = _silu(conv)
    xf_scr[0:tail, :] = xf_scr[pl.ds(Lc, tail), :]

    xs = conv[:, :DI]
    Bm = conv[:, DI:DI + G * N]
    Cm = conv[:, DI + G * N:]

    lane = lax.broadcasted_iota(I32, (Lc, LANES), 1)
    rowi = lax.broadcasted_iota(I32, (Lc, LANES), 0)
    dt = jnp.where((lane < H) & (rowi < n_valid), _softplus(dtr_ref[0] + dtb_ref[...]), 0.0)
    a = dt * (-jnp.exp(alog_ref[...]))
    ti = lax.broadcasted_iota(I32, (Lc, Lc), 0)
    si = lax.broadcasted_iota(I32, (Lc, Lc), 1)
    causal = ti >= si
    tril = jnp.where(causal, 1.0, 0.0).astype(F32)
    eye = jnp.where(lax.broadcasted_iota(I32, (LANES, LANES), 0) ==
                    lax.broadcasted_iota(I32, (LANES, LANES), 1), 1.0, 0.0).astype(F32)
    acum = jnp.dot(tril, a, precision=_HI, preferred_element_type=F32)
    acum_t = lax.dot_general(eye, acum, _NT, precision=_HI, preferred_element_type=F32)
    dt_t = lax.dot_general(eye, dt, _NT, precision=_HI, preferred_element_type=F32)
    alast = acum[Lc - 1:Lc, :]
    wend = jnp.exp(alast - acum) * dt
    ea = jnp.exp(acum)
    edec = jnp.exp(alast)
    dsk = dsk_ref[...]
    first_half = lax.broadcasted_iota(I32, (Lc, 2 * P), 1) < P

    def pair(v, h):
        rows = v.shape[0]
        return jnp.where(first_half[:rows], jnp.broadcast_to(v[:, h:h + 1], (rows, 2 * P)),
                         jnp.broadcast_to(v[:, h + 1:h + 2], (rows, 2 * P)))

    y_parts = []
    for g in range(G):
        Bg = Bm[:, g * N:(g + 1) * N].astype(BF16)
        Cg = Cm[:, g * N:(g + 1) * N].astype(BF16)
        cb = lax.dot_general(Cg, Bg, _NT, preferred_element_type=F32)
        Sg = h_scr[g * GP:(g + 1) * GP, :]
        yoff = lax.dot_general(Cg, Sg.astype(BF16), _NT, preferred_element_type=F32)
        xw_parts = []
        for pr in range(HG // 2):
            h = g * HG + 2 * pr
            xs_pair = xs[:, h * P:(h + 2) * P]
            xs_pair_b = xs_pair.astype(BF16)
            outs = []
            for hh in (h, h + 1):
                seg = acum[:, hh:hh + 1] - acum_t[hh:hh + 1, :]
                dec = jnp.exp(jnp.where(causal, seg, -jnp.inf))
                mix = cb * dec * dt_t[hh:hh + 1, :]
                outs.append(jnp.dot(mix.astype(BF16), xs_pair_b, preferred_element_type=F32))
            ydiag = jnp.where(first_half, outs[0], outs[1])
            y_pair = ydiag + yoff[:, pr * 2 * P:(pr + 1) * 2 * P] * pair(ea, h) + pair(dsk, h) * xs_pair
            y_parts.append(y_pair)
            xw_parts.append(xs_pair * pair(wend, h))
        xw = jnp.concatenate(xw_parts, axis=1).astype(BF16)
        states = lax.dot_general(xw, Bg, _TN, preferred_element_type=F32)
        for hl in range(HG):
            hh = g * HG + hl
            r0 = g * GP + hl * P
            h_scr[r0:r0 + P, :] = edec[:, hh:hh + 1] * Sg[hl * P:(hl + 1) * P, :] + states[hl * P:(hl + 1) * P, :]

    z = z_ref[0]
    gn = DI // G
    outs = []
    for g in range(G):
        parts = y_parts[g * (HG // 2):(g + 1) * (HG // 2)]
        yg = jnp.concatenate(parts, axis=1) if len(parts) > 1 else parts[0]
        yg = yg * _silu(z[:, g * gn:(g + 1) * gn])
        ms = jnp.mean(yg * yg, axis=-1, keepdims=True)
        outs.append(yg * lax.rsqrt(ms + 1e-5))
    y_ref[0] = jnp.concatenate(outs, axis=1) * nw_ref[...]

    @pl.when(c == pl.num_programs(1) - 1)
    def _():
        hn_ref[0] = h_scr[...]


def _ssd(z, xbc, dtr, cbuf, h0, conv_w, conv_b, dt_bias, a_log, d_skip, norm_w, *, Lc, n_valid, G):
    B, L, DI = z.shape
    CD = xbc.shape[-1]
    _, H, P, N = h0.shape
    KW = conv_w.shape[0]
    assert P * 2 == LANES and (H // G) % 2 == 0 and H <= LANES and L % Lc == 0
    padl = lambda v: jnp.pad(v.astype(F32).reshape(1, -1), ((0, 0), (0, LANES - H)))
    seq = lambda b, c: (b, c, 0)
    per_b = lambda b, c: (b, 0, 0)
    fixed = lambda b, c: (0, 0)
    kern = functools.partial(_ssd_kernel, Lc=Lc, H=H, P=P, N=N, G=G, n_valid=n_valid, conv_w=KW)
    y, hn = pl.pallas_call(
        kern, grid=(B, L // Lc),
        in_specs=[pl.BlockSpec((1, Lc, DI), seq), pl.BlockSpec((1, Lc, CD), seq),
                  pl.BlockSpec((1, Lc, LANES), seq), pl.BlockSpec((1, SUBLANES, CD), per_b),
                  pl.BlockSpec((1, H * P, N), per_b),
                  pl.BlockSpec((KW, CD), fixed), pl.BlockSpec((1, CD), fixed),
                  pl.BlockSpec((1, LANES), fixed), pl.BlockSpec((1, LANES), fixed),
                  pl.BlockSpec((1, LANES), fixed), pl.BlockSpec((1, DI), fixed)],
        out_specs=[pl.BlockSpec((1, Lc, DI), seq), pl.BlockSpec((1, H * P, N), per_b)],
        out_shape=[jax.ShapeDtypeStruct((B, L, DI), F32), jax.ShapeDtypeStruct((B, H * P, N), F32)],
        scratch_shapes=[pltpu.VMEM((SUBLANES + Lc, CD), F32), pltpu.VMEM((H * P, N), F32)],
        compiler_params=_cparams("parallel", "arbitrary"), name="ssd",
    )(z, xbc, dtr, cbuf, h0.reshape(B, H * P, N), conv_w, conv_b.reshape(1, CD),
      padl(dt_bias), padl(a_log), padl(d_skip), norm_w.reshape(1, DI))
    return y, hn.reshape(B, H, P, N)


def _mamba_layer(X, dims, state_conv_i, state_ssm_i, params, ln_g, ln_b, alpha):
    Bp, Lp, Bs, Ls = dims
    w_in, conv_w, conv_b, dt_bias, a_log, d_skip, norm_w, w_out = params
    Tp = Bp * Lp
    DI = norm_w.shape[0]
    CD = conv_w.shape[1]
    _, H, P, N = state_ssm_i.shape
    G = (CD - DI) // (2 * N)
    KW = conv_w.shape[0]
    z = _matmul(X, w_in[:, :DI].astype(BF16))
    xbc = _matmul(X, w_in[:, DI:DI + CD].astype(BF16))
    dtr = _matmul(X, _pad_cols(w_in[:, DI + CD:], LANES).astype(BF16))
    args = (conv_w, conv_b, dt_bias, a_log, d_skip, norm_w)

    xbc_p = xbc[:Tp].reshape(Bp, Lp, CD)
    Lc = SSD_CHUNK if Lp % SSD_CHUNK == 0 else Lp
    yp, hp = _ssd(z[:Tp].reshape(Bp, Lp, DI), xbc_p, dtr[:Tp].reshape(Bp, Lp, LANES),
                  jnp.zeros((Bp, SUBLANES, CD), F32), jnp.zeros((Bp, H, P, N), F32), *args,
                  Lc=Lc, n_valid=Lc, G=G)
    conv_p = xbc_p[:, Lp - (KW - 1):]

    Lsp = -(-Ls // SUBLANES) * SUBLANES
    pad_rows = lambda v: jnp.pad(v.reshape(Bs, Ls, -1), ((0, 0), (0, Lsp - Ls), (0, 0)))
    T = Tp + Bs * Ls
    xbc_s = xbc[Tp:T].reshape(Bs, Ls, CD)
    cbuf_s = jnp.pad(state_conv_i, ((0, 0), (SUBLANES - (KW - 1), 0), (0, 0)))
    ys, hs = _ssd(pad_rows(z[Tp:T]), pad_rows(xbc_s), pad_rows(dtr[Tp:T]), cbuf_s, state_ssm_i, *args,
                  Lc=Lsp, n_valid=Ls, G=G)
    conv_s = jnp.concatenate([state_conv_i, xbc_s], axis=1)[:, -(KW - 1):]

    y = jnp.concatenate([yp.reshape(Tp, DI), ys[:, :Ls].reshape(Bs * Ls, DI),
                         jnp.zeros((X.shape[0] - T, DI), F32)], axis=0)
    X = _matmul_ln(y, w_out.astype(BF16), X, ln_g, ln_b, alpha)
    return X, hp, hs, conv_p, conv_s


def _topk_rows(s, k):
    n = s.shape[0]
    rows = lax.broadcasted_iota(I32, s.shape, 0)
    vals, idxs = [], []
    for _ in range(k):
        m = jnp.max(s, axis=0, keepdims=True)
        i = jnp.min(jnp.where(s == m, rows, n), axis=0, keepdims=True)
        vals.append(m)
        idxs.append(i)
        s = jnp.where(rows == i, -jnp.inf, s)
    return vals, idxs


def _peer_topk_kernel(q_ref, k_ref, i1_ref, i2_ref, g_ref, *, heads, half, topk, n_keys):
    k1 = k_ref[0]
    k2 = k_ref[1]
    tb = q_ref.shape[0]
    cand_ab = [(a, b) for a in range(topk) for b in range(topk // (a + 1))]
    n_pad = -len(cand_ab) % SUBLANES
    codes, gates = [], []
    for h in range(heads):
        q1 = q_ref[:, (2 * h) * half:(2 * h + 1) * half]
        q2 = q_ref[:, (2 * h + 1) * half:(2 * h + 2) * half]
        s1 = lax.dot_general(k1, q1, _NT, preferred_element_type=F32)
        s2 = lax.dot_general(k2, q2, _NT, preferred_element_type=F32)
        v1, i1 = _topk_rows(s1, topk)
        v2, i2 = _topk_rows(s2, topk)
        cand = jnp.concatenate([v1[a] + v2[b] for a, b in cand_ab] +
                               [jnp.full((n_pad, tb), -jnp.inf, F32)], axis=0)
        code = jnp.concatenate([i1[a] * n_keys + i2[b] for a, b in cand_ab] +
                               [jnp.zeros((n_pad, tb), I32)], axis=0)
        sc, pos = _topk_rows(cand, topk)
        crow = lax.broadcasted_iota(I32, cand.shape, 0)
        codes += [jnp.max(jnp.where(crow == p, code, -1), axis=0, keepdims=True) for p in pos]
        ex = jnp.exp(jnp.concatenate(sc, axis=0) - sc[0])
        gates.append(ex / jnp.sum(ex, axis=0, keepdims=True))
    code_t = jnp.concatenate(codes, axis=0).T
    i1_ref[...] = code_t // n_keys
    i2_ref[...] = code_t % n_keys
    g_ref[...] = jnp.concatenate(gates, axis=0).T


def _peer_w_kernel(i1_ref, i2_ref, g_ref, w_ref, *, n_keys, nj):
    sub = lax.broadcasted_iota(I32, (n_keys, i1_ref.shape[1]), 0)

    def body(tb, carry):
        for u in range(W_TOKENS_PER_ITER):
            t = tb * W_TOKENS_PER_ITER + u
            i1 = i1_ref[pl.ds(t, 1), :]
            i2 = i2_ref[pl.ds(t, 1), :]
            g = g_ref[pl.ds(t, 1), :]
            onehot1 = jnp.where(sub == i1, 1.0, 0.0).astype(BF16)
            gated2 = jnp.where(sub == i2, g, 0.0).astype(BF16)
            w_t = lax.dot_general(onehot1, gated2, _NT, preferred_element_type=F32)
            r0 = pl.multiple_of(t * nj, nj)
            for k in range(n_keys // nj):
                w_ref[0, k, pl.ds(r0, nj), :] = w_t[k * nj:(k + 1) * nj, :]
        return carry

    lax.fori_loop(0, i1_ref.shape[0] // W_TOKENS_PER_ITER, body, 0)


def _gelu_tanh(x):
    hx = 0.5 * x
    inner = x * (GELU_C0 + GELU_C1 * (x * x))
    return hx * jnp.tanh(inner) + hx


def _peer_dense_kernel(x_ref, u_ref, v_ref, w_ref, g_ref, b_ref, o_ref, acc_ref, xb_ref, *, alpha, nj):
    k = pl.program_id(1)
    td = x_ref.shape[0]

    @pl.when(k == 0)
    def _():
        acc_ref[...] = jnp.zeros_like(acc_ref)
        xb_ref[...] = x_ref[...].astype(BF16)

    act = lax.dot_general(xb_ref[...], u_ref[...], _NT, preferred_element_type=F32)
    w = jnp.concatenate([w_ref[0, 0, pl.ds(j, td, stride=nj), :] for j in range(nj)], axis=1)
    coef = (_gelu_tanh(act) * w).astype(BF16)
    acc_ref[...] += jnp.dot(coef, v_ref[...], preferred_element_type=F32)

    @pl.when(k == pl.num_programs(1) - 1)
    def _():
        o_ref[...] = _deepnorm_ln(alpha * x_ref[...] + acc_ref[...], g_ref[...], b_ref[...])


def _peer_layer(X, w_q, sub_keys, u_tab, v_tab, ln_g, ln_b, alpha):
    T, D = X.shape
    _, n_keys, half = sub_keys.shape
    heads = w_q.shape[1] // (2 * half)
    nsel = heads * PEER_TOPK
    assert n_keys == LANES and nsel == LANES and half % LANES == 0
    q = _matmul(X, w_q.astype(BF16))

    tb = LANES
    tok = lambda i: (i, 0)
    sel_shape = jax.ShapeDtypeStruct((T, nsel), I32)
    i1, i2, gate = pl.pallas_call(
        functools.partial(_peer_topk_kernel, heads=heads, half=half, topk=PEER_TOPK, n_keys=n_keys),
        grid=(T // tb,),
        in_specs=[pl.BlockSpec((tb, q.shape[1]), tok), pl.BlockSpec((2, n_keys, half), lambda i: (0, 0, 0))],
        out_specs=[pl.BlockSpec((tb, nsel), tok)] * 3,
        out_shape=[sel_shape, sel_shape, jax.ShapeDtypeStruct((T, nsel), F32)],
        compiler_params=_cparams("parallel"), name="peer_topk")(q, sub_keys)

    td = _pick(T, (768, 512, 256, 128))
    nj = SUBLANES
    et = nj * n_keys
    sub_blocks = td // tb
    w = pl.pallas_call(
        functools.partial(_peer_w_kernel, n_keys=n_keys, nj=nj), grid=(T // tb,),
        in_specs=[pl.BlockSpec((tb, nsel), tok)] * 3,
        out_specs=pl.BlockSpec((1, n_keys // nj, tb * nj, n_keys),
                               lambda i: (i // sub_blocks, 0, i % sub_blocks, 0)),
        out_shape=jax.ShapeDtypeStruct((T // td, n_keys // nj, td * nj, n_keys), F32),
        compiler_params=_cparams("parallel"), name="peer_w")(i1, i2, gate)

    return pl.pallas_call(
        functools.partial(_peer_dense_kernel, alpha=alpha, nj=nj),
        grid=(T // td, n_keys // nj),
        in_specs=[pl.BlockSpec((td, D), lambda i, k: (i, 0)),
                  pl.BlockSpec((et, D), lambda i, k: (k, 0)),
                  pl.BlockSpec((et, D), lambda i, k: (k, 0)),
                  pl.BlockSpec((1, 1, td * nj, n_keys), lambda i, k: (i, k, 0, 0)),
                  pl.BlockSpec((1, D), lambda i, k: (0, 0)), pl.BlockSpec((1, D), lambda i, k: (0, 0))],
        out_specs=pl.BlockSpec((td, D), lambda i, k: (i, 0)),
        out_shape=jax.ShapeDtypeStruct((T, D), F32),
        scratch_shapes=[pltpu.VMEM((td, D), F32), pltpu.VMEM((td, D), BF16)],
        compiler_params=_cparams("parallel", "arbitrary"), name="peer_dense",
    )(X, u_tab.astype(BF16), v_tab.astype(BF16), w, ln_g.reshape(1, D), ln_b.reshape(1, D))


def _gather_kernel(pt_ref, page_ref, new_ref, kc_ref, vc_ref, ks_ref, vs_ref, *, n_pages, cw):
    del pt_ref
    p = pl.program_id(1)
    outs = (kc_ref, vc_ref, ks_ref, vs_ref)

    @pl.when(p < n_pages)
    def _():
        blk = page_ref[0]
        for r, ref in enumerate(outs):
            ref[0] = blk[:, r * cw:(r + 1) * cw].astype(ref.dtype)

    @pl.when(p == n_pages)
    def _():
        new = new_ref[0]
        ps = page_ref.shape[1]
        for r, ref in enumerate(outs):
            rows = jnp.concatenate([new[:, r * cw:(r + 1) * cw],
                                    jnp.zeros((ps - new.shape[0], cw), F32)], axis=0)
            ref[0] = rows.astype(ref.dtype)


def _gather_pages(pages, page_table, new_rows):
    NP, PS, C = pages.shape
    B, n_pages = page_table.shape
    cw = C // 4
    Lp = (n_pages + 1) * PS
    out_spec = pl.BlockSpec((1, PS, cw), lambda b, p, pt: (b, p, 0))
    return pl.pallas_call(
        functools.partial(_gather_kernel, n_pages=n_pages, cw=cw),
        grid_spec=pltpu.PrefetchScalarGridSpec(
            num_scalar_prefetch=1, grid=(B, n_pages + 1),
            in_specs=[pl.BlockSpec((1, PS, C), lambda b, p, pt: (pt[b, jnp.minimum(p, n_pages - 1)], 0, 0)),
                      pl.BlockSpec((1, new_rows.shape[1], C), lambda b, p, pt: (b, 0, 0))],
            out_specs=[out_spec] * 4),
        out_shape=[jax.ShapeDtypeStruct((B, Lp, cw), F32), jax.ShapeDtypeStruct((B, Lp, cw), F32),
                   jax.ShapeDtypeStruct((B, Lp, cw), BF16), jax.ShapeDtypeStruct((B, Lp, cw), BF16)],
        compiler_params=_cparams("parallel", "arbitrary"), name="kv_gather")(page_table, pages, new_rows)


def _compress(x, w_c, pos_c, nc_pad):
    B, L, C = x.shape
    blk, dh, _ = w_c.shape
    G = C // dh
    n = L // blk
    w_bd = jnp.einsum('lde,gh->lgdhe', w_c, jnp.eye(G, dtype=w_c.dtype)).reshape(blk * C, C)
    xbias = jnp.broadcast_to(pos_c[:, None, :], (blk, G, dh)).reshape(1, blk * C)
    out = _matmul(x.reshape(B * n, blk * C), w_bd.astype(BF16), xbias=xbias, tm_cap=128)
    return jnp.pad(out.reshape(B, n, C), ((0, 0), (0, nc_pad - n), (0, 0)))


def _lane_tiles(x):
    return [x[:, j * LANES:(j + 1) * LANES] for j in range(x.shape[1] // LANES)]


def _row_max(x):
    return jnp.max(functools.reduce(jnp.maximum, _lane_tiles(x)), axis=-1, keepdims=True)


def _row_sum(x):
    return jnp.sum(functools.reduce(jnp.add, _lane_tiles(x)), axis=-1, keepdims=True)


def _split3_dot(x, m):
    x1 = x.astype(BF16)
    r1 = x - x1.astype(F32)
    x2 = r1.astype(BF16)
    x3 = (r1 - x2.astype(F32)).astype(BF16)
    d = lambda a: jnp.dot(a, m, preferred_element_type=F32)
    return d(x1) + d(x2) + d(x3)


def _nsa_kernel(q_ref, gt_ref, kc_ref, vc_ref, ks_ref, vs_ref, kw_ref, vw_ref, ex_ref, o_ref, *,
                nq, t0, tk, n_cmp, cmp_blk, n_sel, kvh, hg, dh, slopes, win_rows, w0, causal_tiles):
    i = pl.program_id(1)
    q0 = t0 + i * nq
    R = hg * nq
    NC = kc_ref.shape[1]
    Lk = ks_ref.shape[1]
    NSP = ex_ref.shape[0]
    qscale = dh ** -0.5 * LOG2E
    slopes2 = tuple(s * LOG2E for s in slopes)
    gates = jax.nn.sigmoid(gt_ref[0])

    def tq_like(shape):
        return q0 + lax.broadcasted_iota(I32, shape, 0)

    if causal_tiles:
        wstart = pl.multiple_of(jnp.maximum(q0 - WINDOW, 0), nq)
    else:
        wstart = 0

    def col(g):
        return (g // 2) * 2 * dh

    def branch(qg, k_t, v_t, distm, g):
        s_all = lax.dot_general(qg, k_t, _NT, preferred_element_type=F32)
        ps = []
        for h in range(hg):
            s2 = s_all[h * nq:(h + 1) * nq] - slopes2[g * hg + h] * distm
            m = _row_max(s2)
            e = jnp.exp2(s2 - jnp.where(m == -jnp.inf, 0.0, m))
            ps.append(e / jnp.maximum(_row_sum(e), 1e-30))
        o = jnp.dot(jnp.concatenate(ps, axis=0).astype(BF16), v_t, preferred_element_type=F32)
        return ps, o

    n_idx = lax.broadcasted_iota(I32, (nq, NC), 1)
    dist_c = tq_like((nq, NC)) - (n_idx * cmp_blk + (cmp_blk - 1))
    distm_c = jnp.where((dist_c >= 0) & (n_idx < n_cmp), dist_c.astype(F32), jnp.inf)
    dist_w = tq_like((nq, win_rows)) - (w0 + wstart + lax.broadcasted_iota(I32, (nq, win_rows), 1))
    distm_w = jnp.where((dist_w >= 0) & (dist_w < WINDOW), dist_w.astype(F32), jnp.inf)
    per_sel = SEL_BLOCK // cmp_blk
    pair = jnp.where(lax.broadcasted_iota(I32, (NC, NSP), 0) // per_sel ==
                     lax.broadcasted_iota(I32, (NC, NSP), 1), 1.0, 0.0).astype(BF16)
    blk = lax.broadcasted_iota(I32, (nq, NSP), 1)
    tqs = tq_like((nq, NSP))
    cur = tqs // SEL_BLOCK
    forced = (blk == 0) | (blk == cur) | (blk == cur - 1)
    sel_valid = blk * SEL_BLOCK <= tqs

    qgs, o_cmps, o_wins, scores = [], [], [], []
    for g in range(kvh):
        c0 = col(g)
        qg = jnp.concatenate([q_ref[0, :, (g * hg + h) * 2 * dh:(g * hg + h + 1) * 2 * dh] for h in range(hg)],
                             axis=0)
        qg = (qg * qscale).astype(BF16)
        qgs.append(qg)
        p_cmp, o_cmp = branch(qg, kc_ref[0, :, c0:c0 + 2 * dh].astype(BF16),
                              vc_ref[0, :, c0:c0 + 2 * dh].astype(BF16), distm_c, g)
        o_cmps.append(o_cmp)
        psum = p_cmp[0]
        for h in range(1, hg):
            psum = psum + p_cmp[h]
        imp = _split3_dot(psum, pair)
        score = jnp.where(forced, FORCE_SCORE, jnp.where(sel_valid, imp, -1.0))
        scores.append(jnp.where(blk < n_sel, score, -2.0))
        _, o_win = branch(qg, kw_ref[0, pl.ds(wstart, win_rows), c0:c0 + 2 * dh],
                          vw_ref[0, pl.ds(wstart, win_rows), c0:c0 + 2 * dh], distm_w, g)
        o_wins.append(o_win)

    score = jnp.concatenate(scores, axis=0)
    n_top = min(N_SEL, n_sel)
    if score.shape[0] % LANES == 0 and NSP == LANES:
        st = score.T
        brow = lax.broadcasted_iota(I32, st.shape, 0)
        sel_t = jnp.zeros(st.shape, F32)
        for _ in range(n_top):
            m = jnp.max(st, axis=0, keepdims=True)
            idx = jnp.min(jnp.where(st == m, brow, NSP), axis=0, keepdims=True)
            hit = brow == idx
            sel_t = jnp.where(hit, 1.0, sel_t)
            st = jnp.where(hit, -jnp.inf, st)
        selm = sel_t.T
    else:
        bcol = lax.broadcasted_iota(I32, score.shape, 1)
        selm = jnp.zeros(score.shape, F32)
        for _ in range(n_top):
            m = jnp.max(score, axis=-1, keepdims=True)
            idx = jnp.min(jnp.where(score == m, bcol, NSP), axis=-1, keepdims=True)
            hit = bcol == idx
            selm = jnp.where(hit, 1.0, selm)
            score = jnp.where(hit, -jnp.inf, score)
    selb = selm.astype(BF16)

    def sel_tile(kt, carry):
        k0 = pl.multiple_of(kt * tk, tk)
        dist = tq_like((nq, tk)) - (k0 + lax.broadcasted_iota(I32, (nq, tk), 1))
        causal = dist >= 0
        distf = dist.astype(F32)
        expand = ex_ref[:, pl.ds(k0, tk)]
        new = []
        for g in range(kvh):
            m_run, l_run, acc = carry[g]
            c0 = col(g)
            s_t = lax.dot_general(qgs[g], ks_ref[0, pl.ds(k0, tk), c0:c0 + 2 * dh], _NT,
                                  preferred_element_type=F32)
            selx = jnp.dot(selb[g * nq:(g + 1) * nq], expand, preferred_element_type=F32)
            distm = jnp.where(causal & (selx > 0.5), distf, jnp.inf)
            m_new, l_new, alphas, ps = [], [], [], []
            for h in range(hg):
                rows = slice(h * nq, (h + 1) * nq)
                s2 = s_t[rows] - slopes2[g * hg + h] * distm
                m_h = jnp.maximum(m_run[rows], _row_max(s2))
                m_safe = jnp.where(m_h == -jnp.inf, 0.0, m_h)
                a_h = jnp.exp2(m_run[rows] - m_safe)
                p_h = jnp.exp2(s2 - m_safe)
                m_new.append(m_h)
                l_new.append(a_h * l_run[rows] + _row_sum(p_h))
                alphas.append(a_h)
                ps.append(p_h)
            pv = jnp.dot(jnp.concatenate(ps, axis=0).astype(BF16), vs_ref[0, pl.ds(k0, tk), c0:c0 + 2 * dh],
                         preferred_element_type=F32)
            new.append((jnp.concatenate(m_new, axis=0), jnp.concatenate(l_new, axis=0),
                        jnp.concatenate(alphas, axis=0) * acc + pv))
        return tuple(new)

    init = tuple((jnp.full((R, 1), -jnp.inf, F32), jnp.zeros((R, 1), F32), jnp.zeros((R, 2 * dh), F32))
                 for _ in range(kvh))
    if causal_tiles:
        n_tiles = (q0 + nq + tk - 1) // tk
    else:
        n_tiles = Lk // tk
    fin = lax.fori_loop(0, n_tiles, sel_tile, init)

    lane = lax.broadcasted_iota(I32, (nq, 2 * dh), 1)
    for g in range(kvh):
        lo = (g % 2) * dh
        keep = (lane >= lo) & (lane < lo + dh)
        _, l_fin, acc = fin[g]
        o_sel = acc / jnp.maximum(l_fin, 1e-30)
        for h in range(hg):
            hd = g * hg + h
            rows = slice(h * nq, (h + 1) * nq)
            o = (gates[:, 3 * hd:3 * hd + 1] * o_cmps[g][rows] + gates[:, 3 * hd + 1:3 * hd + 2] * o_sel[rows] +
                 gates[:, 3 * hd + 2:3 * hd + 3] * o_wins[g][rows])
            o_ref[0, :, hd * 2 * dh:(hd + 1) * 2 * dh] = jnp.where(keep, o, 0.0)


def _nsa_attend(q, gt, kc, vc, ks, vs, kw, vw, cols, *, nq, t0, tk, n_cmp, cmp_blk, n_sel, kvh, dh,
                win_rows, w0, causal_tiles):
    B, Lq, QW = q.shape
    H = QW // (2 * dh)
    hg = H // kvh
    cw = kvh * dh
    Lk = ks.shape[1]
    NC = kc.shape[1]
    nsp = LANES
    assert n_sel <= nsp and Lk % tk == 0 and Lq % nq == 0 and dh * 2 == LANES
    expand = (lax.broadcasted_iota(I32, (nsp, Lk), 1) // SEL_BLOCK ==
              lax.broadcasted_iota(I32, (nsp, Lk), 0)).astype(BF16)
    slopes = tuple(2.0 ** (-8.0 * (h + 1) / H) for h in range(H))
    kern = functools.partial(_nsa_kernel, nq=nq, t0=t0, tk=tk, n_cmp=n_cmp, cmp_blk=cmp_blk, n_sel=n_sel,
                             kvh=kvh, hg=hg, dh=dh, slopes=slopes, win_rows=win_rows, w0=w0,
                             causal_tiles=causal_tiles)
    qmap = lambda b, i: (b, i, 0)
    kv_spec = lambda arr, c: pl.BlockSpec((1, arr.shape[1], cw), lambda b, i, c=c: (b, 0, c))
    return pl.pallas_call(
        kern, grid=(B, Lq // nq),
        in_specs=[pl.BlockSpec((1, nq, QW), qmap), pl.BlockSpec((1, nq, LANES), qmap),
                  kv_spec(kc, cols[0]), kv_spec(vc, cols[1]), kv_spec(ks, cols[2]), kv_spec(vs, cols[3]),
                  kv_spec(kw, cols[4]), kv_spec(vw, cols[5]),
                  pl.BlockSpec((nsp, Lk), lambda b, i: (0, 0))],
        out_specs=pl.BlockSpec((1, nq, QW), qmap),
        out_shape=jax.ShapeDtypeStruct((B, Lq, QW), F32),
        compiler_params=_cparams("parallel", "arbitrary"), name="nsa_attend")(q, gt, kc, vc, ks, vs, kw, vw, expand)


def _head_tiles(w, H, kvh, dh, axis):
    hg = H // kvh
    parts = []
    for h in range(H):
        sl = [slice(None)] * w.ndim
        sl[axis] = slice(h * dh, (h + 1) * dh)
        piece = w[tuple(sl)]
        zero = jnp.zeros_like(piece)
        parts += [zero, piece] if ((h // hg) % 2) else [piece, zero]
    return jnp.concatenate(parts, axis=axis)


def kernel(x_prompt, x_sample, cache_kv_pages, cache_win, state_ssm, state_conv, page_table, ln_g, ln_b, m_w_in, m_conv_w, m_conv_b, m_dt_bias, m_a_log, m_d_skip, m_norm_w, m_w_out, w_kv_shared, w_cmp, pos_cmp, nsa_w_in, nsa_w_out, peer_w_q, peer_sub_keys, peer_u, peer_v):
    Bp, Lp, D = x_prompt.shape
    Bs, Ls, _ = x_sample.shape
    depth = ln_g.shape[0]
    n_a = m_w_in.shape[0]
    alpha = (2 * depth) ** DEPTH_ALPHA_POW
    Tp, Ts = Bp * Lp, Bs * Ls
    T = Tp + Ts
    Tpad = -(-T // 512) * 512
    X = jnp.concatenate([x_prompt.reshape(Tp, D), x_sample.reshape(Ts, D), jnp.zeros((Tpad - T, D), F32)], axis=0)
    dims = (Bp, Lp, Bs, Ls)

    NP, PS, n_rows, kvh, dh = cache_kv_pages.shape
    cw = kvh * dh
    n_pages = page_table.shape[1]
    past_len = n_pages * PS
    win_cache = cache_win.shape[1]
    cmp_blk = w_cmp.shape[1]
    H = nsa_w_in.shape[2] // (dh + 3)

    ssm_p, ssm_s, conv_p, conv_s = [], [], [], []
    for i in range(depth):
        if i < n_a:
            mp = (m_w_in[i], m_conv_w[i], m_conv_b[i], m_dt_bias[i], m_a_log[i], m_d_skip[i], m_norm_w[i], m_w_out[i])
            X, hp, hs, cp, cs = _mamba_layer(X, dims, state_conv[i], state_ssm[i], mp, ln_g[i, 0], ln_b[i, 0], alpha)
            ssm_p.append(hp)
            ssm_s.append(hs)
            conv_p.append(cp)
            conv_s.append(cs)
        else:
            if i == n_a:
                kv = _matmul(X, w_kv_shared.astype(BF16))
                kv_p = kv[:Tp].reshape(Bp, Lp, 6 * cw)
                kv_s = kv[Tp:T].reshape(Bs, Ls, 6 * cw)
                kvb_p = kv_p.astype(BF16)
                n_sel_p = -(-Lp // SEL_BLOCK)
                Lpp = n_sel_p * SEL_BLOCK
                padp = lambda v: jnp.pad(v, ((0, 0), (0, Lpp - Lp), (0, 0)))
                nc_p = -(-(Lpp // cmp_blk) // LANES) * LANES
                kc_p = _compress(padp(kv_p[:, :, 0:cw]), w_cmp[0], pos_cmp[0], nc_p)
                vc_p = _compress(padp(kv_p[:, :, cw:2 * cw]), w_cmp[1], pos_cmp[1], nc_p)
                tk_p = _pick(Lp, (512, 256, 128))
                new_rows = jnp.pad(kv_s[:, :, :4 * cw], ((0, 0), (0, SUBLANES - Ls), (0, 0)))
                kcr_s, vcr_s, ks_s, vs_s = _gather_pages(cache_kv_pages.reshape(NP, PS, n_rows * cw), page_table, new_rows)
                Lk_s = ks_s.shape[1]
                n_sel_s = -(-(past_len + Ls) // SEL_BLOCK)
                nc_s = -(-(Lk_s // cmp_blk) // LANES) * LANES
                kc_s = _compress(kcr_s, w_cmp[0], pos_cmp[0], nc_s)
                vc_s = _compress(vcr_s, w_cmp[1], pos_cmp[1], nc_s)
                win_full_s = jnp.concatenate([cache_win, kv_s[:, :, 4 * cw:].reshape(Bs, Ls, 2, kvh, dh)], axis=1)
                Lw = win_cache + Ls
                Lwp = -(-Lw // LANES) * LANES
                wflat = jnp.pad(win_full_s.reshape(Bs, Lw, 2 * cw), ((0, 0), (0, Lwp - Lw), (0, 0))).astype(BF16)
            j = i - n_a
            wq = _head_tiles(nsa_w_in[j][:, :H * dh], H, kvh, dh, axis=1).astype(BF16)
            wg = _pad_cols(nsa_w_in[j][:, H * dh:], LANES).astype(BF16)
            wo = _head_tiles(nsa_w_out[j], H, kvh, dh, axis=0).astype(BF16)
            q = _matmul(X, wq)
            gt = _matmul(X, wg)
            QW = q.shape[1]
            o_p = _nsa_attend(q[:Tp].reshape(Bp, Lp, QW), gt[:Tp].reshape(Bp, Lp, LANES),
                              kc_p, vc_p, kvb_p, kvb_p, kvb_p, kvb_p, (0, 0, 2, 3, 4, 5),
                              nq=Q_BLOCK, t0=0, tk=tk_p, n_cmp=Lpp // cmp_blk, cmp_blk=cmp_blk, n_sel=n_sel_p,
                              kvh=kvh, dh=dh, win_rows=min(WINDOW + Q_BLOCK, Lp), w0=0, causal_tiles=True)
            pad_q = lambda v: jnp.pad(v.reshape(Bs, Ls, -1), ((0, 0), (0, SUBLANES - Ls), (0, 0)))
            o_s = _nsa_attend(pad_q(q[Tp:T]), pad_q(gt[Tp:T]), kc_s, vc_s, ks_s, vs_s, wflat, wflat,
                              (0, 0, 0, 0, 0, 1),
                              nq=SUBLANES, t0=past_len, tk=Lk_s, n_cmp=-(-(past_len + Ls) // SEL_BLOCK) * (SEL_BLOCK // cmp_blk),
                              cmp_blk=cmp_blk, n_sel=n_sel_s, kvh=kvh, dh=dh, win_rows=Lwp,
                              w0=past_len - win_cache, causal_tiles=False)
            o = jnp.concatenate([o_p.reshape(Tp, QW), o_s[:, :Ls].reshape(Ts, QW), jnp.zeros((Tpad - T, QW), F32)], axis=0)
            X = _matmul_ln(o, wo, X, ln_g[i, 0], ln_b[i, 0], alpha)
        X = _peer_layer(X, peer_w_q[i], peer_sub_keys[i], peer_u[i], peer_v[i], ln_g[i, 1], ln_b[i, 1], alpha)

    rows_p = kv_p[:, :, :4 * cw].reshape(Bp, Lp, n_rows, kvh, dh)
    rows_s = kv_s[:, :, :4 * cw].reshape(Bs, Ls, n_rows, kvh, dh)
    win_p = kv_p[:, :, 4 * cw:].reshape(Bp, Lp, 2, kvh, dh)[:, -min(WINDOW, Lp):]
    return (X[:Tp].reshape(Bp, Lp, D), X[Tp:T].reshape(Bs, Ls, D), rows_p, rows_s, win_p,
            win_full_s[:, -win_cache:], jnp.stack(ssm_p), jnp.stack(ssm_s), jnp.stack(conv_p), jnp.stack(conv_s))
```

```python
import functools
import math

import jax
import jax.numpy as jnp
from jax import lax
from jax.experimental import pallas as pl
from jax.experimental.pallas import tpu as pltpu

F32 = jnp.float32
BF16 = jnp.bfloat16
I32 = jnp.int32

DEPTH_ALPHA_POW = 0.25
LN_EPS = 1e-5
SEL_BLOCK = 64
N_SEL = 16
WINDOW = 512
FORCE_SCORE = 1.0e4
PEER_TOPK = 16
SSD_CHUNK = 128
Q_BLOCK = 128
GELU_C0 = math.sqrt(2.0 / math.pi)
GELU_C1 = 0.044715 * GELU_C0
LOG2E = 1.0 / math.log(2.0)
W_TOKENS_PER_ITER = 16

LANES = 128
SUBLANES = 8
VMEM_LIMIT_BYTES = 56 * 1024 * 1024

_NT = (((1,), (1,)), ((), ()))
_TN = (((0,), (0,)), ((), ()))
_HI = lax.Precision.HIGHEST


def _cparams(*sem):
    return pltpu.CompilerParams(dimension_semantics=sem, vmem_limit_bytes=VMEM_LIMIT_BYTES)


def _pick(n, cands):
    for c in cands:
        if n % c == 0:
            return c
    raise ValueError(f"no tile in {cands} divides {n}")


def _pad_cols(w, n):
    return jnp.pad(w, ((0, 0), (0, n - w.shape[1])))


def _mm_kernel(x_ref, w_ref, o_ref):
    o_ref[...] = jnp.dot(x_ref[...].astype(BF16), w_ref[...],
                         preferred_element_type=F32).astype(o_ref.dtype)


def _mm_bias_kernel(x_ref, xb_ref, w_ref, o_ref):
    x = (x_ref[...] + xb_ref[...]).astype(BF16)
    o_ref[...] = jnp.dot(x, w_ref[...], preferred_element_type=F32).astype(o_ref.dtype)


def _matmul(x, w, *, xbias=None, out_dtype=F32, tm_cap=512):
    M, K = x.shape
    N = w.shape[1]
    tm = _pick(M, tuple(t for t in (512, 256, 128, 64, 32, 16, 8) if t <= tm_cap))
    tn = _pick(N, (1024, 512, 256, 128))
    x_spec = pl.BlockSpec((tm, K), lambda j, i: (i, 0))
    w_spec = pl.BlockSpec((K, tn), lambda j, i: (0, j))
    o_spec = pl.BlockSpec((tm, tn), lambda j, i: (i, j))
    if xbias is None:
        body, specs, args = _mm_kernel, [x_spec, w_spec], (x, w)
    else:
        b_spec = pl.BlockSpec((1, K), lambda j, i: (0, 0))
        body, specs, args = _mm_bias_kernel, [x_spec, b_spec, w_spec], (x, xbias, w)
    return pl.pallas_call(
        body, grid=(N // tn, M // tm), in_specs=specs, out_specs=o_spec,
        out_shape=jax.ShapeDtypeStruct((M, N), out_dtype),
        compiler_params=_cparams("parallel", "parallel"), name="proj")(*args)


def _deepnorm_ln(v, g, b):
    mu = jnp.mean(v, axis=-1, keepdims=True)
    d = v - mu
    var = jnp.mean(d * d, axis=-1, keepdims=True)
    return d * lax.rsqrt(var + LN_EPS) * g + b


def _mm_ln_kernel(x_ref, w_ref, r_ref, g_ref, b_ref, o_ref, *, alpha):
    y = jnp.dot(x_ref[...].astype(BF16), w_ref[...], preferred_element_type=F32)
    o_ref[...] = _deepnorm_ln(alpha * r_ref[...] + y, g_ref[...], b_ref[...])


def _matmul_ln(x, w, resid, g, b, alpha):
    M, K = x.shape
    D = w.shape[1]
    tm = _pick(M, (512, 256, 128, 64, 32, 16, 8))
    row = lambda i: (i, 0)
    fixed = lambda i: (0, 0)
    return pl.pallas_call(
        functools.partial(_mm_ln_kernel, alpha=alpha), grid=(M // tm,),
        in_specs=[pl.BlockSpec((tm, K), row), pl.BlockSpec((K, D), fixed),
                  pl.BlockSpec((tm, D), row), pl.BlockSpec((1, D), fixed), pl.BlockSpec((1, D), fixed)],
        out_specs=pl.BlockSpec((tm, D), row),
        out_shape=jax.ShapeDtypeStruct((M, D), F32),
        compiler_params=_cparams("parallel"), name="proj_ln")(x, w, resid, g.reshape(1, D), b.reshape(1, D))


def _softplus(x):
    return jnp.maximum(x, 0.0) + jnp.log1p(jnp.exp(-jnp.abs(x)))


def _silu(x):
    return x * jax.nn.sigmoid(x)


def _ssd_kernel(z_ref, xbc_ref, dtr_ref, cbuf_ref, h0_ref, cw_ref, cb_ref, dtb_ref, alog_ref,
                dsk_ref, nw_ref, y_ref, hn_ref, xf_scr, h_scr, *, Lc, H, P, N, G, n_valid, conv_w):
    c = pl.program_id(1)
    DI = H * P
    HG = H // G
    GP = HG * P
    tail = xf_scr.shape[0] - Lc

    @pl.when(c == 0)
    def _():
        xf_scr[0:tail, :] = cbuf_ref[0]
        h_scr[...] = h0_ref[0]

    xf_scr[tail:tail + Lc, :] = xbc_ref[0]
    conv = cb_ref[...]
    for k in range(conv_w):
        conv = conv + xf_scr[pl.ds(tail - (conv_w - 1) + k, Lc), :] * cw_ref[k:k + 1, :]
    conv = _silu(conv)
    xf_scr[0:tail, :] = xf_scr[pl.ds(Lc, tail), :]

    xs = conv[:, :DI]
    Bm = conv[:, DI:DI + G * N]
    Cm = conv[:, DI + G * N:]

    lane = lax.broadcasted_iota(I32, (Lc, LANES), 1)
    rowi = lax.broadcasted_iota(I32, (Lc, LANES), 0)
    dt = jnp.where((lane < H) & (rowi < n_valid), _softplus(dtr_ref[0] + dtb_ref[...]), 0.0)
    a = dt * (-jnp.exp(alog_ref[...]))
    ti = lax.broadcasted_iota(I32, (Lc, Lc), 0)
    si = lax.broadcasted_iota(I32, (Lc, Lc), 1)
    causal = ti >= si
    tril = jnp.where(causal, 1.0, 0.0).astype(F32)
    eye = jnp.where(lax.broadcasted_iota(I32, (LANES, LANES), 0) ==
                    lax.broadcasted_iota(I32, (LANES, LANES), 1), 1.0, 0.0).astype(F32)
    acum = jnp.dot(tril, a, precision=_HI, preferred_element_type=F32)
    acum_t = lax.dot_general(eye, acum, _NT, precision=_HI, preferred_element_type=F32)
    dt_t = lax.dot_general(eye, dt, _NT, precision=_HI, preferred_element_type=F32)
    alast = acum[Lc - 1:Lc, :]
    wend = jnp.exp(alast - acum) * dt
    ea = jnp.exp(acum)
    edec = jnp.exp(alast)
    dsk = dsk_ref[...]
    first_half = lax.broadcasted_iota(I32, (Lc, 2 * P), 1) < P

    def pair(v, h):
        rows = v.shape[0]
        return jnp.where(first_half[:rows], jnp.broadcast_to(v[:, h:h + 1], (rows, 2 * P)),
                         jnp.broadcast_to(v[:, h + 1:h + 2], (rows, 2 * P)))

    y_parts = []
    for g in range(G):
        Bg = Bm[:, g * N:(g + 1) * N].astype(BF16)
        Cg = Cm[:, g * N:(g + 1) * N].astype(BF16)
        cb = lax.dot_general(Cg, Bg, _NT, preferred_element_type=F32)
        Sg = h_scr[g * GP:(g + 1) * GP, :]
        yoff = lax.dot_general(Cg, Sg.astype(BF16), _NT, preferred_element_type=F32)
        xw_parts = []
        for pr in range(HG // 2):
            h = g * HG + 2 * pr
            xs_pair = xs[:, h * P:(h + 2) * P]
            xs_pair_b = xs_pair.astype(BF16)
            outs = []
            for hh in (h, h + 1):
                seg = acum[:, hh:hh + 1] - acum_t[hh:hh + 1, :]
                dec = jnp.exp(jnp.where(causal, seg, -jnp.inf))
                mix = cb * dec * dt_t[hh:hh + 1, :]
                outs.append(jnp.dot(mix.astype(BF16), xs_pair_b, preferred_element_type=F32))
            ydiag = jnp.where(first_half, outs[0], outs[1])
            y_pair = ydiag + yoff[:, pr * 2 * P:(pr + 1) * 2 * P] * pair(ea, h) + pair(dsk, h) * xs_pair
            y_parts.append(y_pair)
            xw_parts.append(xs_pair * pair(wend, h))
        xw = jnp.concatenate(xw_parts, axis=1).astype(BF16)
        states = lax.dot_general(xw, Bg, _TN, preferred_element_type=F32)
        for hl in range(HG):
            hh = g * HG + hl
            r0 = g * GP + hl * P
            h_scr[r0:r0 + P, :] = edec[:, hh:hh + 1] * Sg[hl * P:(hl + 1) * P, :] + states[hl * P:(hl + 1) * P, :]

    z = z_ref[0]
    gn = DI // G
    outs = []
    for g in range(G):
        parts = y_parts[g * (HG // 2):(g + 1) * (HG // 2)]
        yg = jnp.concatenate(parts, axis=1) if len(parts) > 1 else parts[0]
        yg = yg * _silu(z[:, g * gn:(g + 1) * gn])
        ms = jnp.mean(yg * yg, axis=-1, keepdims=True)
        outs.append(yg * lax.rsqrt(ms + 1e-5))
    y_ref[0] = jnp.concatenate(outs, axis=1) * nw_ref[...]

    @pl.when(c == pl.num_programs(1) - 1)
    def _():
        hn_ref[0] = h_scr[...]


def _ssd(z, xbc, dtr, cbuf, h0, conv_w, conv_b, dt_bias, a_log, d_skip, norm_w, *, Lc, n_valid, G):
    B, L, DI = z.shape
    CD = xbc.shape[-1]
    _, H, P, N = h0.shape
    KW = conv_w.shape[0]
    assert P * 2 == LANES and (H // G) % 2 == 0 and H <= LANES and L % Lc == 0
    padl = lambda v: jnp.pad(v.astype(F32).reshape(1, -1), ((0, 0), (0, LANES - H)))
    seq = lambda b, c: (b, c, 0)
    per_b = lambda b, c: (b, 0, 0)
    fixed = lambda b, c: (0, 0)
    kern = functools.partial(_ssd_kernel, Lc=Lc, H=H, P=P, N=N, G=G, n_valid=n_valid, conv_w=KW)
    y, hn = pl.pallas_call(
        kern, grid=(B, L // Lc),
        in_specs=[pl.BlockSpec((1, Lc, DI), seq), pl.BlockSpec((1, Lc, CD), seq),
                  pl.BlockSpec((1, Lc, LANES), seq), pl.BlockSpec((1, SUBLANES, CD), per_b),
                  pl.BlockSpec((1, H * P, N), per_b),
                  pl.BlockSpec((KW, CD), fixed), pl.BlockSpec((1, CD), fixed),
                  pl.BlockSpec((1, LANES), fixed), pl.BlockSpec((1, LANES), fixed),
                  pl.BlockSpec((1, LANES), fixed), pl.BlockSpec((1, DI), fixed)],
        out_specs=[pl.BlockSpec((1, Lc, DI), seq), pl.BlockSpec((1, H * P, N), per_b)],
        out_shape=[jax.ShapeDtypeStruct((B, L, DI), F32), jax.ShapeDtypeStruct((B, H * P, N), F32)],
        scratch_shapes=[pltpu.VMEM((SUBLANES + Lc, CD), F32), pltpu.VMEM((H * P, N), F32)],
        compiler_params=_cparams("parallel", "arbitrary"), name="ssd",
    )(z, xbc, dtr, cbuf, h0.reshape(B, H * P, N), conv_w, conv_b.reshape(1, CD),
      padl(dt_bias), padl(a_log), padl(d_skip), norm_w.reshape(1, DI))
    return y, hn.reshape(B, H, P, N)


def _mamba_layer(Xp, Xs, dims, state_conv_i, state_ssm_i, params, ln_g, ln_b, alpha):
    Bp, Lp, Bs, Ls = dims
    w_in, conv_w, conv_b, dt_bias, a_log, d_skip, norm_w, w_out = params
    Tp = Bp * Lp
    DI = norm_w.shape[0]
    CD = conv_w.shape[1]
    _, H, P, N = state_ssm_i.shape
    G = (CD - DI) // (2 * N)
    KW = conv_w.shape[0]
    Ts = Bs * Ls
    wz = w_in[:, :DI].astype(BF16)
    wx = w_in[:, DI:DI + CD].astype(BF16)
    wd = _pad_cols(w_in[:, DI + CD:], LANES).astype(BF16)
    wo = w_out.astype(BF16)
    args = (conv_w, conv_b, dt_bias, a_log, d_skip, norm_w)

    xbc_p = _matmul(Xp, wx).reshape(Bp, Lp, CD)
    Lc = SSD_CHUNK if Lp % SSD_CHUNK == 0 else Lp
    yp, hp = _ssd(_matmul(Xp, wz).reshape(Bp, Lp, DI), xbc_p, _matmul(Xp, wd).reshape(Bp, Lp, LANES),
                  jnp.zeros((Bp, SUBLANES, CD), F32), jnp.zeros((Bp, H, P, N), F32), *args,
                  Lc=Lc, n_valid=Lc, G=G)
    conv_p = xbc_p[:, Lp - (KW - 1):]
    Xp = _matmul_ln(yp.reshape(Tp, DI), wo, Xp, ln_g, ln_b, alpha)

    Lsp = -(-Ls // SUBLANES) * SUBLANES
    pad_rows = lambda v: jnp.pad(v[:Ts].reshape(Bs, Ls, -1), ((0, 0), (0, Lsp - Ls), (0, 0)))
    xbc_s = _matmul(Xs, wx)[:Ts].reshape(Bs, Ls, CD)
    cbuf_s = jnp.pad(state_conv_i, ((0, 0), (SUBLANES - (KW - 1), 0), (0, 0)))
    ys, hs = _ssd(pad_rows(_matmul(Xs, wz)), pad_rows(xbc_s), pad_rows(_matmul(Xs, wd)), cbuf_s, state_ssm_i,
                  *args, Lc=Lsp, n_valid=Ls, G=G)
    conv_s = jnp.concatenate([state_conv_i, xbc_s], axis=1)[:, -(KW - 1):]
    ys = jnp.pad(ys[:, :Ls].reshape(Ts, DI), ((0, Xs.shape[0] - Ts), (0, 0)))
    Xs = _matmul_ln(ys, wo, Xs, ln_g, ln_b, alpha)
    return Xp, Xs, hp, hs, conv_p, conv_s


def _topk_rows(s, k):
    n = s.shape[0]
    rows = lax.broadcasted_iota(I32, s.shape, 0)
    vals, idxs = [], []
    for _ in range(k):
        m = jnp.max(s, axis=0, keepdims=True)
        i = jnp.min(jnp.where(s == m, rows, n), axis=0, keepdims=True)
        vals.append(m)
        idxs.append(i)
        s = jnp.where(rows == i, -jnp.inf, s)
    return vals, idxs


def _peer_topk_kernel(q_ref, k_ref, i1_ref, i2_ref, g_ref, *, heads, half, topk, n_keys):
    k1 = k_ref[0]
    k2 = k_ref[1]
    tb = q_ref.shape[0]
    cand_ab = [(a, b) for a in range(topk) for b in range(topk // (a + 1))]
    n_pad = -len(cand_ab) % SUBLANES
    codes, gates = [], []
    for h in range(heads):
        q1 = q_ref[:, (2 * h) * half:(2 * h + 1) * half]
        q2 = q_ref[:, (2 * h + 1) * half:(2 * h + 2) * half]
        s1 = lax.dot_general(k1, q1, _NT, preferred_element_type=F32)
        s2 = lax.dot_general(k2, q2, _NT, preferred_element_type=F32)
        v1, i1 = _topk_rows(s1, topk)
        v2, i2 = _topk_rows(s2, topk)
        cand = jnp.concatenate([v1[a] + v2[b] for a, b in cand_ab] +
                               [jnp.full((n_pad, tb), -jnp.inf, F32)], axis=0)
        code = jnp.concatenate([i1[a] * n_keys + i2[b] for a, b in cand_ab] +
                               [jnp.zeros((n_pad, tb), I32)], axis=0)
        sc, pos = _topk_rows(cand, topk)
        crow = lax.broadcasted_iota(I32, cand.shape, 0)
        codes += [jnp.max(jnp.where(crow == p, code, -1), axis=0, keepdims=True) for p in pos]
        ex = jnp.exp(jnp.concatenate(sc, axis=0) - sc[0])
        gates.append(ex / jnp.sum(ex, axis=0, keepdims=True))
    code_t = jnp.concatenate(codes, axis=0).T
    i1_ref[...] = code_t // n_keys
    i2_ref[...] = code_t % n_keys
    g_ref[...] = jnp.concatenate(gates, axis=0).T


def _peer_w_kernel(i1_ref, i2_ref, g_ref, w_ref, *, n_keys, nj):
    sub = lax.broadcasted_iota(I32, (n_keys, i1_ref.shape[1]), 0)

    def body(tb, carry):
        for u in range(W_TOKENS_PER_ITER):
            t = tb * W_TOKENS_PER_ITER + u
            i1 = i1_ref[pl.ds(t, 1), :]
            i2 = i2_ref[pl.ds(t, 1), :]
            g = g_ref[pl.ds(t, 1), :]
            onehot1 = jnp.where(sub == i1, 1.0, 0.0).astype(BF16)
            gated2 = jnp.where(sub == i2, g, 0.0).astype(BF16)
            w_t = lax.dot_general(onehot1, gated2, _NT, preferred_element_type=F32)
            r0 = pl.multiple_of(t * nj, nj)
            for k in range(n_keys // nj):
                w_ref[0, k, pl.ds(r0, nj), :] = w_t[k * nj:(k + 1) * nj, :]
        return carry

    lax.fori_loop(0, i1_ref.shape[0] // W_TOKENS_PER_ITER, body, 0)


def _gelu_tanh(x):
    hx = 0.5 * x
    inner = x * (GELU_C0 + GELU_C1 * (x * x))
    return hx * jnp.tanh(inner) + hx


def _peer_dense_kernel(x_ref, u_ref, v_ref, w_ref, g_ref, b_ref, o_ref, acc_ref, xb_ref, *, alpha, nj):
    k = pl.program_id(1)
    td = x_ref.shape[0]

    @pl.when(k == 0)
    def _():
        acc_ref[...] = jnp.zeros_like(acc_ref)
        xb_ref[...] = x_ref[...].astype(BF16)

    act = lax.dot_general(xb_ref[...], u_ref[...], _NT, preferred_element_type=F32)
    w = jnp.concatenate([w_ref[0, 0, pl.ds(j, td, stride=nj), :] for j in range(nj)], axis=1)
    coef = (_gelu_tanh(act) * w).astype(BF16)
    acc_ref[...] += jnp.dot(coef, v_ref[...], preferred_element_type=F32)

    @pl.when(k == pl.num_programs(1) - 1)
    def _():
        o_ref[...] = _deepnorm_ln(alpha * x_ref[...] + acc_ref[...], g_ref[...], b_ref[...])


def _peer_layer(X, w_q, sub_keys, u_tab, v_tab, ln_g, ln_b, alpha):
    T, D = X.shape
    _, n_keys, half = sub_keys.shape
    heads = w_q.shape[1] // (2 * half)
    nsel = heads * PEER_TOPK
    assert n_keys == LANES and nsel == LANES and half % LANES == 0
    q = _matmul(X, w_q)

    tb = LANES
    tok = lambda i: (i, 0)
    sel_shape = jax.ShapeDtypeStruct((T, nsel), I32)
    i1, i2, gate = pl.pallas_call(
        functools.partial(_peer_topk_kernel, heads=heads, half=half, topk=PEER_TOPK, n_keys=n_keys),
        grid=(T // tb,),
        in_specs=[pl.BlockSpec((tb, q.shape[1]), tok), pl.BlockSpec((2, n_keys, half), lambda i: (0, 0, 0))],
        out_specs=[pl.BlockSpec((tb, nsel), tok)] * 3,
        out_shape=[sel_shape, sel_shape, jax.ShapeDtypeStruct((T, nsel), F32)],
        compiler_params=_cparams("parallel"), name="peer_topk")(q, sub_keys)

    td = _pick(T, (1024, 768, 512, 256, 128))
    nj = SUBLANES
    et = nj * n_keys
    sub_blocks = td // tb
    w = pl.pallas_call(
        functools.partial(_peer_w_kernel, n_keys=n_keys, nj=nj), grid=(T // tb,),
        in_specs=[pl.BlockSpec((tb, nsel), tok)] * 3,
        out_specs=pl.BlockSpec((1, n_keys // nj, tb * nj, n_keys),
                               lambda i: (i // sub_blocks, 0, i % sub_blocks, 0)),
        out_shape=jax.ShapeDtypeStruct((T // td, n_keys // nj, td * nj, n_keys), F32),
        compiler_params=_cparams("parallel"), name="peer_w")(i1, i2, gate)

    return pl.pallas_call(
        functools.partial(_peer_dense_kernel, alpha=alpha, nj=nj),
        grid=(T // td, n_keys // nj),
        in_specs=[pl.BlockSpec((td, D), lambda i, k: (i, 0)),
                  pl.BlockSpec((et, D), lambda i, k: (k, 0)),
                  pl.BlockSpec((et, D), lambda i, k: (k, 0)),
                  pl.BlockSpec((1, 1, td * nj, n_keys), lambda i, k: (i, k, 0, 0)),
                  pl.BlockSpec((1, D), lambda i, k: (0, 0)), pl.BlockSpec((1, D), lambda i, k: (0, 0))],
        out_specs=pl.BlockSpec((td, D), lambda i, k: (i, 0)),
        out_shape=jax.ShapeDtypeStruct((T, D), F32),
        scratch_shapes=[pltpu.VMEM((td, D), F32), pltpu.VMEM((td, D), BF16)],
        compiler_params=_cparams("parallel", "arbitrary"), name="peer_dense",
    )(X, u_tab, v_tab, w, ln_g.reshape(1, D), ln_b.reshape(1, D))


def _gather_kernel(pt_ref, page_ref, new_ref, kc_ref, vc_ref, ks_ref, vs_ref, *, n_pages, cw):
    del pt_ref
    p = pl.program_id(1)
    outs = (kc_ref, vc_ref, ks_ref, vs_ref)

    @pl.when(p < n_pages)
    def _():
        blk = page_ref[0]
        for r, ref in enumerate(outs):
            ref[0] = blk[:, r * cw:(r + 1) * cw].astype(ref.dtype)

    @pl.when(p == n_pages)
    def _():
        new = new_ref[0]
        ps = page_ref.shape[1]
        for r, ref in enumerate(outs):
            rows = jnp.concatenate([new[:, r * cw:(r + 1) * cw],
                                    jnp.zeros((ps - new.shape[0], cw), F32)], axis=0)
            ref[0] = rows.astype(ref.dtype)


def _gather_pages(pages, page_table, new_rows):
    NP, PS, C = pages.shape
    B, n_pages = page_table.shape
    cw = C // 4
    Lp = (n_pages + 1) * PS
    out_spec = pl.BlockSpec((1, PS, cw), lambda b, p, pt: (b, p, 0))
    return pl.pallas_call(
        functools.partial(_gather_kernel, n_pages=n_pages, cw=cw),
        grid_spec=pltpu.PrefetchScalarGridSpec(
            num_scalar_prefetch=1, grid=(B, n_pages + 1),
            in_specs=[pl.BlockSpec((1, PS, C), lambda b, p, pt: (pt[b, jnp.minimum(p, n_pages - 1)], 0, 0)),
                      pl.BlockSpec((1, new_rows.shape[1], C), lambda b, p, pt: (b, 0, 0))],
            out_specs=[out_spec] * 4),
        out_shape=[jax.ShapeDtypeStruct((B, Lp, cw), F32), jax.ShapeDtypeStruct((B, Lp, cw), F32),
                   jax.ShapeDtypeStruct((B, Lp, cw), BF16), jax.ShapeDtypeStruct((B, Lp, cw), BF16)],
        compiler_params=_cparams("parallel", "arbitrary"), name="kv_gather")(page_table, pages, new_rows)


def _compress(x, w_c, pos_c, nc_pad):
    B, L, C = x.shape
    blk, dh, _ = w_c.shape
    G = C // dh
    n = L // blk
    w_bd = jnp.einsum('lde,gh->lgdhe', w_c, jnp.eye(G, dtype=w_c.dtype)).reshape(blk * C, C)
    xbias = jnp.broadcast_to(pos_c[:, None, :], (blk, G, dh)).reshape(1, blk * C)
    out = _matmul(x.reshape(B * n, blk * C), w_bd.astype(BF16), xbias=xbias, tm_cap=128)
    return jnp.pad(out.reshape(B, n, C), ((0, 0), (0, nc_pad - n), (0, 0)))


def _lane_tiles(x):
    return [x[:, j * LANES:(j + 1) * LANES] for j in range(x.shape[1] // LANES)]


def _row_max(x):
    return jnp.max(functools.reduce(jnp.maximum, _lane_tiles(x)), axis=-1, keepdims=True)


def _row_sum(x):
    return jnp.sum(functools.reduce(jnp.add, _lane_tiles(x)), axis=-1, keepdims=True)


def _split3_dot(x, m):
    x1 = x.astype(BF16)
    r1 = x - x1.astype(F32)
    x2 = r1.astype(BF16)
    x3 = (r1 - x2.astype(F32)).astype(BF16)
    d = lambda a: jnp.dot(a, m, preferred_element_type=F32)
    return d(x1) + d(x2) + d(x3)


def _nsa_multi_kernel(q_ref, gt_ref, kc_ref, vc_ref, ks_ref, vs_ref, kw_ref, vw_ref, ex_ref, o_ref, *scratch,
                      seqs, **cfg):
    for s in range(seqs):
        one = lambda r: r.at[pl.ds(s, 1)]
        _nsa_kernel(one(q_ref), one(gt_ref), one(kc_ref), one(vc_ref), one(ks_ref), one(vs_ref), one(kw_ref),
                    one(vw_ref), ex_ref, one(o_ref), *scratch, **cfg)


def _nsa_kernel(q_ref, gt_ref, kc_ref, vc_ref, ks_ref, vs_ref, kw_ref, vw_ref, ex_ref, o_ref,
                m_scr, l_scr, acc_scr, flag_ref, *,
                nq, t0, tk, n_cmp, cmp_blk, n_sel, kvh, hg, dh, slopes, win_rows, w0, causal_tiles):
    i = pl.program_id(1)
    q0 = t0 + i * nq
    R = hg * nq
    NC = kc_ref.shape[1]
    Lk = ks_ref.shape[1]
    NSP = ex_ref.shape[0]
    qscale = dh ** -0.5 * LOG2E
    slopes2 = tuple(s * LOG2E for s in slopes)
    gates = jax.nn.sigmoid(gt_ref[0])

    def tq_like(shape):
        return q0 + lax.broadcasted_iota(I32, shape, 0)

    if causal_tiles:
        wstart = pl.multiple_of(jnp.maximum(q0 - WINDOW, 0), nq)
    else:
        wstart = 0

    def col(g):
        return (g // 2) * 2 * dh

    def branch(qg, k_t, v_t, distm, g):
        s_all = lax.dot_general(qg, k_t, _NT, preferred_element_type=F32)
        ps = []
        for h in range(hg):
            s2 = s_all[h * nq:(h + 1) * nq] - slopes2[g * hg + h] * distm
            m = _row_max(s2)
            e = jnp.exp2(s2 - jnp.where(m == -jnp.inf, 0.0, m))
            ps.append(e / jnp.maximum(_row_sum(e), 1e-30))
        o = jnp.dot(jnp.concatenate(ps, axis=0).astype(BF16), v_t, preferred_element_type=F32)
        return ps, o

    n_idx = lax.broadcasted_iota(I32, (nq, NC), 1)
    dist_c = tq_like((nq, NC)) - (n_idx * cmp_blk + (cmp_blk - 1))
    distm_c = jnp.where((dist_c >= 0) & (n_idx < n_cmp), dist_c.astype(F32), jnp.inf)
    dist_w = tq_like((nq, win_rows)) - (w0 + wstart + lax.broadcasted_iota(I32, (nq, win_rows), 1))
    distm_w = jnp.where((dist_w >= 0) & (dist_w < WINDOW), dist_w.astype(F32), jnp.inf)
    per_sel = SEL_BLOCK // cmp_blk
    pair = jnp.where(lax.broadcasted_iota(I32, (NC, NSP), 0) // per_sel ==
                     lax.broadcasted_iota(I32, (NC, NSP), 1), 1.0, 0.0).astype(BF16)
    blk = lax.broadcasted_iota(I32, (nq, NSP), 1)
    tqs = tq_like((nq, NSP))
    cur = tqs // SEL_BLOCK
    forced = (blk == 0) | (blk == cur) | (blk == cur - 1)
    sel_valid = blk * SEL_BLOCK <= tqs

    qgs, o_cmps, o_wins, scores = [], [], [], []
    for g in range(kvh):
        c0 = col(g)
        qg = jnp.concatenate([q_ref[0, :, (g * hg + h) * 2 * dh:(g * hg + h + 1) * 2 * dh] for h in range(hg)],
                             axis=0)
        qg = (qg * qscale).astype(BF16)
        qgs.append(qg)
        p_cmp, o_cmp = branch(qg, kc_ref[0, :, c0:c0 + 2 * dh].astype(BF16),
                              vc_ref[0, :, c0:c0 + 2 * dh].astype(BF16), distm_c, g)
        o_cmps.append(o_cmp)
        psum = p_cmp[0]
        for h in range(1, hg):
            psum = psum + p_cmp[h]
        imp = _split3_dot(psum, pair)
        score = jnp.where(forced, FORCE_SCORE, jnp.where(sel_valid, imp, -1.0))
        scores.append(jnp.where(blk < n_sel, score, -2.0))
        _, o_win = branch(qg, kw_ref[0, pl.ds(wstart, win_rows), c0:c0 + 2 * dh],
                          vw_ref[0, pl.ds(wstart, win_rows), c0:c0 + 2 * dh], distm_w, g)
        o_wins.append(o_win)

    score = jnp.concatenate(scores, axis=0)
    n_top = min(N_SEL, n_sel)
    if score.shape[0] % LANES == 0 and NSP == LANES:
        st = score.T
        brow = lax.broadcasted_iota(I32, st.shape, 0)
        sel_t = jnp.zeros(st.shape, F32)
        for _ in range(n_top):
            m = jnp.max(st, axis=0, keepdims=True)
            idx = jnp.min(jnp.where(st == m, brow, NSP), axis=0, keepdims=True)
            hit = brow == idx
            sel_t = jnp.where(hit, 1.0, sel_t)
            st = jnp.where(hit, -jnp.inf, st)
        selm = sel_t.T
    else:
        bcol = lax.broadcasted_iota(I32, score.shape, 1)
        selm = jnp.zeros(score.shape, F32)
        for _ in range(n_top):
            m = jnp.max(score, axis=-1, keepdims=True)
            idx = jnp.min(jnp.where(score == m, bcol, NSP), axis=-1, keepdims=True)
            hit = bcol == idx
            selm = jnp.where(hit, 1.0, selm)
            score = jnp.where(hit, -jnp.inf, score)
    selb = selm.astype(BF16)

    def sel_tile(kt, carry):
        k0 = kt * tk
        dist = tq_like((nq, tk)) - (k0 + lax.broadcasted_iota(I32, (nq, tk), 1))
        causal = dist >= 0
        distf = dist.astype(F32)
        expand = ex_ref[:, pl.ds(k0, tk)]
        new = []
        for g in range(kvh):
            m_run, l_run, acc = carry[g]
            c0 = col(g)
            s_t = lax.dot_general(qgs[g], ks_ref[0, pl.ds(k0, tk), c0:c0 + 2 * dh], _NT,
                                  preferred_element_type=F32)
            selx = jnp.dot(selb[g * nq:(g + 1) * nq], expand, preferred_element_type=F32)
            distm = jnp.where(causal & (selx > 0.5), distf, jnp.inf)
            m_new, l_new, alphas, ps = [], [], [], []
            for h in range(hg):
                rows = slice(h * nq, (h + 1) * nq)
                s2 = s_t[rows] - slopes2[g * hg + h] * distm
                m_h = jnp.maximum(m_run[rows], _row_max(s2))
                m_safe = jnp.where(m_h == -jnp.inf, 0.0, m_h)
                a_h = jnp.exp2(m_run[rows] - m_safe)
                p_h = jnp.exp2(s2 - m_safe)
                m_new.append(m_h)
                l_new.append(a_h * l_run[rows] + _row_sum(p_h))
                alphas.append(a_h)
                ps.append(p_h)
            pv = jnp.dot(jnp.concatenate(ps, axis=0).astype(BF16), vs_ref[0, pl.ds(k0, tk), c0:c0 + 2 * dh],
                         preferred_element_type=F32)
            new.append((jnp.concatenate(m_new, axis=0), jnp.concatenate(l_new, axis=0),
                        jnp.concatenate(alphas, axis=0) * acc + pv))
        return tuple(new)

    def sel_tile_group(kt, g):
        k0 = pl.multiple_of(kt * tk, tk)
        c0 = col(g)
        dist = tq_like((nq, tk)) - (k0 + lax.broadcasted_iota(I32, (nq, tk), 1))
        selx = jnp.dot(selb[g * nq:(g + 1) * nq], ex_ref[:, pl.ds(k0, tk)], preferred_element_type=F32)
        distm = jnp.where((dist >= 0) & (selx > 0.5), dist.astype(F32), jnp.inf)
        s_t = lax.dot_general(qgs[g], ks_ref[0, pl.ds(k0, tk), c0:c0 + 2 * dh], _NT,
                              preferred_element_type=F32)
        m_run = m_scr[g]
        l_run = l_scr[g]
        m_new, l_new, alphas, ps = [], [], [], []
        for h in range(hg):
            rows = slice(h * nq, (h + 1) * nq)
            s2 = s_t[rows] - slopes2[g * hg + h] * distm
            m_h = jnp.maximum(m_run[rows], _row_max(s2))
            m_safe = jnp.where(m_h == -jnp.inf, 0.0, m_h)
            a_h = jnp.exp2(m_run[rows] - m_safe)
            p_h = jnp.exp2(s2 - m_safe)
            m_new.append(m_h)
            l_new.append(a_h * l_run[rows] + _row_sum(p_h))
            alphas.append(a_h)
            ps.append(p_h)
        pv = jnp.dot(jnp.concatenate(ps, axis=0).astype(BF16), vs_ref[0, pl.ds(k0, tk), c0:c0 + 2 * dh],
                     preferred_element_type=F32)
        m_scr[g] = jnp.concatenate(m_new, axis=0)
        l_scr[g] = jnp.concatenate(l_new, axis=0)
        acc_scr[g] = jnp.concatenate(alphas, axis=0) * acc_scr[g] + pv

    if causal_tiles:
        n_flag = Lk // tk
        blocks_per_tile = tk // SEL_BLOCK
        lane_b = lax.broadcasted_iota(I32, (1, NSP), 1)
        for g in range(kvh):
            any_q = jnp.max(selm[g * nq:(g + 1) * nq], axis=0, keepdims=True)
            for kt in range(n_flag):
                in_tile = (lane_b >= kt * blocks_per_tile) & (lane_b < (kt + 1) * blocks_per_tile)
                flag_ref[g * n_flag + kt] = (jnp.max(jnp.where(in_tile, any_q, 0.0)) > 0.5).astype(I32)
        m_scr[...] = jnp.full(m_scr.shape, -jnp.inf, F32)
        l_scr[...] = jnp.zeros(l_scr.shape, F32)
        acc_scr[...] = jnp.zeros(acc_scr.shape, F32)

        def visit(kt, carry):
            for g in range(kvh):
                @pl.when(flag_ref[g * n_flag + kt] > 0)
                def _():
                    sel_tile_group(kt, g)
            return carry

        lax.fori_loop(0, (q0 + nq + tk - 1) // tk, visit, 0)
        fin = tuple((None, l_scr[g], acc_scr[g]) for g in range(kvh))
    else:
        fin = tuple((jnp.full((R, 1), -jnp.inf, F32), jnp.zeros((R, 1), F32), jnp.zeros((R, 2 * dh), F32))
                    for _ in range(kvh))
        for kt in range(Lk // tk):
            fin = sel_tile(kt, fin)

    lane = lax.broadcasted_iota(I32, (nq, 2 * dh), 1)
    for g in range(kvh):
        lo = (g % 2) * dh
        keep = (lane >= lo) & (lane < lo + dh)
        _, l_fin, acc = fin[g]
        o_sel = acc / jnp.maximum(l_fin, 1e-30)
        for h in range(hg):
            hd = g * hg + h
            rows = slice(h * nq, (h + 1) * nq)
            o = (gates[:, 3 * hd:3 * hd + 1] * o_cmps[g][rows] + gates[:, 3 * hd + 1:3 * hd + 2] * o_sel[rows] +
                 gates[:, 3 * hd + 2:3 * hd + 3] * o_wins[g][rows])
            o_ref[0, :, hd * 2 * dh:(hd + 1) * 2 * dh] = jnp.where(keep, o, 0.0)


def _nsa_attend(q, gt, kc, vc, ks, vs, kw, vw, cols, *, nq, t0, tk, n_cmp, cmp_blk, n_sel, kvh, dh,
                win_rows, w0, causal_tiles, seqs=1):
    B, Lq, QW = q.shape
    H = QW // (2 * dh)
    hg = H // kvh
    cw = kvh * dh
    Lk = ks.shape[1]
    NC = kc.shape[1]
    nsp = LANES
    assert n_sel <= nsp and Lk % tk == 0 and Lq % nq == 0 and dh * 2 == LANES
    expand = (lax.broadcasted_iota(I32, (nsp, Lk), 1) // SEL_BLOCK ==
              lax.broadcasted_iota(I32, (nsp, Lk), 0)).astype(BF16)
    slopes = tuple(2.0 ** (-8.0 * (h + 1) / H) for h in range(H))
    assert B % seqs == 0
    kern = functools.partial(_nsa_multi_kernel, seqs=seqs, nq=nq, t0=t0, tk=tk, n_cmp=n_cmp, cmp_blk=cmp_blk,
                             n_sel=n_sel, kvh=kvh, hg=hg, dh=dh, slopes=slopes, win_rows=win_rows, w0=w0,
                             causal_tiles=causal_tiles)
    qmap = lambda b, i: (b, i, 0)
    kv_spec = lambda arr, c: pl.BlockSpec((seqs, arr.shape[1], cw), lambda b, i, c=c: (b, 0, c))
    R = hg * nq
    return pl.pallas_call(
        kern, grid=(B // seqs, Lq // nq),
        in_specs=[pl.BlockSpec((seqs, nq, QW), qmap), pl.BlockSpec((seqs, nq, LANES), qmap),
                  kv_spec(kc, cols[0]), kv_spec(vc, cols[1]), kv_spec(ks, cols[2]), kv_spec(vs, cols[3]),
                  kv_spec(kw, cols[4]), kv_spec(vw, cols[5]),
                  pl.BlockSpec((nsp, Lk), lambda b, i: (0, 0))],
        out_specs=pl.BlockSpec((seqs, nq, QW), qmap),
        out_shape=jax.ShapeDtypeStruct((B, Lq, QW), F32),
        scratch_shapes=[pltpu.VMEM((kvh, R, 1), F32), pltpu.VMEM((kvh, R, 1), F32),
                        pltpu.VMEM((kvh, R, 2 * dh), F32), pltpu.SMEM((kvh * (Lk // tk),), I32)],
        compiler_params=_cparams("parallel", "arbitrary"), name="nsa_attend")(q, gt, kc, vc, ks, vs, kw, vw, expand)


def _head_tiles(w, H, kvh, dh, axis):
    hg = H // kvh
    parts = []
    for h in range(H):
        sl = [slice(None)] * w.ndim
        sl[axis] = slice(h * dh, (h + 1) * dh)
        piece = w[tuple(sl)]
        zero = jnp.zeros_like(piece)
        parts += [zero, piece] if ((h // hg) % 2) else [piece, zero]
    return jnp.concatenate(parts, axis=axis)


def kernel(x_prompt, x_sample, cache_kv_pages, cache_win, state_ssm, state_conv, page_table, ln_g, ln_b, m_w_in, m_conv_w, m_conv_b, m_dt_bias, m_a_log, m_d_skip, m_norm_w, m_w_out, w_kv_shared, w_cmp, pos_cmp, nsa_w_in, nsa_w_out, peer_w_q, peer_sub_keys, peer_u, peer_v):
    Bp, Lp, D = x_prompt.shape
    Bs, Ls, _ = x_sample.shape
    depth = ln_g.shape[0]
    n_a = m_w_in.shape[0]
    alpha = (2 * depth) ** DEPTH_ALPHA_POW
    Tp, Ts = Bp * Lp, Bs * Ls
    assert Tp % LANES == 0
    Tsp = -(-Ts // LANES) * LANES
    Xp = x_prompt.reshape(Tp, D)
    Xs = jnp.pad(x_sample.reshape(Ts, D), ((0, Tsp - Ts), (0, 0)))
    dims = (Bp, Lp, Bs, Ls)

    NP, PS, n_rows, kvh, dh = cache_kv_pages.shape
    cw = kvh * dh
    n_pages = page_table.shape[1]
    past_len = n_pages * PS
    win_cache = cache_win.shape[1]
    cmp_blk = w_cmp.shape[1]
    H = nsa_w_in.shape[2] // (dh + 3)

    ssm_p, ssm_s, conv_p, conv_s = [], [], [], []
    for i in range(depth):
        if i < n_a:
            mp = (m_w_in[i], m_conv_w[i], m_conv_b[i], m_dt_bias[i], m_a_log[i], m_d_skip[i], m_norm_w[i], m_w_out[i])
            Xp, Xs, hp, hs, cp, cs = _mamba_layer(Xp, Xs, dims, state_conv[i], state_ssm[i], mp,
                                                  ln_g[i, 0], ln_b[i, 0], alpha)
            ssm_p.append(hp)
            ssm_s.append(hs)
            conv_p.append(cp)
            conv_s.append(cs)
        else:
            if i == n_a:
                wkv = w_kv_shared.astype(BF16)
                kv_p = _matmul(Xp, wkv).reshape(Bp, Lp, 6 * cw)
                kv_s = _matmul(Xs, wkv)[:Ts].reshape(Bs, Ls, 6 * cw)
                kvb_p = kv_p.astype(BF16)
                n_sel_p = -(-Lp // SEL_BLOCK)
                Lpp = n_sel_p * SEL_BLOCK
                padp = lambda v: jnp.pad(v, ((0, 0), (0, Lpp - Lp), (0, 0)))
                nc_p = -(-(Lpp // cmp_blk) // LANES) * LANES
                kc_p = _compress(padp(kv_p[:, :, 0:cw]), w_cmp[0], pos_cmp[0], nc_p)
                vc_p = _compress(padp(kv_p[:, :, cw:2 * cw]), w_cmp[1], pos_cmp[1], nc_p)
                tk_p = _pick(Lp, (512, 256, 128))
                new_rows = jnp.pad(kv_s[:, :, :4 * cw], ((0, 0), (0, SUBLANES - Ls), (0, 0)))
                kcr_s, vcr_s, ks_s, vs_s = _gather_pages(cache_kv_pages.reshape(NP, PS, n_rows * cw), page_table, new_rows)
                Lk_s = ks_s.shape[1]
                n_sel_s = -(-(past_len + Ls) // SEL_BLOCK)
                nc_s = -(-(Lk_s // cmp_blk) // LANES) * LANES
                kc_s = _compress(kcr_s, w_cmp[0], pos_cmp[0], nc_s)
                vc_s = _compress(vcr_s, w_cmp[1], pos_cmp[1], nc_s)
                win_full_s = jnp.concatenate([cache_win, kv_s[:, :, 4 * cw:].reshape(Bs, Ls, 2, kvh, dh)], axis=1)
                Lw = win_cache + Ls
                Lwp = -(-Lw // LANES) * LANES
                wflat = jnp.pad(win_full_s.reshape(Bs, Lw, 2 * cw), ((0, 0), (0, Lwp - Lw), (0, 0))).astype(BF16)
            j = i - n_a
            wq = _head_tiles(nsa_w_in[j][:, :H * dh], H, kvh, dh, axis=1).astype(BF16)
            wg = _pad_cols(nsa_w_in[j][:, H * dh:], LANES).astype(BF16)
            wo = _head_tiles(nsa_w_out[j], H, kvh, dh, axis=0).astype(BF16)
            QW = wq.shape[1]
            o_p = _nsa_attend(_matmul(Xp, wq).reshape(Bp, Lp, QW), _matmul(Xp, wg).reshape(Bp, Lp, LANES),
                              kc_p, vc_p, kvb_p, kvb_p, kvb_p, kvb_p, (0, 0, 2, 3, 4, 5),
                              nq=Q_BLOCK, t0=0, tk=tk_p, n_cmp=Lpp // cmp_blk, cmp_blk=cmp_blk, n_sel=n_sel_p,
                              kvh=kvh, dh=dh, win_rows=min(WINDOW + Q_BLOCK, Lp), w0=0, causal_tiles=True)
            Xp = _matmul_ln(o_p.reshape(Tp, QW), wo, Xp, ln_g[i, 0], ln_b[i, 0], alpha)
            pad_q = lambda v: jnp.pad(v[:Ts].reshape(Bs, Ls, -1), ((0, 0), (0, SUBLANES - Ls), (0, 0)))
            o_s = _nsa_attend(pad_q(_matmul(Xs, wq)), pad_q(_matmul(Xs, wg)), kc_s, vc_s, ks_s, vs_s, wflat, wflat,
                              (0, 0, 0, 0, 0, 1),
                              nq=SUBLANES, t0=past_len, tk=Lk_s, n_cmp=-(-(past_len + Ls) // SEL_BLOCK) * (SEL_BLOCK // cmp_blk),
                              cmp_blk=cmp_blk, n_sel=n_sel_s, kvh=kvh, dh=dh, win_rows=Lwp,
                              w0=past_len - win_cache, causal_tiles=False)
            o_s = jnp.pad(o_s[:, :Ls].reshape(Ts, QW), ((0, Tsp - Ts), (0, 0)))
            Xs = _matmul_ln(o_s, wo, Xs, ln_g[i, 0], ln_b[i, 0], alpha)
        peer = (peer_w_q[i].astype(BF16), peer_sub_keys[i], peer_u[i].astype(BF16), peer_v[i].astype(BF16),
                ln_g[i, 1], ln_b[i, 1], alpha)
        Xp = _peer_layer(Xp, *peer)
        Xs = _peer_layer(Xs, *peer)

    rows_p = kv_p[:, :, :4 * cw].reshape(Bp, Lp, n_rows, kvh, dh)
    rows_s = kv_s[:, :, :4 * cw].reshape(Bs, Ls, n_rows, kvh, dh)
    win_p = kv_p[:, :, 4 * cw:].reshape(Bp, Lp, 2, kvh, dh)[:, -min(WINDOW, Lp):]
    return (Xp.reshape(Bp, Lp, D), Xs[:Ts].reshape(Bs, Ls, D), rows_p, rows_s, win_p,
            win_full_s[:, -win_cache:], jnp.stack(ssm_p), jnp.stack(ssm_s), jnp.stack(conv_p), jnp.stack(conv_s))
```

```python
import functools
import math

import jax
import jax.numpy as jnp
from jax import lax
from jax.experimental import pallas as pl
from jax.experimental.pallas import tpu as pltpu

F32 = jnp.float32
BF16 = jnp.bfloat16
I32 = jnp.int32

DEPTH_ALPHA_POW = 0.25
LN_EPS = 1e-5
SEL_BLOCK = 64
N_SEL = 16
WINDOW = 512
FORCE_SCORE = 1.0e4
PEER_TOPK = 16
SSD_CHUNK = 128
Q_BLOCK = 128
GELU_C0 = math.sqrt(2.0 / math.pi)
GELU_C1 = 0.044715 * GELU_C0
LOG2E = 1.0 / math.log(2.0)
W_TOKENS_PER_ITER = 16

LANES = 128
SUBLANES = 8
VMEM_LIMIT_BYTES = 56 * 1024 * 1024

_NT = (((1,), (1,)), ((), ()))
_TN = (((0,), (0,)), ((), ()))
_HI = lax.Precision.HIGHEST


def _cparams(*sem):
    return pltpu.CompilerParams(dimension_semantics=sem, vmem_limit_bytes=VMEM_LIMIT_BYTES)


def _pick(n, cands):
    for c in cands:
        if n % c == 0:
            return c
    raise ValueError(f"no tile in {cands} divides {n}")


def _pad_cols(w, n):
    return jnp.pad(w, ((0, 0), (0, n - w.shape[1])))


def _mm_kernel(x_ref, w_ref, o_ref):
    o_ref[...] = jnp.dot(x_ref[...].astype(BF16), w_ref[...],
                         preferred_element_type=F32).astype(o_ref.dtype)


def _mm_bias_kernel(x_ref, xb_ref, w_ref, o_ref):
    x = (x_ref[...] + xb_ref[...]).astype(BF16)
    o_ref[...] = jnp.dot(x, w_ref[...], preferred_element_type=F32).astype(o_ref.dtype)


def _matmul(x, w, *, xbias=None, out_dtype=F32, tm_cap=512):
    M, K = x.shape
    N = w.shape[1]
    tm = _pick(M, tuple(t for t in (512, 256, 128, 64, 32, 16, 8) if t <= tm_cap))
    tn = _pick(N, (1024, 512, 256, 128))
    x_spec = pl.BlockSpec((tm, K), lambda j, i: (i, 0))
    w_spec = pl.BlockSpec((K, tn), lambda j, i: (0, j))
    o_spec = pl.BlockSpec((tm, tn), lambda j, i: (i, j))
    if xbias is None:
        body, specs, args = _mm_kernel, [x_spec, w_spec], (x, w)
    else:
        b_spec = pl.BlockSpec((1, K), lambda j, i: (0, 0))
        body, specs, args = _mm_bias_kernel, [x_spec, b_spec, w_spec], (x, xbias, w)
    return pl.pallas_call(
        body, grid=(N // tn, M // tm), in_specs=specs, out_specs=o_spec,
        out_shape=jax.ShapeDtypeStruct((M, N), out_dtype),
        compiler_params=_cparams("parallel", "parallel"), name="proj")(*args)


def _deepnorm_ln(v, g, b):
    mu = jnp.mean(v, axis=-1, keepdims=True)
    d = v - mu
    var = jnp.mean(d * d, axis=-1, keepdims=True)
    return d * lax.rsqrt(var + LN_EPS) * g + b


def _mm_ln_kernel(x_ref, w_ref, r_ref, g_ref, b_ref, o_ref, *, alpha):
    y = jnp.dot(x_ref[...].astype(BF16), w_ref[...], preferred_element_type=F32)
    o_ref[...] = _deepnorm_ln(alpha * r_ref[...] + y, g_ref[...], b_ref[...])


def _matmul_ln(x, w, resid, g, b, alpha):
    M, K = x.shape
    D = w.shape[1]
    tm = _pick(M, (512, 256, 128, 64, 32, 16, 8))
    row = lambda i: (i, 0)
    fixed = lambda i: (0, 0)
    return pl.pallas_call(
        functools.partial(_mm_ln_kernel, alpha=alpha), grid=(M // tm,),
        in_specs=[pl.BlockSpec((tm, K), row), pl.BlockSpec((K, D), fixed),
                  pl.BlockSpec((tm, D), row), pl.BlockSpec((1, D), fixed), pl.BlockSpec((1, D), fixed)],
        out_specs=pl.BlockSpec((tm, D), row),
        out_shape=jax.ShapeDtypeStruct((M, D), F32),
        compiler_params=_cparams("parallel"), name="proj_ln")(x, w, resid, g.reshape(1, D), b.reshape(1, D))


def _softplus(x):
    return jnp.maximum(x, 0.0) + jnp.log1p(jnp.exp(-jnp.abs(x)))


def _silu(x):
    return x * jax.nn.sigmoid(x)


def _ssd_kernel(z_ref, xbc_ref, dtr_ref, cbuf_ref, h0_ref, cw_ref, cb_ref, dtb_ref, alog_ref,
                dsk_ref, nw_ref, y_ref, hn_ref, xf_scr, h_scr, *, Lc, H, P, N, G, n_valid, conv_w):
    c = pl.program_id(1)
    DI = H * P
    HG = H // G
    GP = HG * P
    tail = xf_scr.shape[0] - Lc

    @pl.when(c == 0)
    def _():
        xf_scr[0:tail, :] = cbuf_ref[0]
        h_scr[...] = h0_ref[0]

    xf_scr[tail:tail + Lc, :] = xbc_ref[0]
    conv = cb_ref[...]
    for k in range(conv_w):
        conv = conv + xf_scr[pl.ds(tail - (conv_w - 1) + k, Lc), :] * cw_ref[k:k + 1, :]
    conv = _silu(conv)
    xf_scr[0:tail, :] = xf_scr[pl.ds(Lc, tail), :]

    xs = conv[:, :DI]
    Bm = conv[:, DI:DI + G * N]
    Cm = conv[:, DI + G * N:]

    lane = lax.broadcasted_iota(I32, (Lc, LANES), 1)
    rowi = lax.broadcasted_iota(I32, (Lc, LANES), 0)
    dt = jnp.where((lane < H) & (rowi < n_valid), _softplus(dtr_ref[0] + dtb_ref[...]), 0.0)
    a = dt * (-jnp.exp(alog_ref[...]))
    ti = lax.broadcasted_iota(I32, (Lc, Lc), 0)
    si = lax.broadcasted_iota(I32, (Lc, Lc), 1)
    causal = ti >= si
    tril = jnp.where(causal, 1.0, 0.0).astype(F32)
    eye = jnp.where(lax.broadcasted_iota(I32, (LANES, LANES), 0) ==
                    lax.broadcasted_iota(I32, (LANES, LANES), 1), 1.0, 0.0).astype(F32)
    acum = jnp.dot(tril, a, precision=_HI, preferred_element_type=F32)
    acum_t = lax.dot_general(eye, acum, _NT, precision=_HI, preferred_element_type=F32)
    dt_t = lax.dot_general(eye, dt, _NT, precision=_HI, preferred_element_type=F32)
    alast = acum[Lc - 1:Lc, :]
    wend = jnp.exp(alast - acum) * dt
    ea = jnp.exp(acum)
    edec = jnp.exp(alast)
    dsk = dsk_ref[...]
    first_half = lax.broadcasted_iota(I32, (Lc, 2 * P), 1) < P

    def pair(v, h):
        rows = v.shape[0]
        return jnp.where(first_half[:rows], jnp.broadcast_to(v[:, h:h + 1], (rows, 2 * P)),
                         jnp.broadcast_to(v[:, h + 1:h + 2], (rows, 2 * P)))

    y_parts = []
    for g in range(G):
        Bg = Bm[:, g * N:(g + 1) * N].astype(BF16)
        Cg = Cm[:, g * N:(g + 1) * N].astype(BF16)
        cb = lax.dot_general(Cg, Bg, _NT, preferred_element_type=F32)
        Sg = h_scr[g * GP:(g + 1) * GP, :]
        yoff = lax.dot_general(Cg, Sg.astype(BF16), _NT, preferred_element_type=F32)
        xw_parts = []
        for pr in range(HG // 2):
            h = g * HG + 2 * pr
            xs_pair = xs[:, h * P:(h + 2) * P]
            xs_pair_b = xs_pair.astype(BF16)
            outs = []
            for hh in (h, h + 1):
                seg = acum[:, hh:hh + 1] - acum_t[hh:hh + 1, :]
                dec = jnp.exp(jnp.where(causal, seg, -jnp.inf))
                mix = cb * dec * dt_t[hh:hh + 1, :]
                outs.append(jnp.dot(mix.astype(BF16), xs_pair_b, preferred_element_type=F32))
            ydiag = jnp.where(first_half, outs[0], outs[1])
            y_pair = ydiag + yoff[:, pr * 2 * P:(pr + 1) * 2 * P] * pair(ea, h) + pair(dsk, h) * xs_pair
            y_parts.append(y_pair)
            xw_parts.append(xs_pair * pair(wend, h))
        xw = jnp.concatenate(xw_parts, axis=1).astype(BF16)
        states = lax.dot_general(xw, Bg, _TN, preferred_element_type=F32)
        for hl in range(HG):
            hh = g * HG + hl
            r0 = g * GP + hl * P
            h_scr[r0:r0 + P, :] = edec[:, hh:hh + 1] * Sg[hl * P:(hl + 1) * P, :] + states[hl * P:(hl + 1) * P, :]

    z = z_ref[0]
    gn = DI // G
    outs = []
    for g in range(G):
        parts = y_parts[g * (HG // 2):(g + 1) * (HG // 2)]
        yg = jnp.concatenate(parts, axis=1) if len(parts) > 1 else parts[0]
        yg = yg * _silu(z[:, g * gn:(g + 1) * gn])
        ms = jnp.mean(yg * yg, axis=-1, keepdims=True)
        outs.append(yg * lax.rsqrt(ms + 1e-5))
    y_ref[0] = jnp.concatenate(outs, axis=1) * nw_ref[...]

    @pl.when(c == pl.num_programs(1) - 1)
    def _():
        hn_ref[0] = h_scr[...]


def _ssd(z, xbc, dtr, cbuf, h0, conv_w, conv_b, dt_bias, a_log, d_skip, norm_w, *, Lc, n_valid, G):
    B, L, DI = z.shape
    CD = xbc.shape[-1]
    _, H, P, N = h0.shape
    KW = conv_w.shape[0]
    assert P * 2 == LANES and (H // G) % 2 == 0 and H <= LANES and L % Lc == 0
    padl = lambda v: jnp.pad(v.astype(F32).reshape(1, -1), ((0, 0), (0, LANES - H)))
    seq = lambda b, c: (b, c, 0)
    per_b = lambda b, c: (b, 0, 0)
    fixed = lambda b, c: (0, 0)
    kern = functools.partial(_ssd_kernel, Lc=Lc, H=H, P=P, N=N, G=G, n_valid=n_valid, conv_w=KW)
    y, hn = pl.pallas_call(
        kern, grid=(B, L // Lc),
        in_specs=[pl.BlockSpec((1, Lc, DI), seq), pl.BlockSpec((1, Lc, CD), seq),
                  pl.BlockSpec((1, Lc, LANES), seq), pl.BlockSpec((1, SUBLANES, CD), per_b),
                  pl.BlockSpec((1, H * P, N), per_b),
                  pl.BlockSpec((KW, CD), fixed), pl.BlockSpec((1, CD), fixed),
                  pl.BlockSpec((1, LANES), fixed), pl.BlockSpec((1, LANES), fixed),
                  pl.BlockSpec((1, LANES), fixed), pl.BlockSpec((1, DI), fixed)],
        out_specs=[pl.BlockSpec((1, Lc, DI), seq), pl.BlockSpec((1, H * P, N), per_b)],
        out_shape=[jax.ShapeDtypeStruct((B, L, DI), F32), jax.ShapeDtypeStruct((B, H * P, N), F32)],
        scratch_shapes=[pltpu.VMEM((SUBLANES + Lc, CD), F32), pltpu.VMEM((H * P, N), F32)],
        compiler_params=_cparams("parallel", "arbitrary"), name="ssd",
    )(z, xbc, dtr, cbuf, h0.reshape(B, H * P, N), conv_w, conv_b.reshape(1, CD),
      padl(dt_bias), padl(a_log), padl(d_skip), norm_w.reshape(1, DI))
    return y, hn.reshape(B, H, P, N)


def _mamba_layer(Xp, Xs, dims, state_conv_i, state_ssm_i, params, ln_g, ln_b, alpha):
    Bp, Lp, Bs, Ls = dims
    w_in, conv_w, conv_b, dt_bias, a_log, d_skip, norm_w, w_out = params
    Tp = Bp * Lp
    DI = norm_w.shape[0]
    CD = conv_w.shape[1]
    _, H, P, N = state_ssm_i.shape
    G = (CD - DI) // (2 * N)
    KW = conv_w.shape[0]
    Ts = Bs * Ls
    wz = w_in[:, :DI].astype(BF16)
    wx = w_in[:, DI:DI + CD].astype(BF16)
    wd = _pad_cols(w_in[:, DI + CD:], LANES).astype(BF16)
    wo = w_out.astype(BF16)
    args = (conv_w, conv_b, dt_bias, a_log, d_skip, norm_w)

    xbc_p = _matmul(Xp, wx).reshape(Bp, Lp, CD)
    Lc = SSD_CHUNK if Lp % SSD_CHUNK == 0 else Lp
    yp, hp = _ssd(_matmul(Xp, wz).reshape(Bp, Lp, DI), xbc_p, _matmul(Xp, wd).reshape(Bp, Lp, LANES),
                  jnp.zeros((Bp, SUBLANES, CD), F32), jnp.zeros((Bp, H, P, N), F32), *args,
                  Lc=Lc, n_valid=Lc, G=G)
    conv_p = xbc_p[:, Lp - (KW - 1):]
    Xp = _matmul_ln(yp.reshape(Tp, DI), wo, Xp, ln_g, ln_b, alpha)

    Lsp = -(-Ls // SUBLANES) * SUBLANES
    pad_rows = lambda v: jnp.pad(v[:Ts].reshape(Bs, Ls, -1), ((0, 0), (0, Lsp - Ls), (0, 0)))
    xbc_s = _matmul(Xs, wx)[:Ts].reshape(Bs, Ls, CD)
    cbuf_s = jnp.pad(state_conv_i, ((0, 0), (SUBLANES - (KW - 1), 0), (0, 0)))
    ys, hs = _ssd(pad_rows(_matmul(Xs, wz)), pad_rows(xbc_s), pad_rows(_matmul(Xs, wd)), cbuf_s, state_ssm_i,
                  *args, Lc=Lsp, n_valid=Ls, G=G)
    conv_s = jnp.concatenate([state_conv_i, xbc_s], axis=1)[:, -(KW - 1):]
    ys = jnp.pad(ys[:, :Ls].reshape(Ts, DI), ((0, Xs.shape[0] - Ts), (0, 0)))
    Xs = _matmul_ln(ys, wo, Xs, ln_g, ln_b, alpha)
    return Xp, Xs, hp, hs, conv_p, conv_s


def _topk_rows(s, k):
    n = s.shape[0]
    rows = lax.broadcasted_iota(I32, s.shape, 0)
    vals, idxs = [], []
    for _ in range(k):
        m = jnp.max(s, axis=0, keepdims=True)
        i = jnp.min(jnp.where(s == m, rows, n), axis=0, keepdims=True)
        vals.append(m)
        idxs.append(i)
        s = jnp.where(rows == i, -jnp.inf, s)
    return vals, idxs


def _peer_topk_kernel(q_ref, k_ref, i1_ref, i2_ref, g_ref, *, heads, half, topk, n_keys):
    k1 = k_ref[0]
    k2 = k_ref[1]
    tb = q_ref.shape[0]
    cand_ab = [(a, b) for a in range(topk) for b in range(topk // (a + 1))]
    n_pad = -len(cand_ab) % SUBLANES
    codes, gates = [], []
    for h in range(heads):
        q1 = q_ref[:, (2 * h) * half:(2 * h + 1) * half]
        q2 = q_ref[:, (2 * h + 1) * half:(2 * h + 2) * half]
        s1 = lax.dot_general(k1, q1, _NT, preferred_element_type=F32)
        s2 = lax.dot_general(k2, q2, _NT, preferred_element_type=F32)
        v1, i1 = _topk_rows(s1, topk)
        v2, i2 = _topk_rows(s2, topk)
        cand = jnp.concatenate([v1[a] + v2[b] for a, b in cand_ab] +
                               [jnp.full((n_pad, tb), -jnp.inf, F32)], axis=0)
        code = jnp.concatenate([i1[a] * n_keys + i2[b] for a, b in cand_ab] +
                               [jnp.zeros((n_pad, tb), I32)], axis=0)
        sc, pos = _topk_rows(cand, topk)
        crow = lax.broadcasted_iota(I32, cand.shape, 0)
        codes += [jnp.max(jnp.where(crow == p, code, -1), axis=0, keepdims=True) for p in pos]
        ex = jnp.exp(jnp.concatenate(sc, axis=0) - sc[0])
        gates.append(ex / jnp.sum(ex, axis=0, keepdims=True))
    code_t = jnp.concatenate(codes, axis=0).T
    i1_ref[...] = code_t // n_keys
    i2_ref[...] = code_t % n_keys
    g_ref[...] = jnp.concatenate(gates, axis=0).T


def _peer_w_kernel(i1_ref, i2_ref, g_ref, w_ref, *, n_keys, nj):
    sub = lax.broadcasted_iota(I32, (n_keys, i1_ref.shape[1]), 0)

    def body(tb, carry):
        for u in range(W_TOKENS_PER_ITER):
            t = tb * W_TOKENS_PER_ITER + u
            i1 = i1_ref[pl.ds(t, 1), :]
            i2 = i2_ref[pl.ds(t, 1), :]
            g = g_ref[pl.ds(t, 1), :]
            onehot1 = jnp.where(sub == i1, 1.0, 0.0).astype(BF16)
            gated2 = jnp.where(sub == i2, g, 0.0).astype(BF16)
            w_t = lax.dot_general(onehot1, gated2, _NT, preferred_element_type=F32)
            r0 = pl.multiple_of(t * nj, nj)
            for k in range(n_keys // nj):
                w_ref[0, k, pl.ds(r0, nj), :] = w_t[k * nj:(k + 1) * nj, :]
        return carry

    lax.fori_loop(0, i1_ref.shape[0] // W_TOKENS_PER_ITER, body, 0)


def _gelu_tanh(x):
    hx = 0.5 * x
    inner = x * (GELU_C0 + GELU_C1 * (x * x))
    return hx * jnp.tanh(inner) + hx


def _peer_dense_kernel(x_ref, u_ref, v_ref, w_ref, g_ref, b_ref, o_ref, acc_ref, xb_ref, *, alpha, nj):
    k = pl.program_id(1)
    td = x_ref.shape[0]

    @pl.when(k == 0)
    def _():
        acc_ref[...] = jnp.zeros_like(acc_ref)
        xb_ref[...] = x_ref[...].astype(BF16)

    act = lax.dot_general(xb_ref[...], u_ref[...], _NT, preferred_element_type=F32)
    w = jnp.concatenate([w_ref[0, 0, pl.ds(j, td, stride=nj), :] for j in range(nj)], axis=1)
    coef = (_gelu_tanh(act) * w).astype(BF16)
    acc_ref[...] += jnp.dot(coef, v_ref[...], preferred_element_type=F32)

    @pl.when(k == pl.num_programs(1) - 1)
    def _():
        o_ref[...] = _deepnorm_ln(alpha * x_ref[...] + acc_ref[...], g_ref[...], b_ref[...])


def _peer_layer(X, w_q, sub_keys, u_tab, v_tab, ln_g, ln_b, alpha):
    T, D = X.shape
    _, n_keys, half = sub_keys.shape
    heads = w_q.shape[1] // (2 * half)
    nsel = heads * PEER_TOPK
    assert n_keys == LANES and nsel == LANES and half % LANES == 0
    q = _matmul(X, w_q)

    tb = LANES
    tok = lambda i: (i, 0)
    sel_shape = jax.ShapeDtypeStruct((T, nsel), I32)
    i1, i2, gate = pl.pallas_call(
        functools.partial(_peer_topk_kernel, heads=heads, half=half, topk=PEER_TOPK, n_keys=n_keys),
        grid=(T // tb,),
        in_specs=[pl.BlockSpec((tb, q.shape[1]), tok), pl.BlockSpec((2, n_keys, half), lambda i: (0, 0, 0))],
        out_specs=[pl.BlockSpec((tb, nsel), tok)] * 3,
        out_shape=[sel_shape, sel_shape, jax.ShapeDtypeStruct((T, nsel), F32)],
        compiler_params=_cparams("parallel"), name="peer_topk")(q, sub_keys)

    td = _pick(T, (1024, 768, 512, 256, 128))
    nj = SUBLANES
    et = nj * n_keys
    sub_blocks = td // tb
    w = pl.pallas_call(
        functools.partial(_peer_w_kernel, n_keys=n_keys, nj=nj), grid=(T // tb,),
        in_specs=[pl.BlockSpec((tb, nsel), tok)] * 3,
        out_specs=pl.BlockSpec((1, n_keys // nj, tb * nj, n_keys),
                               lambda i: (i // sub_blocks, 0, i % sub_blocks, 0)),
        out_shape=jax.ShapeDtypeStruct((T // td, n_keys // nj, td * nj, n_keys), F32),
        compiler_params=_cparams("parallel"), name="peer_w")(i1, i2, gate)

    return pl.pallas_call(
        functools.partial(_peer_dense_kernel, alpha=alpha, nj=nj),
        grid=(T // td, n_keys // nj),
        in_specs=[pl.BlockSpec((td, D), lambda i, k: (i, 0)),
                  pl.BlockSpec((et, D), lambda i, k: (k, 0)),
                  pl.BlockSpec((et, D), lambda i, k: (k, 0)),
                  pl.BlockSpec((1, 1, td * nj, n_keys), lambda i, k: (i, k, 0, 0)),
                  pl.BlockSpec((1, D), lambda i, k: (0, 0)), pl.BlockSpec((1, D), lambda i, k: (0, 0))],
        out_specs=pl.BlockSpec((td, D), lambda i, k: (i, 0)),
        out_shape=jax.ShapeDtypeStruct((T, D), F32),
        scratch_shapes=[pltpu.VMEM((td, D), F32), pltpu.VMEM((td, D), BF16)],
        compiler_params=_cparams("parallel", "arbitrary"), name="peer_dense",
    )(X, u_tab, v_tab, w, ln_g.reshape(1, D), ln_b.reshape(1, D))


def _gather_kernel(pt_ref, pa_ref, pb_ref, new_ref, kc_ref, vc_ref, ks_ref, vs_ref, col_scr, *, n_steps, cw, blk):
    del pt_ref
    p = pl.program_id(1)
    ps = pa_ref.shape[1]
    per_page = ps // blk

    @pl.when(p < n_steps)
    def _():
        for j, page in enumerate((pa_ref, pb_ref)):
            n_tiles = 2 * cw // LANES
            for c in range(n_tiles):
                col_scr[j * n_tiles + c] = page[0, :, c * LANES:(c + 1) * LANES]
            for r, ref in enumerate((kc_ref, vc_ref)):
                for l in range(blk):
                    for c in range(cw // LANES):
                        ref[0, j * per_page:(j + 1) * per_page, l * cw + c * LANES:l * cw + (c + 1) * LANES] = \
                            col_scr[j * n_tiles + r * (cw // LANES) + c, pl.ds(l, per_page, stride=blk), :]
            ks_ref[0, j * ps:(j + 1) * ps, :] = page[0, :, 2 * cw:3 * cw].astype(BF16)
            vs_ref[0, j * ps:(j + 1) * ps, :] = page[0, :, 3 * cw:4 * cw].astype(BF16)

    @pl.when(p == n_steps)
    def _():
        new = new_ref[0]
        n_new = new.shape[0]
        for r, ref in enumerate((kc_ref, vc_ref)):
            ref[0] = jnp.zeros(ref.shape[1:], F32)
            for l in range(n_new):
                ref[0, 0:1, l * cw:(l + 1) * cw] = new[l:l + 1, r * cw:(r + 1) * cw]
        for r, ref in ((2, ks_ref), (3, vs_ref)):
            rows = jnp.concatenate([new[:, r * cw:(r + 1) * cw], jnp.zeros((2 * ps - n_new, cw), F32)], axis=0)
            ref[0] = rows.astype(BF16)


def _gather_pages(pages, page_table, new_rows, blk):
    NP, PS, C = pages.shape
    B, n_pages = page_table.shape
    assert n_pages % 2 == 0 and PS % blk == 0 and (2 * PS // blk) % SUBLANES == 0 and new_rows.shape[1] <= blk
    cw = C // 4
    n_steps = n_pages // 2
    L = (n_pages + 2) * PS
    nb = 2 * PS // blk
    page_spec = lambda j: pl.BlockSpec(
        (1, PS, C), lambda b, p, pt: (pt[b, jnp.minimum(2 * p + j, n_pages - 2 + j)], 0, 0))
    cmp_spec = pl.BlockSpec((1, nb, blk * cw), lambda b, p, pt: (b, p, 0))
    sel_spec = pl.BlockSpec((1, 2 * PS, cw), lambda b, p, pt: (b, p, 0))
    return pl.pallas_call(
        functools.partial(_gather_kernel, n_steps=n_steps, cw=cw, blk=blk),
        grid_spec=pltpu.PrefetchScalarGridSpec(
            num_scalar_prefetch=1, grid=(B, n_steps + 1),
            in_specs=[page_spec(0), page_spec(1),
                      pl.BlockSpec((1, new_rows.shape[1], C), lambda b, p, pt: (b, 0, 0))],
            out_specs=[cmp_spec, cmp_spec, sel_spec, sel_spec],
            scratch_shapes=[pltpu.VMEM((2 * (2 * cw // LANES), PS, LANES), F32)]),
        out_shape=[jax.ShapeDtypeStruct((B, L // blk, blk * cw), F32)] * 2 +
                  [jax.ShapeDtypeStruct((B, L, cw), BF16)] * 2,
        compiler_params=_cparams("parallel", "arbitrary"), name="kv_gather")(page_table, pages, pages, new_rows)


def _compress_blocks(xb, w_c, pos_c, nc_pad):
    B, n, K = xb.shape
    blk, dh, _ = w_c.shape
    C = K // blk
    G = C // dh
    w_bd = jnp.einsum('lde,gh->lgdhe', w_c, jnp.eye(G, dtype=w_c.dtype)).reshape(K, C)
    xbias = jnp.broadcast_to(pos_c[:, None, :], (blk, G, dh)).reshape(1, K)
    out = _matmul(xb.reshape(B * n, K), w_bd.astype(BF16), xbias=xbias, tm_cap=128)
    return jnp.pad(out.reshape(B, n, C), ((0, 0), (0, nc_pad - n), (0, 0)))


def _compress(x, w_c, pos_c, nc_pad):
    B, L, C = x.shape
    blk = w_c.shape[0]
    return _compress_blocks(x.reshape(B, L // blk, blk * C), w_c, pos_c, nc_pad)


def _lane_tiles(x):
    return [x[:, j * LANES:(j + 1) * LANES] for j in range(x.shape[1] // LANES)]


def _row_max(x):
    return jnp.max(functools.reduce(jnp.maximum, _lane_tiles(x)), axis=-1, keepdims=True)


def _row_sum(x):
    return jnp.sum(functools.reduce(jnp.add, _lane_tiles(x)), axis=-1, keepdims=True)


def _split3_dot(x, m):
    x1 = x.astype(BF16)
    r1 = x - x1.astype(F32)
    x2 = r1.astype(BF16)
    x3 = (r1 - x2.astype(F32)).astype(BF16)
    d = lambda a: jnp.dot(a, m, preferred_element_type=F32)
    return d(x1) + d(x2) + d(x3)


def _nsa_staged_kernel(q_ref, gt_ref, kc_ref, vc_ref, ks_ref, vs_ref, kw_ref, vw_ref, ex_ref, o_ref, *,
                       seqs, nq, t0, tk, n_cmp, cmp_blk, n_sel, kvh, hg, dh, slopes, win_rows, w0):
    q0 = t0 + pl.program_id(1) * nq
    R = hg * nq
    NC = kc_ref.shape[1]
    Lk = ks_ref.shape[1]
    NSP = ex_ref.shape[0]
    assert tk == Lk
    qscale = dh ** -0.5 * LOG2E
    slopes2 = tuple(s * LOG2E for s in slopes)
    units = [(s, g) for s in range(seqs) for g in range(kvh)]
    col = lambda g: (g // 2) * 2 * dh

    def tq_like(shape):
        return q0 + lax.broadcasted_iota(I32, shape, 0)

    def attend(keys, vals, distms):
        s_alls = [lax.dot_general(qgs[u], keys[u], _NT, preferred_element_type=F32) for u in range(len(units))]
        s2s = [[s_alls[u][h * nq:(h + 1) * nq] - slopes2[g * hg + h] * distms[u] for h in range(hg)]
               for u, (_, g) in enumerate(units)]
        ms = [[_row_max(x) for x in row] for row in s2s]
        es = [[jnp.exp2(x - jnp.where(m == -jnp.inf, 0.0, m)) for x, m in zip(xr, mr)] for xr, mr in zip(s2s, ms)]
        ls = [[_row_sum(e) for e in row] for row in es]
        ps = [[e / jnp.maximum(l, 1e-30) for e, l in zip(er, lr)] for er, lr in zip(es, ls)]
        os = [jnp.dot(jnp.concatenate(ps[u], axis=0).astype(BF16), vals[u], preferred_element_type=F32)
              for u in range(len(units))]
        return ps, os

    n_idx = lax.broadcasted_iota(I32, (nq, NC), 1)
    dist_c = tq_like((nq, NC)) - (n_idx * cmp_blk + (cmp_blk - 1))
    distm_c = jnp.where((dist_c >= 0) & (n_idx < n_cmp), dist_c.astype(F32), jnp.inf)
    dist_w = tq_like((nq, win_rows)) - (w0 + lax.broadcasted_iota(I32, (nq, win_rows), 1))
    distm_w = jnp.where((dist_w >= 0) & (dist_w < WINDOW), dist_w.astype(F32), jnp.inf)
    dist_s = tq_like((nq, Lk)) - lax.broadcasted_iota(I32, (nq, Lk), 1)
    per_sel = SEL_BLOCK // cmp_blk
    pair = jnp.where(lax.broadcasted_iota(I32, (NC, NSP), 0) // per_sel ==
                     lax.broadcasted_iota(I32, (NC, NSP), 1), 1.0, 0.0).astype(BF16)
    blk = lax.broadcasted_iota(I32, (nq, NSP), 1)
    tqs = tq_like((nq, NSP))
    cur = tqs // SEL_BLOCK
    forced = (blk == 0) | (blk == cur) | (blk == cur - 1)
    sel_valid = blk * SEL_BLOCK <= tqs

    qgs = [(jnp.concatenate([q_ref[s, :, (g * hg + h) * 2 * dh:(g * hg + h + 1) * 2 * dh] for h in range(hg)],
                            axis=0) * qscale).astype(BF16) for s, g in units]
    tile = lambda ref, s, g: ref[s, :, col(g):col(g) + 2 * dh]
    p_cmp, o_cmp = attend([tile(kc_ref, s, g).astype(BF16) for s, g in units],
                          [tile(vc_ref, s, g).astype(BF16) for s, g in units], [distm_c] * len(units))
    _, o_win = attend([tile(kw_ref, s, g) for s, g in units], [tile(vw_ref, s, g) for s, g in units],
                      [distm_w] * len(units))

    psums = [functools.reduce(jnp.add, row) for row in p_cmp]
    imps = [_split3_dot(x, pair) for x in psums]
    scores = [jnp.where(blk < n_sel, jnp.where(forced, FORCE_SCORE, jnp.where(sel_valid, imp, -1.0)), -2.0)
              for imp in imps]
    selm = _top_blocks(jnp.concatenate(scores, axis=0), min(N_SEL, n_sel), NSP)
    selb = selm.astype(BF16)
    expand = ex_ref[...]
    selxs = [jnp.dot(selb[u * nq:(u + 1) * nq], expand, preferred_element_type=F32) for u in range(len(units))]
    distm_s = [jnp.where((dist_s >= 0) & (x > 0.5), dist_s.astype(F32), jnp.inf) for x in selxs]
    _, o_sel = attend([tile(ks_ref, s, g) for s, g in units], [tile(vs_ref, s, g) for s, g in units], distm_s)

    lane = lax.broadcasted_iota(I32, (nq, 2 * dh), 1)
    for s in range(seqs):
        gates = jax.nn.sigmoid(gt_ref[s])
        for g in range(kvh):
            u = s * kvh + g
            lo = (g % 2) * dh
            keep = (lane >= lo) & (lane < lo + dh)
            for h in range(hg):
                hd = g * hg + h
                rows = slice(h * nq, (h + 1) * nq)
                o = (gates[:, 3 * hd:3 * hd + 1] * o_cmp[u][rows] + gates[:, 3 * hd + 1:3 * hd + 2] * o_sel[u][rows] +
                     gates[:, 3 * hd + 2:3 * hd + 3] * o_win[u][rows])
                o_ref[s, :, hd * 2 * dh:(hd + 1) * 2 * dh] = jnp.where(keep, o, 0.0)


def _top_blocks(score, n_top, nsp):
    if score.shape[0] % LANES == 0 and nsp == LANES:
        st = score.T
        brow = lax.broadcasted_iota(I32, st.shape, 0)
        sel_t = jnp.zeros(st.shape, F32)
        for _ in range(n_top):
            m = jnp.max(st, axis=0, keepdims=True)
            idx = jnp.min(jnp.where(st == m, brow, nsp), axis=0, keepdims=True)
            hit = brow == idx
            sel_t = jnp.where(hit, 1.0, sel_t)
            st = jnp.where(hit, -jnp.inf, st)
        return sel_t.T
    bcol = lax.broadcasted_iota(I32, score.shape, 1)
    selm = jnp.zeros(score.shape, F32)
    for _ in range(n_top):
        m = jnp.max(score, axis=-1, keepdims=True)
        idx = jnp.min(jnp.where(score == m, bcol, nsp), axis=-1, keepdims=True)
        hit = bcol == idx
        selm = jnp.where(hit, 1.0, selm)
        score = jnp.where(hit, -jnp.inf, score)
    return selm


def _nsa_multi_kernel(q_ref, gt_ref, kc_ref, vc_ref, ks_ref, vs_ref, kw_ref, vw_ref, ex_ref, o_ref, *scratch,
                      seqs, **cfg):
    for s in range(seqs):
        one = lambda r: r.at[pl.ds(s, 1)]
        _nsa_kernel(one(q_ref), one(gt_ref), one(kc_ref), one(vc_ref), one(ks_ref), one(vs_ref), one(kw_ref),
                    one(vw_ref), ex_ref, one(o_ref), *scratch, **cfg)


def _nsa_kernel(q_ref, gt_ref, kc_ref, vc_ref, ks_ref, vs_ref, kw_ref, vw_ref, ex_ref, o_ref,
                m_scr, l_scr, acc_scr, flag_ref, *,
                nq, t0, tk, n_cmp, cmp_blk, n_sel, kvh, hg, dh, slopes, win_rows, w0, causal_tiles):
    i = pl.program_id(1)
    q0 = t0 + i * nq
    R = hg * nq
    NC = kc_ref.shape[1]
    Lk = ks_ref.shape[1]
    NSP = ex_ref.shape[0]
    qscale = dh ** -0.5 * LOG2E
    slopes2 = tuple(s * LOG2E for s in slopes)
    gates = jax.nn.sigmoid(gt_ref[0])

    def tq_like(shape):
        return q0 + lax.broadcasted_iota(I32, shape, 0)

    if causal_tiles:
        wstart = pl.multiple_of(jnp.maximum(q0 - WINDOW, 0), nq)
    else:
        wstart = 0

    def col(g):
        return (g // 2) * 2 * dh

    def branch(qg, k_t, v_t, distm, g):
        s_all = lax.dot_general(qg, k_t, _NT, preferred_element_type=F32)
        ps = []
        for h in range(hg):
            s2 = s_all[h * nq:(h + 1) * nq] - slopes2[g * hg + h] * distm
            m = _row_max(s2)
            e = jnp.exp2(s2 - jnp.where(m == -jnp.inf, 0.0, m))
            ps.append(e / jnp.maximum(_row_sum(e), 1e-30))
        o = jnp.dot(jnp.concatenate(ps, axis=0).astype(BF16), v_t, preferred_element_type=F32)
        return ps, o

    n_idx = lax.broadcasted_iota(I32, (nq, NC), 1)
    dist_c = tq_like((nq, NC)) - (n_idx * cmp_blk + (cmp_blk - 1))
    distm_c = jnp.where((dist_c >= 0) & (n_idx < n_cmp), dist_c.astype(F32), jnp.inf)
    dist_w = tq_like((nq, win_rows)) - (w0 + wstart + lax.broadcasted_iota(I32, (nq, win_rows), 1))
    distm_w = jnp.where((dist_w >= 0) & (dist_w < WINDOW), dist_w.astype(F32), jnp.inf)
    per_sel = SEL_BLOCK // cmp_blk
    pair = jnp.where(lax.broadcasted_iota(I32, (NC, NSP), 0) // per_sel ==
                     lax.broadcasted_iota(I32, (NC, NSP), 1), 1.0, 0.0).astype(BF16)
    blk = lax.broadcasted_iota(I32, (nq, NSP), 1)
    tqs = tq_like((nq, NSP))
    cur = tqs // SEL_BLOCK
    forced = (blk == 0) | (blk == cur) | (blk == cur - 1)
    sel_valid = blk * SEL_BLOCK <= tqs

    qgs, o_cmps, o_wins, scores = [], [], [], []
    for g in range(kvh):
        c0 = col(g)
        qg = jnp.concatenate([q_ref[0, :, (g * hg + h) * 2 * dh:(g * hg + h + 1) * 2 * dh] for h in range(hg)],
                             axis=0)
        qg = (qg * qscale).astype(BF16)
        qgs.append(qg)
        p_cmp, o_cmp = branch(qg, kc_ref[0, :, c0:c0 + 2 * dh].astype(BF16),
                              vc_ref[0, :, c0:c0 + 2 * dh].astype(BF16), distm_c, g)
        o_cmps.append(o_cmp)
        psum = p_cmp[0]
        for h in range(1, hg):
            psum = psum + p_cmp[h]
        imp = _split3_dot(psum, pair)
        score = jnp.where(forced, FORCE_SCORE, jnp.where(sel_valid, imp, -1.0))
        scores.append(jnp.where(blk < n_sel, score, -2.0))
        _, o_win = branch(qg, kw_ref[0, pl.ds(wstart, win_rows), c0:c0 + 2 * dh],
                          vw_ref[0, pl.ds(wstart, win_rows), c0:c0 + 2 * dh], distm_w, g)
        o_wins.append(o_win)

    selm = _top_blocks(jnp.concatenate(scores, axis=0), min(N_SEL, n_sel), NSP)
    selb = selm.astype(BF16)

    def sel_tile(kt, carry):
        k0 = kt * tk
        dist = tq_like((nq, tk)) - (k0 + lax.broadcasted_iota(I32, (nq, tk), 1))
        causal = dist >= 0
        distf = dist.astype(F32)
        expand = ex_ref[:, pl.ds(k0, tk)]
        new = []
        for g in range(kvh):
            m_run, l_run, acc = carry[g]
            c0 = col(g)
            s_t = lax.dot_general(qgs[g], ks_ref[0, pl.ds(k0, tk), c0:c0 + 2 * dh], _NT,
                                  preferred_element_type=F32)
            selx = jnp.dot(selb[g * nq:(g + 1) * nq], expand, preferred_element_type=F32)
            distm = jnp.where(causal & (selx > 0.5), distf, jnp.inf)
            m_new, l_new, alphas, ps = [], [], [], []
            for h in range(hg):
                rows = slice(h * nq, (h + 1) * nq)
                s2 = s_t[rows] - slopes2[g * hg + h] * distm
                m_h = jnp.maximum(m_run[rows], _row_max(s2))
                m_safe = jnp.where(m_h == -jnp.inf, 0.0, m_h)
                a_h = jnp.exp2(m_run[rows] - m_safe)
                p_h = jnp.exp2(s2 - m_safe)
                m_new.append(m_h)
                l_new.append(a_h * l_run[rows] + _row_sum(p_h))
                alphas.append(a_h)
                ps.append(p_h)
            pv = jnp.dot(jnp.concatenate(ps, axis=0).astype(BF16), vs_ref[0, pl.ds(k0, tk), c0:c0 + 2 * dh],
                         preferred_element_type=F32)
            new.append((jnp.concatenate(m_new, axis=0), jnp.concatenate(l_new, axis=0),
                        jnp.concatenate(alphas, axis=0) * acc + pv))
        return tuple(new)

    def sel_tile_group(kt, g):
        k0 = pl.multiple_of(kt * tk, tk)
        c0 = col(g)
        dist = tq_like((nq, tk)) - (k0 + lax.broadcasted_iota(I32, (nq, tk), 1))
        selx = jnp.dot(selb[g * nq:(g + 1) * nq], ex_ref[:, pl.ds(k0, tk)], preferred_element_type=F32)
        distm = jnp.where((dist >= 0) & (selx > 0.5), dist.astype(F32), jnp.inf)
        s_t = lax.dot_general(qgs[g], ks_ref[0, pl.ds(k0, tk), c0:c0 + 2 * dh], _NT,
                              preferred_element_type=F32)
        m_run = m_scr[g]
        l_run = l_scr[g]
        m_new, l_new, alphas, ps = [], [], [], []
        for h in range(hg):
            rows = slice(h * nq, (h + 1) * nq)
            s2 = s_t[rows] - slopes2[g * hg + h] * distm
            m_h = jnp.maximum(m_run[rows], _row_max(s2))
            m_safe = jnp.where(m_h == -jnp.inf, 0.0, m_h)
            a_h = jnp.exp2(m_run[rows] - m_safe)
            p_h = jnp.exp2(s2 - m_safe)
            m_new.append(m_h)
            l_new.append(a_h * l_run[rows] + _row_sum(p_h))
            alphas.append(a_h)
            ps.append(p_h)
        pv = jnp.dot(jnp.concatenate(ps, axis=0).astype(BF16), vs_ref[0, pl.ds(k0, tk), c0:c0 + 2 * dh],
                     preferred_element_type=F32)
        m_scr[g] = jnp.concatenate(m_new, axis=0)
        l_scr[g] = jnp.concatenate(l_new, axis=0)
        acc_scr[g] = jnp.concatenate(alphas, axis=0) * acc_scr[g] + pv

    if causal_tiles:
        n_flag = Lk // tk
        blocks_per_tile = tk // SEL_BLOCK
        lane_b = lax.broadcasted_iota(I32, (1, NSP), 1)
        for g in range(kvh):
            any_q = jnp.max(selm[g * nq:(g + 1) * nq], axis=0, keepdims=True)
            for kt in range(n_flag):
                in_tile = (lane_b >= kt * blocks_per_tile) & (lane_b < (kt + 1) * blocks_per_tile)
                flag_ref[g * n_flag + kt] = (jnp.max(jnp.where(in_tile, any_q, 0.0)) > 0.5).astype(I32)
        m_scr[...] = jnp.full(m_scr.shape, -jnp.inf, F32)
        l_scr[...] = jnp.zeros(l_scr.shape, F32)
        acc_scr[...] = jnp.zeros(acc_scr.shape, F32)

        def visit(kt, carry):
            for g in range(kvh):
                @pl.when(flag_ref[g * n_flag + kt] > 0)
                def _():
                    sel_tile_group(kt, g)
            return carry

        lax.fori_loop(0, (q0 + nq + tk - 1) // tk, visit, 0)
        fin = tuple((None, l_scr[g], acc_scr[g]) for g in range(kvh))
    else:
        fin = tuple((jnp.full((R, 1), -jnp.inf, F32), jnp.zeros((R, 1), F32), jnp.zeros((R, 2 * dh), F32))
                    for _ in range(kvh))
        for kt in range(Lk // tk):
            fin = sel_tile(kt, fin)

    lane = lax.broadcasted_iota(I32, (nq, 2 * dh), 1)
    for g in range(kvh):
        lo = (g % 2) * dh
        keep = (lane >= lo) & (lane < lo + dh)
        _, l_fin, acc = fin[g]
        o_sel = acc / jnp.maximum(l_fin, 1e-30)
        for h in range(hg):
            hd = g * hg + h
            rows = slice(h * nq, (h + 1) * nq)
            o = (gates[:, 3 * hd:3 * hd + 1] * o_cmps[g][rows] + gates[:, 3 * hd + 1:3 * hd + 2] * o_sel[rows] +
                 gates[:, 3 * hd + 2:3 * hd + 3] * o_wins[g][rows])
            o_ref[0, :, hd * 2 * dh:(hd + 1) * 2 * dh] = jnp.where(keep, o, 0.0)


def _nsa_attend(q, gt, kc, vc, ks, vs, kw, vw, cols, *, nq, t0, tk, n_cmp, cmp_blk, n_sel, kvh, dh,
                win_rows, w0, causal_tiles, seqs=1):
    B, Lq, QW = q.shape
    H = QW // (2 * dh)
    hg = H // kvh
    cw = kvh * dh
    Lk = ks.shape[1]
    NC = kc.shape[1]
    nsp = LANES
    assert n_sel <= nsp and Lk % tk == 0 and Lq % nq == 0 and dh * 2 == LANES
    expand = (lax.broadcasted_iota(I32, (nsp, Lk), 1) // SEL_BLOCK ==
              lax.broadcasted_iota(I32, (nsp, Lk), 0)).astype(BF16)
    slopes = tuple(2.0 ** (-8.0 * (h + 1) / H) for h in range(H))
    assert B % seqs == 0
    cfg = dict(seqs=seqs, nq=nq, t0=t0, tk=tk, n_cmp=n_cmp, cmp_blk=cmp_blk, n_sel=n_sel, kvh=kvh, hg=hg, dh=dh,
               slopes=slopes, win_rows=win_rows, w0=w0)
    R = hg * nq
    if causal_tiles:
        kern = functools.partial(_nsa_multi_kernel, causal_tiles=True, **cfg)
        scratch = [pltpu.VMEM((kvh, R, 1), F32), pltpu.VMEM((kvh, R, 1), F32),
                   pltpu.VMEM((kvh, R, 2 * dh), F32), pltpu.SMEM((kvh * (Lk // tk),), I32)]
    else:
        assert win_rows == kw.shape[1]
        kern = functools.partial(_nsa_staged_kernel, **cfg)
        scratch = []
    qmap = lambda b, i: (b, i, 0)
    kv_spec = lambda arr, c: pl.BlockSpec((seqs, arr.shape[1], cw), lambda b, i, c=c: (b, 0, c))
    return pl.pallas_call(
        kern, grid=(B // seqs, Lq // nq),
        in_specs=[pl.BlockSpec((seqs, nq, QW), qmap), pl.BlockSpec((seqs, nq, LANES), qmap),
                  kv_spec(kc, cols[0]), kv_spec(vc, cols[1]), kv_spec(ks, cols[2]), kv_spec(vs, cols[3]),
                  kv_spec(kw, cols[4]), kv_spec(vw, cols[5]),
                  pl.BlockSpec((nsp, Lk), lambda b, i: (0, 0))],
        out_specs=pl.BlockSpec((seqs, nq, QW), qmap),
        out_shape=jax.ShapeDtypeStruct((B, Lq, QW), F32),
        scratch_shapes=scratch,
        compiler_params=_cparams("parallel", "arbitrary"), name="nsa_attend")(q, gt, kc, vc, ks, vs, kw, vw, expand)


def _head_tiles(w, H, kvh, dh, axis):
    hg = H // kvh
    parts = []
    for h in range(H):
        sl = [slice(None)] * w.ndim
        sl[axis] = slice(h * dh, (h + 1) * dh)
        piece = w[tuple(sl)]
        zero = jnp.zeros_like(piece)
        parts += [zero, piece] if ((h // hg) % 2) else [piece, zero]
    return jnp.concatenate(parts, axis=axis)


def kernel(x_prompt, x_sample, cache_kv_pages, cache_win, state_ssm, state_conv, page_table, ln_g, ln_b, m_w_in, m_conv_w, m_conv_b, m_dt_bias, m_a_log, m_d_skip, m_norm_w, m_w_out, w_kv_shared, w_cmp, pos_cmp, nsa_w_in, nsa_w_out, peer_w_q, peer_sub_keys, peer_u, peer_v):
    Bp, Lp, D = x_prompt.shape
    Bs, Ls, _ = x_sample.shape
    depth = ln_g.shape[0]
    n_a = m_w_in.shape[0]
    alpha = (2 * depth) ** DEPTH_ALPHA_POW
    Tp, Ts = Bp * Lp, Bs * Ls
    assert Tp % LANES == 0
    Tsp = -(-Ts // LANES) * LANES
    Xp = x_prompt.reshape(Tp, D)
    Xs = jnp.pad(x_sample.reshape(Ts, D), ((0, Tsp - Ts), (0, 0)))
    dims = (Bp, Lp, Bs, Ls)

    NP, PS, n_rows, kvh, dh = cache_kv_pages.shape
    cw = kvh * dh
    n_pages = page_table.shape[1]
    past_len = n_pages * PS
    win_cache = cache_win.shape[1]
    cmp_blk = w_cmp.shape[1]
    H = nsa_w_in.shape[2] // (dh + 3)

    ssm_p, ssm_s, conv_p, conv_s = [], [], [], []
    for i in range(depth):
        if i < n_a:
            mp = (m_w_in[i], m_conv_w[i], m_conv_b[i], m_dt_bias[i], m_a_log[i], m_d_skip[i], m_norm_w[i], m_w_out[i])
            Xp, Xs, hp, hs, cp, cs = _mamba_layer(Xp, Xs, dims, state_conv[i], state_ssm[i], mp,
                                                  ln_g[i, 0], ln_b[i, 0], alpha)
            ssm_p.append(hp)
            ssm_s.append(hs)
            conv_p.append(cp)
            conv_s.append(cs)
        else:
            if i == n_a:
                wkv = w_kv_shared.astype(BF16)
                kv_p = _matmul(Xp, wkv).reshape(Bp, Lp, 6 * cw)
                kv_s = _matmul(Xs, wkv)[:Ts].reshape(Bs, Ls, 6 * cw)
                kvb_p = kv_p.astype(BF16)
                n_sel_p = -(-Lp // SEL_BLOCK)
                Lpp = n_sel_p * SEL_BLOCK
                padp = lambda v: jnp.pad(v, ((0, 0), (0, Lpp - Lp), (0, 0)))
                nc_p = -(-(Lpp // cmp_blk) // LANES) * LANES
                kc_p = _compress(padp(kv_p[:, :, 0:cw]), w_cmp[0], pos_cmp[0], nc_p)
                vc_p = _compress(padp(kv_p[:, :, cw:2 * cw]), w_cmp[1], pos_cmp[1], nc_p)
                tk_p = _pick(Lp, (512, 256, 128))
                new_rows = jnp.pad(kv_s[:, :, :4 * cw], ((0, 0), (0, SUBLANES - Ls), (0, 0)))
                kcb_s, vcb_s, ks_s, vs_s = _gather_pages(cache_kv_pages.reshape(NP, PS, n_rows * cw), page_table,
                                                         new_rows, cmp_blk)
                Lk_s = ks_s.shape[1]
                n_sel_s = -(-(past_len + Ls) // SEL_BLOCK)
                nc_s = -(-(Lk_s // cmp_blk) // LANES) * LANES
                kc_s = _compress_blocks(kcb_s, w_cmp[0], pos_cmp[0], nc_s)
                vc_s = _compress_blocks(vcb_s, w_cmp[1], pos_cmp[1], nc_s)
                win_full_s = jnp.concatenate([cache_win, kv_s[:, :, 4 * cw:].reshape(Bs, Ls, 2, kvh, dh)], axis=1)
                Lw = win_cache + Ls
                Lwp = -(-Lw // LANES) * LANES
                wflat = jnp.pad(win_full_s.reshape(Bs, Lw, 2 * cw), ((0, 0), (0, Lwp - Lw), (0, 0))).astype(BF16)
            j = i - n_a
            wq = _head_tiles(nsa_w_in[j][:, :H * dh], H, kvh, dh, axis=1).astype(BF16)
            wg = _pad_cols(nsa_w_in[j][:, H * dh:], LANES).astype(BF16)
            wo = _head_tiles(nsa_w_out[j], H, kvh, dh, axis=0).astype(BF16)
            QW = wq.shape[1]
            o_p = _nsa_attend(_matmul(Xp, wq).reshape(Bp, Lp, QW), _matmul(Xp, wg).reshape(Bp, Lp, LANES),
                              kc_p, vc_p, kvb_p, kvb_p, kvb_p, kvb_p, (0, 0, 2, 3, 4, 5),
                              nq=Q_BLOCK, t0=0, tk=tk_p, n_cmp=Lpp // cmp_blk, cmp_blk=cmp_blk, n_sel=n_sel_p,
                              kvh=kvh, dh=dh, win_rows=min(WINDOW + Q_BLOCK, Lp), w0=0, causal_tiles=True)
            Xp = _matmul_ln(o_p.reshape(Tp, QW), wo, Xp, ln_g[i, 0], ln_b[i, 0], alpha)
            pad_q = lambda v: jnp.pad(v[:Ts].reshape(Bs, Ls, -1), ((0, 0), (0, SUBLANES - Ls), (0, 0)))
            o_s = _nsa_attend(pad_q(_matmul(Xs, wq)), pad_q(_matmul(Xs, wg)), kc_s, vc_s, ks_s, vs_s, wflat, wflat,
                              (0, 0, 0, 0, 0, 1),
                              nq=SUBLANES, t0=past_len, tk=Lk_s, n_cmp=-(-(past_len + Ls) // SEL_BLOCK) * (SEL_BLOCK // cmp_blk),
                              cmp_blk=cmp_blk, n_sel=n_sel_s, kvh=kvh, dh=dh, win_rows=Lwp,
                              w0=past_len - win_cache, causal_tiles=False, seqs=_pick(Bs, (4, 2, 1)))
            o_s = jnp.pad(o_s[:, :Ls].reshape(Ts, QW), ((0, Tsp - Ts), (0, 0)))
            Xs = _matmul_ln(o_s, wo, Xs, ln_g[i, 0], ln_b[i, 0], alpha)
        peer = (peer_w_q[i].astype(BF16), peer_sub_keys[i], peer_u[i].astype(BF16), peer_v[i].astype(BF16),
                ln_g[i, 1], ln_b[i, 1], alpha)
        Xp = _peer_layer(Xp, *peer)
        Xs = _peer_layer(Xs, *peer)

    rows_p = kv_p[:, :, :4 * cw].reshape(Bp, Lp, n_rows, kvh, dh)
    rows_s = kv_s[:, :, :4 * cw].reshape(Bs, Ls, n_rows, kvh, dh)
    win_p = kv_p[:, :, 4 * cw:].reshape(Bp, Lp, 2, kvh, dh)[:, -min(WINDOW, Lp):]
    return (Xp.reshape(Bp, Lp, D), Xs[:Ts].reshape(Bs, Ls, D), rows_p, rows_s, win_p,
            win_full_s[:, -win_cache:], jnp.stack(ssm_p), jnp.stack(ssm_s), jnp.stack(conv_p), jnp.stack(conv_s))
```

```python
import functools
import math

import jax
import jax.numpy as jnp
from jax import lax
from jax.experimental import pallas as pl
from jax.experimental.pallas import tpu as pltpu

F32 = jnp.float32
BF16 = jnp.bfloat16
I32 = jnp.int32

DEPTH_ALPHA_POW = 0.25
LN_EPS = 1e-5
SEL_BLOCK = 64
N_SEL = 16
WINDOW = 512
FORCE_SCORE = 1.0e4
PEER_TOPK = 16
SSD_CHUNK = 128
Q_BLOCK = 128
GELU_C0 = math.sqrt(2.0 / math.pi)
GELU_C1 = 0.044715 * GELU_C0
LOG2E = 1.0 / math.log(2.0)
W_TOKENS_PER_ITER = 16

LANES = 128
SUBLANES = 8
VMEM_LIMIT_BYTES = 56 * 1024 * 1024

_NT = (((1,), (1,)), ((), ()))
_TN = (((0,), (0,)), ((), ()))
_HI = lax.Precision.HIGHEST


def _cparams(*sem):
    return pltpu.CompilerParams(dimension_semantics=sem, vmem_limit_bytes=VMEM_LIMIT_BYTES)


def _pick(n, cands):
    for c in cands:
        if n % c == 0:
            return c
    raise ValueError(f"no tile in {cands} divides {n}")


def _pad_cols(w, n):
    return jnp.pad(w, ((0, 0), (0, n - w.shape[1])))


def _mm_kernel(x_ref, w_ref, o_ref):
    o_ref[...] = jnp.dot(x_ref[...].astype(BF16), w_ref[...],
                         preferred_element_type=F32).astype(o_ref.dtype)


def _mm_bias_kernel(x_ref, xb_ref, w_ref, o_ref):
    x = (x_ref[...] + xb_ref[...]).astype(BF16)
    o_ref[...] = jnp.dot(x, w_ref[...], preferred_element_type=F32).astype(o_ref.dtype)


def _matmul(x, w, *, xbias=None, out_dtype=F32, tm_cap=512):
    M, K = x.shape
    N = w.shape[1]
    tm = _pick(M, tuple(t for t in (512, 256, 128, 64, 32, 16, 8) if t <= tm_cap))
    tn = _pick(N, (1024, 512, 256, 128))
    x_spec = pl.BlockSpec((tm, K), lambda j, i: (i, 0))
    w_spec = pl.BlockSpec((K, tn), lambda j, i: (0, j))
    o_spec = pl.BlockSpec((tm, tn), lambda j, i: (i, j))
    if xbias is None:
        body, specs, args = _mm_kernel, [x_spec, w_spec], (x, w)
    else:
        b_spec = pl.BlockSpec((1, K), lambda j, i: (0, 0))
        body, specs, args = _mm_bias_kernel, [x_spec, b_spec, w_spec], (x, xbias, w)
    return pl.pallas_call(
        body, grid=(N // tn, M // tm), in_specs=specs, out_specs=o_spec,
        out_shape=jax.ShapeDtypeStruct((M, N), out_dtype),
        compiler_params=_cparams("parallel", "parallel"), name="proj")(*args)


def _deepnorm_ln(v, g, b):
    mu = jnp.mean(v, axis=-1, keepdims=True)
    d = v - mu
    var = jnp.mean(d * d, axis=-1, keepdims=True)
    return d * lax.rsqrt(var + LN_EPS) * g + b


def _mm_ln_kernel(x_ref, w_ref, r_ref, g_ref, b_ref, o_ref, *, alpha):
    y = jnp.dot(x_ref[...].astype(BF16), w_ref[...], preferred_element_type=F32)
    o_ref[...] = _deepnorm_ln(alpha * r_ref[...] + y, g_ref[...], b_ref[...])


def _matmul_ln(x, w, resid, g, b, alpha):
    M, K = x.shape
    D = w.shape[1]
    tm = _pick(M, (512, 256, 128, 64, 32, 16, 8))
    row = lambda i: (i, 0)
    fixed = lambda i: (0, 0)
    return pl.pallas_call(
        functools.partial(_mm_ln_kernel, alpha=alpha), grid=(M // tm,),
        in_specs=[pl.BlockSpec((tm, K), row), pl.BlockSpec((K, D), fixed),
                  pl.BlockSpec((tm, D), row), pl.BlockSpec((1, D), fixed), pl.BlockSpec((1, D), fixed)],
        out_specs=pl.BlockSpec((tm, D), row),
        out_shape=jax.ShapeDtypeStruct((M, D), F32),
        compiler_params=_cparams("parallel"), name="proj_ln")(x, w, resid, g.reshape(1, D), b.reshape(1, D))


def _softplus(x):
    return jnp.maximum(x, 0.0) + jnp.log1p(jnp.exp(-jnp.abs(x)))


def _silu(x):
    return x * jax.nn.sigmoid(x)


def _ssd_kernel(z_ref, xbc_ref, dtr_ref, cbuf_ref, h0_ref, cw_ref, cb_ref, dtb_ref, alog_ref,
                dsk_ref, nw_ref, y_ref, hn_ref, xf_scr, h_scr, *, Lc, H, P, N, G, n_valid, conv_w):
    c = pl.program_id(1)
    DI = H * P
    HG = H // G
    GP = HG * P
    tail = xf_scr.shape[0] - Lc

    @pl.when(c == 0)
    def _():
        xf_scr[0:tail, :] = cbuf_ref[0]
        h_scr[...] = h0_ref[0]

    xf_scr[tail:tail + Lc, :] = xbc_ref[0]
    conv = cb_ref[...]
    for k in range(conv_w):
        conv = conv + xf_scr[pl.ds(tail - (conv_w - 1) + k, Lc), :] * cw_ref[k:k + 1, :]
    conv = _silu(conv)
    xf_scr[0:tail, :] = xf_scr[pl.ds(Lc, tail), :]

    xs = conv[:, :DI]
    Bm = conv[:, DI:DI + G * N]
    Cm = conv[:, DI + G * N:]

    lane = lax.broadcasted_iota(I32, (Lc, LANES), 1)
    rowi = lax.broadcasted_iota(I32, (Lc, LANES), 0)
    dt = jnp.where((lane < H) & (rowi < n_valid), _softplus(dtr_ref[0] + dtb_ref[...]), 0.0)
    a = dt * (-jnp.exp(alog_ref[...]))
    ti = lax.broadcasted_iota(I32, (Lc, Lc), 0)
    si = lax.broadcasted_iota(I32, (Lc, Lc), 1)
    causal = ti >= si
    tril = jnp.where(causal, 1.0, 0.0).astype(F32)
    eye = jnp.where(lax.broadcasted_iota(I32, (LANES, LANES), 0) ==
                    lax.broadcasted_iota(I32, (LANES, LANES), 1), 1.0, 0.0).astype(F32)
    acum = jnp.dot(tril, a, precision=_HI, preferred_element_type=F32)
    acum_t = lax.dot_general(eye, acum, _NT, precision=_HI, preferred_element_type=F32)
    dt_t = lax.dot_general(eye, dt, _NT, precision=_HI, preferred_element_type=F32)
    alast = acum[Lc - 1:Lc, :]
    wend = jnp.exp(alast - acum) * dt
    ea = jnp.exp(acum)
    edec = jnp.exp(alast)
    dsk = dsk_ref[...]
    first_half = lax.broadcasted_iota(I32, (Lc, 2 * P), 1) < P

    def pair(v, h):
        rows = v.shape[0]
        return jnp.where(first_half[:rows], jnp.broadcast_to(v[:, h:h + 1], (rows, 2 * P)),
                         jnp.broadcast_to(v[:, h + 1:h + 2], (rows, 2 * P)))

    y_parts = []
    for g in range(G):
        Bg = Bm[:, g * N:(g + 1) * N].astype(BF16)
        Cg = Cm[:, g * N:(g + 1) * N].astype(BF16)
        cb = lax.dot_general(Cg, Bg, _NT, preferred_element_type=F32)
        Sg = h_scr[g * GP:(g + 1) * GP, :]
        yoff = lax.dot_general(Cg, Sg.astype(BF16), _NT, preferred_element_type=F32)
        xw_parts = []
        for pr in range(HG // 2):
            h = g * HG + 2 * pr
            xs_pair = xs[:, h * P:(h + 2) * P]
            xs_pair_b = xs_pair.astype(BF16)
            outs = []
            for hh in (h, h + 1):
                seg = acum[:, hh:hh + 1] - acum_t[hh:hh + 1, :]
                dec = jnp.exp(jnp.where(causal, seg, -jnp.inf))
                mix = cb * dec * dt_t[hh:hh + 1, :]
                outs.append(jnp.dot(mix.astype(BF16), xs_pair_b, preferred_element_type=F32))
            ydiag = jnp.where(first_half, outs[0], outs[1])
            y_pair = ydiag + yoff[:, pr * 2 * P:(pr + 1) * 2 * P] * pair(ea, h) + pair(dsk, h) * xs_pair
            y_parts.append(y_pair)
            xw_parts.append(xs_pair * pair(wend, h))
        xw = jnp.concatenate(xw_parts, axis=1).astype(BF16)
        states = lax.dot_general(xw, Bg, _TN, preferred_element_type=F32)
        for hl in range(HG):
            hh = g * HG + hl
            r0 = g * GP + hl * P
            h_scr[r0:r0 + P, :] = edec[:, hh:hh + 1] * Sg[hl * P:(hl + 1) * P, :] + states[hl * P:(hl + 1) * P, :]

    z = z_ref[0]
    gn = DI // G
    outs = []
    for g in range(G):
        parts = y_parts[g * (HG // 2):(g + 1) * (HG // 2)]
        yg = jnp.concatenate(parts, axis=1) if len(parts) > 1 else parts[0]
        yg = yg * _silu(z[:, g * gn:(g + 1) * gn])
        ms = jnp.mean(yg * yg, axis=-1, keepdims=True)
        outs.append(yg * lax.rsqrt(ms + 1e-5))
    y_ref[0] = jnp.concatenate(outs, axis=1) * nw_ref[...]

    @pl.when(c == pl.num_programs(1) - 1)
    def _():
        hn_ref[0] = h_scr[...]


def _ssd(z, xbc, dtr, cbuf, h0, conv_w, conv_b, dt_bias, a_log, d_skip, norm_w, *, Lc, n_valid, G):
    B, L, DI = z.shape
    CD = xbc.shape[-1]
    _, H, P, N = h0.shape
    KW = conv_w.shape[0]
    assert P * 2 == LANES and (H // G) % 2 == 0 and H <= LANES and L % Lc == 0
    padl = lambda v: jnp.pad(v.astype(F32).reshape(1, -1), ((0, 0), (0, LANES - H)))
    seq = lambda b, c: (b, c, 0)
    per_b = lambda b, c: (b, 0, 0)
    fixed = lambda b, c: (0, 0)
    kern = functools.partial(_ssd_kernel, Lc=Lc, H=H, P=P, N=N, G=G, n_valid=n_valid, conv_w=KW)
    y, hn = pl.pallas_call(
        kern, grid=(B, L // Lc),
        in_specs=[pl.BlockSpec((1, Lc, DI), seq), pl.BlockSpec((1, Lc, CD), seq),
                  pl.BlockSpec((1, Lc, LANES), seq), pl.BlockSpec((1, SUBLANES, CD), per_b),
                  pl.BlockSpec((1, H * P, N), per_b),
                  pl.BlockSpec((KW, CD), fixed), pl.BlockSpec((1, CD), fixed),
                  pl.BlockSpec((1, LANES), fixed), pl.BlockSpec((1, LANES), fixed),
                  pl.BlockSpec((1, LANES), fixed), pl.BlockSpec((1, DI), fixed)],
        out_specs=[pl.BlockSpec((1, Lc, DI), seq), pl.BlockSpec((1, H * P, N), per_b)],
        out_shape=[jax.ShapeDtypeStruct((B, L, DI), F32), jax.ShapeDtypeStruct((B, H * P, N), F32)],
        scratch_shapes=[pltpu.VMEM((SUBLANES + Lc, CD), F32), pltpu.VMEM((H * P, N), F32)],
        compiler_params=_cparams("parallel", "arbitrary"), name="ssd",
    )(z, xbc, dtr, cbuf, h0.reshape(B, H * P, N), conv_w, conv_b.reshape(1, CD),
      padl(dt_bias), padl(a_log), padl(d_skip), norm_w.reshape(1, DI))
    return y, hn.reshape(B, H, P, N)


def _mamba_layer(Xp, Xs, dims, state_conv_i, state_ssm_i, params, ln_g, ln_b, alpha):
    Bp, Lp, Bs, Ls = dims
    w_in, conv_w, conv_b, dt_bias, a_log, d_skip, norm_w, w_out = params
    Tp = Bp * Lp
    DI = norm_w.shape[0]
    CD = conv_w.shape[1]
    _, H, P, N = state_ssm_i.shape
    G = (CD - DI) // (2 * N)
    KW = conv_w.shape[0]
    Ts = Bs * Ls
    wz = w_in[:, :DI].astype(BF16)
    wx = w_in[:, DI:DI + CD].astype(BF16)
    wd = _pad_cols(w_in[:, DI + CD:], LANES).astype(BF16)
    wo = w_out.astype(BF16)
    args = (conv_w, conv_b, dt_bias, a_log, d_skip, norm_w)

    xbc_p = _matmul(Xp, wx).reshape(Bp, Lp, CD)
    Lc = SSD_CHUNK if Lp % SSD_CHUNK == 0 else Lp
    yp, hp = _ssd(_matmul(Xp, wz).reshape(Bp, Lp, DI), xbc_p, _matmul(Xp, wd).reshape(Bp, Lp, LANES),
                  jnp.zeros((Bp, SUBLANES, CD), F32), jnp.zeros((Bp, H, P, N), F32), *args,
                  Lc=Lc, n_valid=Lc, G=G)
    conv_p = xbc_p[:, Lp - (KW - 1):]
    Xp = _matmul_ln(yp.reshape(Tp, DI), wo, Xp, ln_g, ln_b, alpha)

    Lsp = -(-Ls // SUBLANES) * SUBLANES
    pad_rows = lambda v: jnp.pad(v[:Ts].reshape(Bs, Ls, -1), ((0, 0), (0, Lsp - Ls), (0, 0)))
    xbc_s = _matmul(Xs, wx)[:Ts].reshape(Bs, Ls, CD)
    cbuf_s = jnp.pad(state_conv_i, ((0, 0), (SUBLANES - (KW - 1), 0), (0, 0)))
    ys, hs = _ssd(pad_rows(_matmul(Xs, wz)), pad_rows(xbc_s), pad_rows(_matmul(Xs, wd)), cbuf_s, state_ssm_i,
                  *args, Lc=Lsp, n_valid=Ls, G=G)
    conv_s = jnp.concatenate([state_conv_i, xbc_s], axis=1)[:, -(KW - 1):]
    ys = jnp.pad(ys[:, :Ls].reshape(Ts, DI), ((0, Xs.shape[0] - Ts), (0, 0)))
    Xs = _matmul_ln(ys, wo, Xs, ln_g, ln_b, alpha)
    return Xp, Xs, hp, hs, conv_p, conv_s


def _topk_rows(s, k):
    n = s.shape[0]
    rows = lax.broadcasted_iota(I32, s.shape, 0)
    vals, idxs = [], []
    for _ in range(k):
        m = jnp.max(s, axis=0, keepdims=True)
        i = jnp.min(jnp.where(s == m, rows, n), axis=0, keepdims=True)
        vals.append(m)
        idxs.append(i)
        s = jnp.where(rows == i, -jnp.inf, s)
    return vals, idxs


def _peer_topk_kernel(q_ref, k_ref, i1_ref, i2_ref, g_ref, *, heads, half, topk, n_keys):
    k1 = k_ref[0]
    k2 = k_ref[1]
    tb = q_ref.shape[0]
    cand_ab = [(a, b) for a in range(topk) for b in range(topk // (a + 1))]
    n_pad = -len(cand_ab) % SUBLANES
    codes, gates = [], []
    for h in range(heads):
        q1 = q_ref[:, (2 * h) * half:(2 * h + 1) * half]
        q2 = q_ref[:, (2 * h + 1) * half:(2 * h + 2) * half]
        s1 = lax.dot_general(k1, q1, _NT, preferred_element_type=F32)
        s2 = lax.dot_general(k2, q2, _NT, preferred_element_type=F32)
        v1, i1 = _topk_rows(s1, topk)
        v2, i2 = _topk_rows(s2, topk)
        cand = jnp.concatenate([v1[a] + v2[b] for a, b in cand_ab] +
                               [jnp.full((n_pad, tb), -jnp.inf, F32)], axis=0)
        code = jnp.concatenate([i1[a] * n_keys + i2[b] for a, b in cand_ab] +
                               [jnp.zeros((n_pad, tb), I32)], axis=0)
        sc, pos = _topk_rows(cand, topk)
        crow = lax.broadcasted_iota(I32, cand.shape, 0)
        codes += [jnp.max(jnp.where(crow == p, code, -1), axis=0, keepdims=True) for p in pos]
        ex = jnp.exp(jnp.concatenate(sc, axis=0) - sc[0])
        gates.append(ex / jnp.sum(ex, axis=0, keepdims=True))
    code_t = jnp.concatenate(codes, axis=0).T
    i1_ref[...] = code_t // n_keys
    i2_ref[...] = code_t % n_keys
    g_ref[...] = jnp.concatenate(gates, axis=0).T


def _peer_w_kernel(i1_ref, i2_ref, g_ref, w_ref, *, n_keys, nj):
    sub = lax.broadcasted_iota(I32, (n_keys, i1_ref.shape[1]), 0)

    def body(tb, carry):
        for u in range(W_TOKENS_PER_ITER):
            t = tb * W_TOKENS_PER_ITER + u
            i1 = i1_ref[pl.ds(t, 1), :]
            i2 = i2_ref[pl.ds(t, 1), :]
            g = g_ref[pl.ds(t, 1), :]
            onehot1 = jnp.where(sub == i1, 1.0, 0.0).astype(BF16)
            gated2 = jnp.where(sub == i2, g, 0.0).astype(BF16)
            w_t = lax.dot_general(onehot1, gated2, _NT, preferred_element_type=F32)
            r0 = pl.multiple_of(t * nj, nj)
            for k in range(n_keys // nj):
                w_ref[0, k, pl.ds(r0, nj), :] = w_t[k * nj:(k + 1) * nj, :]
        return carry

    lax.fori_loop(0, i1_ref.shape[0] // W_TOKENS_PER_ITER, body, 0)


def _gelu_tanh(x):
    hx = 0.5 * x
    inner = x * (GELU_C0 + GELU_C1 * (x * x))
    return hx * jnp.tanh(inner) + hx


def _peer_dense_kernel(x_ref, u_ref, v_ref, w_ref, g_ref, b_ref, o_ref, acc_ref, xb_ref, *, alpha, nj):
    k = pl.program_id(1)
    td = x_ref.shape[0]

    @pl.when(k == 0)
    def _():
        acc_ref[...] = jnp.zeros_like(acc_ref)
        xb_ref[...] = x_ref[...].astype(BF16)

    act = lax.dot_general(xb_ref[...], u_ref[...], _NT, preferred_element_type=F32)
    w = jnp.concatenate([w_ref[0, 0, pl.ds(j, td, stride=nj), :] for j in range(nj)], axis=1)
    coef = (_gelu_tanh(act) * w).astype(BF16)
    acc_ref[...] += jnp.dot(coef, v_ref[...], preferred_element_type=F32)

    @pl.when(k == pl.num_programs(1) - 1)
    def _():
        o_ref[...] = _deepnorm_ln(alpha * x_ref[...] + acc_ref[...], g_ref[...], b_ref[...])


def _peer_layer(X, w_q, sub_keys, u_tab, v_tab, ln_g, ln_b, alpha):
    T, D = X.shape
    _, n_keys, half = sub_keys.shape
    heads = w_q.shape[1] // (2 * half)
    nsel = heads * PEER_TOPK
    assert n_keys == LANES and nsel == LANES and half % LANES == 0
    q = _matmul(X, w_q)

    tb = LANES
    tok = lambda i: (i, 0)
    sel_shape = jax.ShapeDtypeStruct((T, nsel), I32)
    i1, i2, gate = pl.pallas_call(
        functools.partial(_peer_topk_kernel, heads=heads, half=half, topk=PEER_TOPK, n_keys=n_keys),
        grid=(T // tb,),
        in_specs=[pl.BlockSpec((tb, q.shape[1]), tok), pl.BlockSpec((2, n_keys, half), lambda i: (0, 0, 0))],
        out_specs=[pl.BlockSpec((tb, nsel), tok)] * 3,
        out_shape=[sel_shape, sel_shape, jax.ShapeDtypeStruct((T, nsel), F32)],
        compiler_params=_cparams("parallel"), name="peer_topk")(q, sub_keys)

    td = _pick(T, (1024, 768, 512, 256, 128))
    nj = SUBLANES
    et = nj * n_keys
    sub_blocks = td // tb
    w = pl.pallas_call(
        functools.partial(_peer_w_kernel, n_keys=n_keys, nj=nj), grid=(T // tb,),
        in_specs=[pl.BlockSpec((tb, nsel), tok)] * 3,
        out_specs=pl.BlockSpec((1, n_keys // nj, tb * nj, n_keys),
                               lambda i: (i // sub_blocks, 0, i % sub_blocks, 0)),
        out_shape=jax.ShapeDtypeStruct((T // td, n_keys // nj, td * nj, n_keys), F32),
        compiler_params=_cparams("parallel"), name="peer_w")(i1, i2, gate)

    return pl.pallas_call(
        functools.partial(_peer_dense_kernel, alpha=alpha, nj=nj),
        grid=(T // td, n_keys // nj),
        in_specs=[pl.BlockSpec((td, D), lambda i, k: (i, 0)),
                  pl.BlockSpec((et, D), lambda i, k: (k, 0)),
                  pl.BlockSpec((et, D), lambda i, k: (k, 0)),
                  pl.BlockSpec((1, 1, td * nj, n_keys), lambda i, k: (i, k, 0, 0)),
                  pl.BlockSpec((1, D), lambda i, k: (0, 0)), pl.BlockSpec((1, D), lambda i, k: (0, 0))],
        out_specs=pl.BlockSpec((td, D), lambda i, k: (i, 0)),
        out_shape=jax.ShapeDtypeStruct((T, D), F32),
        scratch_shapes=[pltpu.VMEM((td, D), F32), pltpu.VMEM((td, D), BF16)],
        compiler_params=_cparams("parallel", "arbitrary"), name="peer_dense",
    )(X, u_tab, v_tab, w, ln_g.reshape(1, D), ln_b.reshape(1, D))


def _gather_kernel(pt_ref, pa_ref, pb_ref, new_ref, kc_ref, vc_ref, ks_ref, vs_ref, col_scr, *, n_steps, cw, blk):
    del pt_ref
    p = pl.program_id(1)
    ps = pa_ref.shape[1]
    per_page = ps // blk

    @pl.when(p < n_steps)
    def _():
        for j, page in enumerate((pa_ref, pb_ref)):
            n_tiles = 2 * cw // LANES
            for c in range(n_tiles):
                col_scr[j * n_tiles + c] = page[0, :, c * LANES:(c + 1) * LANES]
            for r, ref in enumerate((kc_ref, vc_ref)):
                for l in range(blk):
                    for c in range(cw // LANES):
                        ref[0, j * per_page:(j + 1) * per_page, l * cw + c * LANES:l * cw + (c + 1) * LANES] = \
                            col_scr[j * n_tiles + r * (cw // LANES) + c, pl.ds(l, per_page, stride=blk), :]
            ks_ref[0, j * ps:(j + 1) * ps, :] = page[0, :, 2 * cw:3 * cw].astype(BF16)
            vs_ref[0, j * ps:(j + 1) * ps, :] = page[0, :, 3 * cw:4 * cw].astype(BF16)

    @pl.when(p == n_steps)
    def _():
        new = new_ref[0]
        n_new = new.shape[0]
        for r, ref in enumerate((kc_ref, vc_ref)):
            ref[0] = jnp.zeros(ref.shape[1:], F32)
            for l in range(n_new):
                ref[0, 0:1, l * cw:(l + 1) * cw] = new[l:l + 1, r * cw:(r + 1) * cw]
        for r, ref in ((2, ks_ref), (3, vs_ref)):
            rows = jnp.concatenate([new[:, r * cw:(r + 1) * cw], jnp.zeros((2 * ps - n_new, cw), F32)], axis=0)
            ref[0] = rows.astype(BF16)


def _gather_pages(pages, page_table, new_rows, blk):
    NP, PS, C = pages.shape
    B, n_pages = page_table.shape
    assert n_pages % 2 == 0 and PS % blk == 0 and (2 * PS // blk) % SUBLANES == 0 and new_rows.shape[1] <= blk
    cw = C // 4
    n_steps = n_pages // 2
    L = (n_pages + 2) * PS
    nb = 2 * PS // blk
    page_spec = lambda j: pl.BlockSpec(
        (1, PS, C), lambda b, p, pt: (pt[b, jnp.minimum(2 * p + j, n_pages - 2 + j)], 0, 0))
    cmp_spec = pl.BlockSpec((1, nb, blk * cw), lambda b, p, pt: (b, p, 0))
    sel_spec = pl.BlockSpec((1, 2 * PS, cw), lambda b, p, pt: (b, p, 0))
    return pl.pallas_call(
        functools.partial(_gather_kernel, n_steps=n_steps, cw=cw, blk=blk),
        grid_spec=pltpu.PrefetchScalarGridSpec(
            num_scalar_prefetch=1, grid=(B, n_steps + 1),
            in_specs=[page_spec(0), page_spec(1),
                      pl.BlockSpec((1, new_rows.shape[1], C), lambda b, p, pt: (b, 0, 0))],
            out_specs=[cmp_spec, cmp_spec, sel_spec, sel_spec],
            scratch_shapes=[pltpu.VMEM((2 * (2 * cw // LANES), PS, LANES), F32)]),
        out_shape=[jax.ShapeDtypeStruct((B, L // blk, blk * cw), F32)] * 2 +
                  [jax.ShapeDtypeStruct((B, L, cw), BF16)] * 2,
        compiler_params=_cparams("parallel", "arbitrary"), name="kv_gather")(page_table, pages, pages, new_rows)


def _compress_blocks(xb, w_c, pos_c, nc_pad):
    B, n, K = xb.shape
    blk, dh, _ = w_c.shape
    C = K // blk
    G = C // dh
    w_bd = jnp.einsum('lde,gh->lgdhe', w_c, jnp.eye(G, dtype=w_c.dtype)).reshape(K, C)
    xbias = jnp.broadcast_to(pos_c[:, None, :], (blk, G, dh)).reshape(1, K)
    out = _matmul(xb.reshape(B * n, K), w_bd.astype(BF16), xbias=xbias, tm_cap=128)
    return jnp.pad(out.reshape(B, n, C), ((0, 0), (0, nc_pad - n), (0, 0)))


def _compress(x, w_c, pos_c, nc_pad):
    B, L, C = x.shape
    blk = w_c.shape[0]
    return _compress_blocks(x.reshape(B, L // blk, blk * C), w_c, pos_c, nc_pad)


def _lane_tiles(x):
    return [x[:, j * LANES:(j + 1) * LANES] for j in range(x.shape[1] // LANES)]


def _row_max(x):
    return jnp.max(functools.reduce(jnp.maximum, _lane_tiles(x)), axis=-1, keepdims=True)


def _row_sum(x):
    return jnp.sum(functools.reduce(jnp.add, _lane_tiles(x)), axis=-1, keepdims=True)


def _split3_dot(x, m):
    x1 = x.astype(BF16)
    r1 = x - x1.astype(F32)
    x2 = r1.astype(BF16)
    x3 = (r1 - x2.astype(F32)).astype(BF16)
    d = lambda a: jnp.dot(a, m, preferred_element_type=F32)
    return d(x1) + d(x2) + d(x3)


def _nsa_staged_kernel(q_ref, gt_ref, kc_ref, vc_ref, ks_ref, vs_ref, kw_ref, vw_ref, ex_ref, o_ref, *,
                       seqs, nq, t0, tk, n_cmp, cmp_blk, n_sel, kvh, hg, dh, slopes, win_rows, w0):
    q0 = t0 + pl.program_id(1) * nq
    R = hg * nq
    NC = kc_ref.shape[1]
    Lk = ks_ref.shape[1]
    NSP = ex_ref.shape[0]
    assert tk == Lk
    qscale = dh ** -0.5 * LOG2E
    slopes2 = tuple(s * LOG2E for s in slopes)
    units = [(s, g) for s in range(seqs) for g in range(kvh)]
    col = lambda g: (g // 2) * 2 * dh

    def tq_like(shape):
        return q0 + lax.broadcasted_iota(I32, shape, 0)

    def attend(keys, vals, distms):
        s_alls = [lax.dot_general(qgs[u], keys[u], _NT, preferred_element_type=F32) for u in range(len(units))]
        s2s = [[s_alls[u][h * nq:(h + 1) * nq] - slopes2[g * hg + h] * distms[u] for h in range(hg)]
               for u, (_, g) in enumerate(units)]
        ms = [[_row_max(x) for x in row] for row in s2s]
        es = [[jnp.exp2(x - jnp.where(m == -jnp.inf, 0.0, m)) for x, m in zip(xr, mr)] for xr, mr in zip(s2s, ms)]
        ls = [[_row_sum(e) for e in row] for row in es]
        ps = [[e / jnp.maximum(l, 1e-30) for e, l in zip(er, lr)] for er, lr in zip(es, ls)]
        os = [jnp.dot(jnp.concatenate(ps[u], axis=0).astype(BF16), vals[u], preferred_element_type=F32)
              for u in range(len(units))]
        return ps, os

    n_idx = lax.broadcasted_iota(I32, (nq, NC), 1)
    dist_c = tq_like((nq, NC)) - (n_idx * cmp_blk + (cmp_blk - 1))
    distm_c = jnp.where((dist_c >= 0) & (n_idx < n_cmp), dist_c.astype(F32), jnp.inf)
    dist_w = tq_like((nq, win_rows)) - (w0 + lax.broadcasted_iota(I32, (nq, win_rows), 1))
    distm_w = jnp.where((dist_w >= 0) & (dist_w < WINDOW), dist_w.astype(F32), jnp.inf)
    dist_s = tq_like((nq, Lk)) - lax.broadcasted_iota(I32, (nq, Lk), 1)
    per_sel = SEL_BLOCK // cmp_blk
    pair = jnp.where(lax.broadcasted_iota(I32, (NC, NSP), 0) // per_sel ==
                     lax.broadcasted_iota(I32, (NC, NSP), 1), 1.0, 0.0).astype(BF16)
    blk = lax.broadcasted_iota(I32, (nq, NSP), 1)
    tqs = tq_like((nq, NSP))
    cur = tqs // SEL_BLOCK
    forced = (blk == 0) | (blk == cur) | (blk == cur - 1)
    sel_valid = blk * SEL_BLOCK <= tqs

    qgs = [(jnp.concatenate([q_ref[s, :, (g * hg + h) * 2 * dh:(g * hg + h + 1) * 2 * dh] for h in range(hg)],
                            axis=0) * qscale).astype(BF16) for s, g in units]
    tile = lambda ref, s, g: ref[s, :, col(g):col(g) + 2 * dh]
    p_cmp, o_cmp = attend([tile(kc_ref, s, g).astype(BF16) for s, g in units],
                          [tile(vc_ref, s, g).astype(BF16) for s, g in units], [distm_c] * len(units))
    _, o_win = attend([tile(kw_ref, s, g) for s, g in units], [tile(vw_ref, s, g) for s, g in units],
                      [distm_w] * len(units))

    psums = [functools.reduce(jnp.add, row) for row in p_cmp]
    imps = [_split3_dot(x, pair) for x in psums]
    scores = [jnp.where(blk < n_sel, jnp.where(forced, FORCE_SCORE, jnp.where(sel_valid, imp, -1.0)), -2.0)
              for imp in imps]
    selm = _top_blocks(jnp.concatenate(scores, axis=0), min(N_SEL, n_sel), NSP)
    selb = selm.astype(BF16)
    expand = ex_ref[...]
    selxs = [jnp.dot(selb[u * nq:(u + 1) * nq], expand, preferred_element_type=F32) for u in range(len(units))]
    distm_s = [jnp.where((dist_s >= 0) & (x > 0.5), dist_s.astype(F32), jnp.inf) for x in selxs]
    _, o_sel = attend([tile(ks_ref, s, g) for s, g in units], [tile(vs_ref, s, g) for s, g in units], distm_s)

    lane = lax.broadcasted_iota(I32, (nq, 2 * dh), 1)
    for s in range(seqs):
        gates = jax.nn.sigmoid(gt_ref[s])
        for g in range(kvh):
            u = s * kvh + g
            lo = (g % 2) * dh
            keep = (lane >= lo) & (lane < lo + dh)
            for h in range(hg):
                hd = g * hg + h
                rows = slice(h * nq, (h + 1) * nq)
                o = (gates[:, 3 * hd:3 * hd + 1] * o_cmp[u][rows] + gates[:, 3 * hd + 1:3 * hd + 2] * o_sel[u][rows] +
                     gates[:, 3 * hd + 2:3 * hd + 3] * o_win[u][rows])
                o_ref[s, :, hd * 2 * dh:(hd + 1) * 2 * dh] = jnp.where(keep, o, 0.0)


def _top_blocks(score, n_top, nsp):
    if score.shape[0] % LANES == 0 and nsp == LANES:
        st = score.T
        brow = lax.broadcasted_iota(I32, st.shape, 0)
        sel_t = jnp.zeros(st.shape, F32)
        for _ in range(n_top):
            m = jnp.max(st, axis=0, keepdims=True)
            idx = jnp.min(jnp.where(st == m, brow, nsp), axis=0, keepdims=True)
            hit = brow == idx
            sel_t = jnp.where(hit, 1.0, sel_t)
            st = jnp.where(hit, -jnp.inf, st)
        return sel_t.T
    bcol = lax.broadcasted_iota(I32, score.shape, 1)
    selm = jnp.zeros(score.shape, F32)
    for _ in range(n_top):
        m = jnp.max(score, axis=-1, keepdims=True)
        idx = jnp.min(jnp.where(score == m, bcol, nsp), axis=-1, keepdims=True)
        hit = bcol == idx
        selm = jnp.where(hit, 1.0, selm)
        score = jnp.where(hit, -jnp.inf, score)
    return selm


def _nsa_multi_kernel(q_ref, gt_ref, kc_ref, vc_ref, ks_ref, vs_ref, kw_ref, vw_ref, ex_ref, o_ref, *scratch,
                      seqs, **cfg):
    for s in range(seqs):
        one = lambda r: r.at[pl.ds(s, 1)]
        _nsa_kernel(one(q_ref), one(gt_ref), one(kc_ref), one(vc_ref), one(ks_ref), one(vs_ref), one(kw_ref),
                    one(vw_ref), ex_ref, one(o_ref), *scratch, **cfg)


def _nsa_kernel(q_ref, gt_ref, kc_ref, vc_ref, ks_ref, vs_ref, kw_ref, vw_ref, ex_ref, o_ref,
                m_scr, l_scr, acc_scr, flag_ref, *,
                nq, t0, tk, n_cmp, cmp_blk, n_sel, kvh, hg, dh, slopes, win_rows, w0, causal_tiles):
    i = pl.program_id(1)
    q0 = t0 + i * nq
    R = hg * nq
    NC = kc_ref.shape[1]
    Lk = ks_ref.shape[1]
    NSP = ex_ref.shape[0]
    qscale = dh ** -0.5 * LOG2E
    slopes2 = tuple(s * LOG2E for s in slopes)
    gates = jax.nn.sigmoid(gt_ref[0])

    def tq_like(shape):
        return q0 + lax.broadcasted_iota(I32, shape, 0)

    if causal_tiles:
        wstart = pl.multiple_of(jnp.maximum(q0 - WINDOW, 0), nq)
    else:
        wstart = 0

    def col(g):
        return (g // 2) * 2 * dh

    groups = range(kvh)

    def attend(keys, vals, distm):
        s_alls = [lax.dot_general(qgs[g], keys[g], _NT, preferred_element_type=F32) for g in groups]
        s2s = [[s_alls[g][h * nq:(h + 1) * nq] - slopes2[g * hg + h] * distm for h in range(hg)] for g in groups]
        ms = [[_row_max(x) for x in row] for row in s2s]
        es = [[jnp.exp2(x - jnp.where(m == -jnp.inf, 0.0, m)) for x, m in zip(xr, mr)] for xr, mr in zip(s2s, ms)]
        ls = [[_row_sum(e) for e in row] for row in es]
        ps = [[e / jnp.maximum(l, 1e-30) for e, l in zip(er, lr)] for er, lr in zip(es, ls)]
        os = [jnp.dot(jnp.concatenate(ps[g], axis=0).astype(BF16), vals[g], preferred_element_type=F32)
              for g in groups]
        return ps, os

    n_idx = lax.broadcasted_iota(I32, (nq, NC), 1)
    dist_c = tq_like((nq, NC)) - (n_idx * cmp_blk + (cmp_blk - 1))
    distm_c = jnp.where((dist_c >= 0) & (n_idx < n_cmp), dist_c.astype(F32), jnp.inf)
    dist_w = tq_like((nq, win_rows)) - (w0 + wstart + lax.broadcasted_iota(I32, (nq, win_rows), 1))
    distm_w = jnp.where((dist_w >= 0) & (dist_w < WINDOW), dist_w.astype(F32), jnp.inf)
    per_sel = SEL_BLOCK // cmp_blk
    pair = jnp.where(lax.broadcasted_iota(I32, (NC, NSP), 0) // per_sel ==
                     lax.broadcasted_iota(I32, (NC, NSP), 1), 1.0, 0.0).astype(BF16)
    blk = lax.broadcasted_iota(I32, (nq, NSP), 1)
    tqs = tq_like((nq, NSP))
    cur = tqs // SEL_BLOCK
    forced = (blk == 0) | (blk == cur) | (blk == cur - 1)
    sel_valid = blk * SEL_BLOCK <= tqs

    qgs = [(jnp.concatenate([q_ref[0, :, (g * hg + h) * 2 * dh:(g * hg + h + 1) * 2 * dh] for h in range(hg)],
                            axis=0) * qscale).astype(BF16) for g in groups]
    p_cmps, o_cmps = attend([kc_ref[0, :, col(g):col(g) + 2 * dh].astype(BF16) for g in groups],
                            [vc_ref[0, :, col(g):col(g) + 2 * dh].astype(BF16) for g in groups], distm_c)
    _, o_wins = attend([kw_ref[0, pl.ds(wstart, win_rows), col(g):col(g) + 2 * dh] for g in groups],
                       [vw_ref[0, pl.ds(wstart, win_rows), col(g):col(g) + 2 * dh] for g in groups], distm_w)
    imps = [_split3_dot(functools.reduce(jnp.add, p_cmps[g]), pair) for g in groups]
    scores = [jnp.where(blk < n_sel, jnp.where(forced, FORCE_SCORE, jnp.where(sel_valid, imp, -1.0)), -2.0)
              for imp in imps]

    selm = _top_blocks(jnp.concatenate(scores, axis=0), min(N_SEL, n_sel), NSP)
    selb = selm.astype(BF16)

    def sel_tile(kt, carry):
        k0 = kt * tk
        dist = tq_like((nq, tk)) - (k0 + lax.broadcasted_iota(I32, (nq, tk), 1))
        causal = dist >= 0
        distf = dist.astype(F32)
        expand = ex_ref[:, pl.ds(k0, tk)]
        new = []
        for g in range(kvh):
            m_run, l_run, acc = carry[g]
            c0 = col(g)
            s_t = lax.dot_general(qgs[g], ks_ref[0, pl.ds(k0, tk), c0:c0 + 2 * dh], _NT,
                                  preferred_element_type=F32)
            selx = jnp.dot(selb[g * nq:(g + 1) * nq], expand, preferred_element_type=F32)
            distm = jnp.where(causal & (selx > 0.5), distf, jnp.inf)
            m_new, l_new, alphas, ps = [], [], [], []
            for h in range(hg):
                rows = slice(h * nq, (h + 1) * nq)
                s2 = s_t[rows] - slopes2[g * hg + h] * distm
                m_h = jnp.maximum(m_run[rows], _row_max(s2))
                m_safe = jnp.where(m_h == -jnp.inf, 0.0, m_h)
                a_h = jnp.exp2(m_run[rows] - m_safe)
                p_h = jnp.exp2(s2 - m_safe)
                m_new.append(m_h)
                l_new.append(a_h * l_run[rows] + _row_sum(p_h))
                alphas.append(a_h)
                ps.append(p_h)
            pv = jnp.dot(jnp.concatenate(ps, axis=0).astype(BF16), vs_ref[0, pl.ds(k0, tk), c0:c0 + 2 * dh],
                         preferred_element_type=F32)
            new.append((jnp.concatenate(m_new, axis=0), jnp.concatenate(l_new, axis=0),
                        jnp.concatenate(alphas, axis=0) * acc + pv))
        return tuple(new)

    def sel_tile_group(kt, g):
        k0 = pl.multiple_of(kt * tk, tk)
        c0 = col(g)
        dist = tq_like((nq, tk)) - (k0 + lax.broadcasted_iota(I32, (nq, tk), 1))
        selx = jnp.dot(selb[g * nq:(g + 1) * nq], ex_ref[:, pl.ds(k0, tk)], preferred_element_type=F32)
        distm = jnp.where((dist >= 0) & (selx > 0.5), dist.astype(F32), jnp.inf)
        s_t = lax.dot_general(qgs[g], ks_ref[0, pl.ds(k0, tk), c0:c0 + 2 * dh], _NT,
                              preferred_element_type=F32)
        m_run = m_scr[g]
        l_run = l_scr[g]
        heads = range(hg)
        rows = [slice(h * nq, (h + 1) * nq) for h in heads]
        s2s = [s_t[rows[h]] - slopes2[g * hg + h] * distm for h in heads]
        tmax = [_row_max(x) for x in s2s]
        m_new = [jnp.maximum(m_run[rows[h]], tmax[h]) for h in heads]
        m_safe = [jnp.where(m == -jnp.inf, 0.0, m) for m in m_new]
        alphas = [jnp.exp2(m_run[rows[h]] - m_safe[h]) for h in heads]
        ps = [jnp.exp2(s2s[h] - jnp.concatenate([m_safe[h]] * (tk // LANES), axis=1)) for h in heads]
        l_new = [alphas[h] * l_run[rows[h]] + _row_sum(ps[h]) for h in heads]
        pv = jnp.dot(jnp.concatenate(ps, axis=0).astype(BF16), vs_ref[0, pl.ds(k0, tk), c0:c0 + 2 * dh],
                     preferred_element_type=F32)
        m_scr[g] = jnp.concatenate(m_new, axis=0)
        l_scr[g] = jnp.concatenate(l_new, axis=0)
        acc_scr[g] = jnp.concatenate(alphas, axis=0) * acc_scr[g] + pv

    if causal_tiles:
        n_flag = Lk // tk
        blocks_per_tile = tk // SEL_BLOCK
        lane_b = lax.broadcasted_iota(I32, (1, NSP), 1)
        for g in range(kvh):
            any_q = jnp.max(selm[g * nq:(g + 1) * nq], axis=0, keepdims=True)
            for kt in range(n_flag):
                in_tile = (lane_b >= kt * blocks_per_tile) & (lane_b < (kt + 1) * blocks_per_tile)
                flag_ref[g * n_flag + kt] = (jnp.max(jnp.where(in_tile, any_q, 0.0)) > 0.5).astype(I32)
        m_scr[...] = jnp.full(m_scr.shape, -jnp.inf, F32)
        l_scr[...] = jnp.zeros(l_scr.shape, F32)
        acc_scr[...] = jnp.zeros(acc_scr.shape, F32)

        def visit(kt, carry):
            for g in range(kvh):
                @pl.when(flag_ref[g * n_flag + kt] > 0)
                def _():
                    sel_tile_group(kt, g)
            return carry

        lax.fori_loop(0, (q0 + nq + tk - 1) // tk, visit, 0)
        fin = tuple((None, l_scr[g], acc_scr[g]) for g in range(kvh))
    else:
        fin = tuple((jnp.full((R, 1), -jnp.inf, F32), jnp.zeros((R, 1), F32), jnp.zeros((R, 2 * dh), F32))
                    for _ in range(kvh))
        for kt in range(Lk // tk):
            fin = sel_tile(kt, fin)

    lane = lax.broadcasted_iota(I32, (nq, 2 * dh), 1)
    for g in range(kvh):
        lo = (g % 2) * dh
        keep = (lane >= lo) & (lane < lo + dh)
        _, l_fin, acc = fin[g]
        o_sel = acc / jnp.maximum(l_fin, 1e-30)
        for h in range(hg):
            hd = g * hg + h
            rows = slice(h * nq, (h + 1) * nq)
            o = (gates[:, 3 * hd:3 * hd + 1] * o_cmps[g][rows] + gates[:, 3 * hd + 1:3 * hd + 2] * o_sel[rows] +
                 gates[:, 3 * hd + 2:3 * hd + 3] * o_wins[g][rows])
            o_ref[0, :, hd * 2 * dh:(hd + 1) * 2 * dh] = jnp.where(keep, o, 0.0)


def _nsa_attend(q, gt, kc, vc, ks, vs, kw, vw, cols, *, nq, t0, tk, n_cmp, cmp_blk, n_sel, kvh, dh,
                win_rows, w0, causal_tiles, seqs=1):
    B, Lq, QW = q.shape
    H = QW // (2 * dh)
    hg = H // kvh
    cw = kvh * dh
    Lk = ks.shape[1]
    NC = kc.shape[1]
    nsp = LANES
    assert n_sel <= nsp and Lk % tk == 0 and Lq % nq == 0 and dh * 2 == LANES
    expand = (lax.broadcasted_iota(I32, (nsp, Lk), 1) // SEL_BLOCK ==
              lax.broadcasted_iota(I32, (nsp, Lk), 0)).astype(BF16)
    slopes = tuple(2.0 ** (-8.0 * (h + 1) / H) for h in range(H))
    assert B % seqs == 0
    cfg = dict(seqs=seqs, nq=nq, t0=t0, tk=tk, n_cmp=n_cmp, cmp_blk=cmp_blk, n_sel=n_sel, kvh=kvh, hg=hg, dh=dh,
               slopes=slopes, win_rows=win_rows, w0=w0)
    R = hg * nq
    if causal_tiles:
        kern = functools.partial(_nsa_multi_kernel, causal_tiles=True, **cfg)
        scratch = [pltpu.VMEM((kvh, R, LANES), F32), pltpu.VMEM((kvh, R, LANES), F32),
                   pltpu.VMEM((kvh, R, 2 * dh), F32), pltpu.SMEM((kvh * (Lk // tk),), I32)]
    else:
        assert win_rows == kw.shape[1]
        kern = functools.partial(_nsa_staged_kernel, **cfg)
        scratch = []
    qmap = lambda b, i: (b, i, 0)
    kv_spec = lambda arr, c: pl.BlockSpec((seqs, arr.shape[1], cw), lambda b, i, c=c: (b, 0, c))
    return pl.pallas_call(
        kern, grid=(B // seqs, Lq // nq),
        in_specs=[pl.BlockSpec((seqs, nq, QW), qmap), pl.BlockSpec((seqs, nq, LANES), qmap),
                  kv_spec(kc, cols[0]), kv_spec(vc, cols[1]), kv_spec(ks, cols[2]), kv_spec(vs, cols[3]),
                  kv_spec(kw, cols[4]), kv_spec(vw, cols[5]),
                  pl.BlockSpec((nsp, Lk), lambda b, i: (0, 0))],
        out_specs=pl.BlockSpec((seqs, nq, QW), qmap),
        out_shape=jax.ShapeDtypeStruct((B, Lq, QW), F32),
        scratch_shapes=scratch,
        compiler_params=_cparams("parallel", "arbitrary"), name="nsa_attend")(q, gt, kc, vc, ks, vs, kw, vw, expand)


def _head_tiles(w, H, kvh, dh, axis):
    hg = H // kvh
    parts = []
    for h in range(H):
        sl = [slice(None)] * w.ndim
        sl[axis] = slice(h * dh, (h + 1) * dh)
        piece = w[tuple(sl)]
        zero = jnp.zeros_like(piece)
        parts += [zero, piece] if ((h // hg) % 2) else [piece, zero]
    return jnp.concatenate(parts, axis=axis)


def kernel(x_prompt, x_sample, cache_kv_pages, cache_win, state_ssm, state_conv, page_table, ln_g, ln_b, m_w_in, m_conv_w, m_conv_b, m_dt_bias, m_a_log, m_d_skip, m_norm_w, m_w_out, w_kv_shared, w_cmp, pos_cmp, nsa_w_in, nsa_w_out, peer_w_q, peer_sub_keys, peer_u, peer_v):
    Bp, Lp, D = x_prompt.shape
    Bs, Ls, _ = x_sample.shape
    depth = ln_g.shape[0]
    n_a = m_w_in.shape[0]
    alpha = (2 * depth) ** DEPTH_ALPHA_POW
    Tp, Ts = Bp * Lp, Bs * Ls
    assert Tp % LANES == 0
    Tsp = -(-Ts // LANES) * LANES
    Xp = x_prompt.reshape(Tp, D)
    Xs = jnp.pad(x_sample.reshape(Ts, D), ((0, Tsp - Ts), (0, 0)))
    dims = (Bp, Lp, Bs, Ls)

    NP, PS, n_rows, kvh, dh = cache_kv_pages.shape
    cw = kvh * dh
    n_pages = page_table.shape[1]
    past_len = n_pages * PS
    win_cache = cache_win.shape[1]
    cmp_blk = w_cmp.shape[1]
    H = nsa_w_in.shape[2] // (dh + 3)

    ssm_p, ssm_s, conv_p, conv_s = [], [], [], []
    for i in range(depth):
        if i < n_a:
            mp = (m_w_in[i], m_conv_w[i], m_conv_b[i], m_dt_bias[i], m_a_log[i], m_d_skip[i], m_norm_w[i], m_w_out[i])
            Xp, Xs, hp, hs, cp, cs = _mamba_layer(Xp, Xs, dims, state_conv[i], state_ssm[i], mp,
                                                  ln_g[i, 0], ln_b[i, 0], alpha)
            ssm_p.append(hp)
            ssm_s.append(hs)
            conv_p.append(cp)
            conv_s.append(cs)
        else:
            if i == n_a:
                wkv = w_kv_shared.astype(BF16)
                kv_p = _matmul(Xp, wkv).reshape(Bp, Lp, 6 * cw)
                kv_s = _matmul(Xs, wkv)[:Ts].reshape(Bs, Ls, 6 * cw)
                kvb_p = kv_p.astype(BF16)
                n_sel_p = -(-Lp // SEL_BLOCK)
                Lpp = n_sel_p * SEL_BLOCK
                padp = lambda v: jnp.pad(v, ((0, 0), (0, Lpp - Lp), (0, 0)))
                nc_p = -(-(Lpp // cmp_blk) // LANES) * LANES
                kc_p = _compress(padp(kv_p[:, :, 0:cw]), w_cmp[0], pos_cmp[0], nc_p)
                vc_p = _compress(padp(kv_p[:, :, cw:2 * cw]), w_cmp[1], pos_cmp[1], nc_p)
                tk_p = _pick(Lp, (512, 256, 128))
                new_rows = jnp.pad(kv_s[:, :, :4 * cw], ((0, 0), (0, SUBLANES - Ls), (0, 0)))
                kcb_s, vcb_s, ks_s, vs_s = _gather_pages(cache_kv_pages.reshape(NP, PS, n_rows * cw), page_table,
                                                         new_rows, cmp_blk)
                Lk_s = ks_s.shape[1]
                n_sel_s = -(-(past_len + Ls) // SEL_BLOCK)
                nc_s = -(-(Lk_s // cmp_blk) // LANES) * LANES
                kc_s = _compress_blocks(kcb_s, w_cmp[0], pos_cmp[0], nc_s)
                vc_s = _compress_blocks(vcb_s, w_cmp[1], pos_cmp[1], nc_s)
                win_full_s = jnp.concatenate([cache_win, kv_s[:, :, 4 * cw:].reshape(Bs, Ls, 2, kvh, dh)], axis=1)
                Lw = win_cache + Ls
                Lwp = -(-Lw // LANES) * LANES
                wflat = jnp.pad(win_full_s.reshape(Bs, Lw, 2 * cw), ((0, 0), (0, Lwp - Lw), (0, 0))).astype(BF16)
            j = i - n_a
            wq = _head_tiles(nsa_w_in[j][:, :H * dh], H, kvh, dh, axis=1).astype(BF16)
            wg = _pad_cols(nsa_w_in[j][:, H * dh:], LANES).astype(BF16)
            wo = _head_tiles(nsa_w_out[j], H, kvh, dh, axis=0).astype(BF16)
            QW = wq.shape[1]
            o_p = _nsa_attend(_matmul(Xp, wq).reshape(Bp, Lp, QW), _matmul(Xp, wg).reshape(Bp, Lp, LANES),
                              kc_p, vc_p, kvb_p, kvb_p, kvb_p, kvb_p, (0, 0, 2, 3, 4, 5),
                              nq=Q_BLOCK, t0=0, tk=tk_p, n_cmp=Lpp // cmp_blk, cmp_blk=cmp_blk, n_sel=n_sel_p,
                              kvh=kvh, dh=dh, win_rows=min(WINDOW + Q_BLOCK, Lp), w0=0, causal_tiles=True)
            Xp = _matmul_ln(o_p.reshape(Tp, QW), wo, Xp, ln_g[i, 0], ln_b[i, 0], alpha)
            pad_q = lambda v: jnp.pad(v[:Ts].reshape(Bs, Ls, -1), ((0, 0), (0, SUBLANES - Ls), (0, 0)))
            o_s = _nsa_attend(pad_q(_matmul(Xs, wq)), pad_q(_matmul(Xs, wg)), kc_s, vc_s, ks_s, vs_s, wflat, wflat,
                              (0, 0, 0, 0, 0, 1),
                              nq=SUBLANES, t0=past_len, tk=Lk_s, n_cmp=-(-(past_len + Ls) // SEL_BLOCK) * (SEL_BLOCK // cmp_blk),
                              cmp_blk=cmp_blk, n_sel=n_sel_s, kvh=kvh, dh=dh, win_rows=Lwp,
                              w0=past_len - win_cache, causal_tiles=False, seqs=_pick(Bs, (4, 2, 1)))
            o_s = jnp.pad(o_s[:, :Ls].reshape(Ts, QW), ((0, Tsp - Ts), (0, 0)))
            Xs = _matmul_ln(o_s, wo, Xs, ln_g[i, 0], ln_b[i, 0], alpha)
        peer = (peer_w_q[i].astype(BF16), peer_sub_keys[i], peer_u[i].astype(BF16), peer_v[i].astype(BF16),
                ln_g[i, 1], ln_b[i, 1], alpha)
        Xp = _peer_layer(Xp, *peer)
        Xs = _peer_layer(Xs, *peer)

    rows_p = kv_p[:, :, :4 * cw].reshape(Bp, Lp, n_rows, kvh, dh)
    rows_s = kv_s[:, :, :4 * cw].reshape(Bs, Ls, n_rows, kvh, dh)
    win_p = kv_p[:, :, 4 * cw:].reshape(Bp, Lp, 2, kvh, dh)[:, -min(WINDOW, Lp):]
    return (Xp.reshape(Bp, Lp, D), Xs[:Ts].reshape(Bs, Ls, D), rows_p, rows_s, win_p,
            win_full_s[:, -win_cache:], jnp.stack(ssm_p), jnp.stack(ssm_s), jnp.stack(conv_p), jnp.stack(conv_s))
```

```python
import functools
import math

import jax
import jax.numpy as jnp
from jax import lax
from jax.experimental import pallas as pl
from jax.experimental.pallas import tpu as pltpu

F32 = jnp.float32
BF16 = jnp.bfloat16
I32 = jnp.int32

DEPTH_ALPHA_POW = 0.25
LN_EPS = 1e-5
SEL_BLOCK = 64
N_SEL = 16
WINDOW = 512
FORCE_SCORE = 1.0e4
PEER_TOPK = 16
SSD_CHUNK = 128
Q_BLOCK = 128
GELU_C0 = math.sqrt(2.0 / math.pi)
GELU_C1 = 0.044715 * GELU_C0
LOG2E = 1.0 / math.log(2.0)
W_TOKENS_PER_ITER = 16
MAX_FULL_N = 3072

LANES = 128
SUBLANES = 8
VMEM_LIMIT_BYTES = 56 * 1024 * 1024

_NT = (((1,), (1,)), ((), ()))
_TN = (((0,), (0,)), ((), ()))
_HI = lax.Precision.HIGHEST


def _cparams(*sem):
    return pltpu.CompilerParams(dimension_semantics=sem, vmem_limit_bytes=VMEM_LIMIT_BYTES)


def _pick(n, cands):
    for c in cands:
        if n % c == 0:
            return c
    raise ValueError(f"no tile in {cands} divides {n}")


def _pad_cols(w, n):
    return jnp.pad(w, ((0, 0), (0, n - w.shape[1])))


def _mm_kernel(x_ref, w_ref, o_ref):
    o_ref[...] = jnp.dot(x_ref[...].astype(BF16), w_ref[...],
                         preferred_element_type=F32).astype(o_ref.dtype)


def _mm_bias_kernel(x_ref, xb_ref, w_ref, o_ref):
    x = (x_ref[...] + xb_ref[...]).astype(BF16)
    o_ref[...] = jnp.dot(x, w_ref[...], preferred_element_type=F32).astype(o_ref.dtype)


def _matmul(x, w, *, xbias=None, out_dtype=F32, tm_cap=512):
    M, K = x.shape
    N = w.shape[1]
    tm = _pick(M, tuple(t for t in (512, 256, 128, 64, 32, 16, 8) if t <= tm_cap))
    tn = N if N <= MAX_FULL_N else _pick(N, (1024, 512, 256, 128))
    x_spec = pl.BlockSpec((tm, K), lambda j, i: (i, 0))
    w_spec = pl.BlockSpec((K, tn), lambda j, i: (0, j))
    o_spec = pl.BlockSpec((tm, tn), lambda j, i: (i, j))
    if xbias is None:
        body, specs, args = _mm_kernel, [x_spec, w_spec], (x, w)
    else:
        b_spec = pl.BlockSpec((1, K), lambda j, i: (0, 0))
        body, specs, args = _mm_bias_kernel, [x_spec, b_spec, w_spec], (x, xbias, w)
    return pl.pallas_call(
        body, grid=(N // tn, M // tm), in_specs=specs, out_specs=o_spec,
        out_shape=jax.ShapeDtypeStruct((M, N), out_dtype),
        compiler_params=_cparams("parallel", "parallel"), name="proj")(*args)


def _deepnorm_ln(v, g, b):
    mu = jnp.mean(v, axis=-1, keepdims=True)
    d = v - mu
    var = jnp.mean(d * d, axis=-1, keepdims=True)
    return d * lax.rsqrt(var + LN_EPS) * g + b


def _mm_ln_kernel(x_ref, w_ref, r_ref, g_ref, b_ref, o_ref, *, alpha):
    y = jnp.dot(x_ref[...].astype(BF16), w_ref[...], preferred_element_type=F32)
    o_ref[...] = _deepnorm_ln(alpha * r_ref[...] + y, g_ref[...], b_ref[...])


def _matmul_ln(x, w, resid, g, b, alpha):
    M, K = x.shape
    D = w.shape[1]
    tm = _pick(M, (512, 256, 128, 64, 32, 16, 8))
    row = lambda i: (i, 0)
    fixed = lambda i: (0, 0)
    return pl.pallas_call(
        functools.partial(_mm_ln_kernel, alpha=alpha), grid=(M // tm,),
        in_specs=[pl.BlockSpec((tm, K), row), pl.BlockSpec((K, D), fixed),
                  pl.BlockSpec((tm, D), row), pl.BlockSpec((1, D), fixed), pl.BlockSpec((1, D), fixed)],
        out_specs=pl.BlockSpec((tm, D), row),
        out_shape=jax.ShapeDtypeStruct((M, D), F32),
        compiler_params=_cparams("parallel"), name="proj_ln")(x, w, resid, g.reshape(1, D), b.reshape(1, D))


def _softplus(x):
    return jnp.maximum(x, 0.0) + jnp.log1p(jnp.exp(-jnp.abs(x)))


def _silu(x):
    return x * jax.nn.sigmoid(x)


def _ssd_kernel(z_ref, xbc_ref, dtr_ref, cbuf_ref, h0_ref, cw_ref, cb_ref, dtb_ref, alog_ref,
                dsk_ref, nw_ref, y_ref, hn_ref, xf_scr, h_scr, *, Lc, H, P, N, G, n_valid, conv_w):
    c = pl.program_id(1)
    DI = H * P
    HG = H // G
    GP = HG * P
    tail = xf_scr.shape[0] - Lc

    @pl.when(c == 0)
    def _():
        xf_scr[0:tail, :] = cbuf_ref[0]
        h_scr[...] = h0_ref[0]

    xf_scr[tail:tail + Lc, :] = xbc_ref[0]
    conv = cb_ref[...]
    for k in range(conv_w):
        conv = conv + xf_scr[pl.ds(tail - (conv_w - 1) + k, Lc), :] * cw_ref[k:k + 1, :]
    conv = _silu(conv)
    xf_scr[0:tail, :] = xf_scr[pl.ds(Lc, tail), :]

    xs = conv[:, :DI]
    Bm = conv[:, DI:DI + G * N]
    Cm = conv[:, DI + G * N:]

    lane = lax.broadcasted_iota(I32, (Lc, LANES), 1)
    rowi = lax.broadcasted_iota(I32, (Lc, LANES), 0)
    dt = jnp.where((lane < H) & (rowi < n_valid), _softplus(dtr_ref[0] + dtb_ref[...]), 0.0)
    a = dt * (-jnp.exp(alog_ref[...]))
    ti = lax.broadcasted_iota(I32, (Lc, Lc), 0)
    si = lax.broadcasted_iota(I32, (Lc, Lc), 1)
    causal = ti >= si
    tril = jnp.where(causal, 1.0, 0.0).astype(F32)
    eye = jnp.where(lax.broadcasted_iota(I32, (LANES, LANES), 0) ==
                    lax.broadcasted_iota(I32, (LANES, LANES), 1), 1.0, 0.0).astype(F32)
    acum = jnp.dot(tril, a, precision=_HI, preferred_element_type=F32)
    acum_t = lax.dot_general(eye, acum, _NT, precision=_HI, preferred_element_type=F32)
    dt_t = lax.dot_general(eye, dt, _NT, precision=_HI, preferred_element_type=F32)
    alast = acum[Lc - 1:Lc, :]
    wend = jnp.exp(alast - acum) * dt
    ea = jnp.exp(acum)
    edec = jnp.exp(alast)
    dsk = dsk_ref[...]
    first_half = lax.broadcasted_iota(I32, (Lc, 2 * P), 1) < P

    def pair(v, h):
        rows = v.shape[0]
        return jnp.where(first_half[:rows], jnp.broadcast_to(v[:, h:h + 1], (rows, 2 * P)),
                         jnp.broadcast_to(v[:, h + 1:h + 2], (rows, 2 * P)))

    y_parts = []
    for g in range(G):
        Bg = Bm[:, g * N:(g + 1) * N].astype(BF16)
        Cg = Cm[:, g * N:(g + 1) * N].astype(BF16)
        cb = lax.dot_general(Cg, Bg, _NT, preferred_element_type=F32)
        Sg = h_scr[g * GP:(g + 1) * GP, :]
        yoff = lax.dot_general(Cg, Sg.astype(BF16), _NT, preferred_element_type=F32)
        xw_parts = []
        for pr in range(HG // 2):
            h = g * HG + 2 * pr
            xs_pair = xs[:, h * P:(h + 2) * P]
            xs_pair_b = xs_pair.astype(BF16)
            outs = []
            for hh in (h, h + 1):
                seg = acum[:, hh:hh + 1] - acum_t[hh:hh + 1, :]
                dec = jnp.exp(jnp.where(causal, seg, -jnp.inf))
                mix = cb * dec * dt_t[hh:hh + 1, :]
                outs.append(jnp.dot(mix.astype(BF16), xs_pair_b, preferred_element_type=F32))
            ydiag = jnp.where(first_half, outs[0], outs[1])
            y_pair = ydiag + yoff[:, pr * 2 * P:(pr + 1) * 2 * P] * pair(ea, h) + pair(dsk, h) * xs_pair
            y_parts.append(y_pair)
            xw_parts.append(xs_pair * pair(wend, h))
        xw = jnp.concatenate(xw_parts, axis=1).astype(BF16)
        states = lax.dot_general(xw, Bg, _TN, preferred_element_type=F32)
        for hl in range(HG):
            hh = g * HG + hl
            r0 = g * GP + hl * P
            h_scr[r0:r0 + P, :] = edec[:, hh:hh + 1] * Sg[hl * P:(hl + 1) * P, :] + states[hl * P:(hl + 1) * P, :]

    z = z_ref[0]
    gn = DI // G
    outs = []
    for g in range(G):
        parts = y_parts[g * (HG // 2):(g + 1) * (HG // 2)]
        yg = jnp.concatenate(parts, axis=1) if len(parts) > 1 else parts[0]
        yg = yg * _silu(z[:, g * gn:(g + 1) * gn])
        ms = jnp.mean(yg * yg, axis=-1, keepdims=True)
        outs.append(yg * lax.rsqrt(ms + 1e-5))
    y_ref[0] = jnp.concatenate(outs, axis=1) * nw_ref[...]

    @pl.when(c == pl.num_programs(1) - 1)
    def _():
        hn_ref[0] = h_scr[...]


def _ssd(z, xbc, dtr, cbuf, h0, conv_w, conv_b, dt_bias, a_log, d_skip, norm_w, *, Lc, n_valid, G):
    B, L, DI = z.shape
    CD = xbc.shape[-1]
    _, H, P, N = h0.shape
    KW = conv_w.shape[0]
    assert P * 2 == LANES and (H // G) % 2 == 0 and H <= LANES and L % Lc == 0
    padl = lambda v: jnp.pad(v.astype(F32).reshape(1, -1), ((0, 0), (0, LANES - H)))
    seq = lambda b, c: (b, c, 0)
    per_b = lambda b, c: (b, 0, 0)
    fixed = lambda b, c: (0, 0)
    kern = functools.partial(_ssd_kernel, Lc=Lc, H=H, P=P, N=N, G=G, n_valid=n_valid, conv_w=KW)
    y, hn = pl.pallas_call(
        kern, grid=(B, L // Lc),
        in_specs=[pl.BlockSpec((1, Lc, DI), seq), pl.BlockSpec((1, Lc, CD), seq),
                  pl.BlockSpec((1, Lc, LANES), seq), pl.BlockSpec((1, SUBLANES, CD), per_b),
                  pl.BlockSpec((1, H * P, N), per_b),
                  pl.BlockSpec((KW, CD), fixed), pl.BlockSpec((1, CD), fixed),
                  pl.BlockSpec((1, LANES), fixed), pl.BlockSpec((1, LANES), fixed),
                  pl.BlockSpec((1, LANES), fixed), pl.BlockSpec((1, DI), fixed)],
        out_specs=[pl.BlockSpec((1, Lc, DI), seq), pl.BlockSpec((1, H * P, N), per_b)],
        out_shape=[jax.ShapeDtypeStruct((B, L, DI), F32), jax.ShapeDtypeStruct((B, H * P, N), F32)],
        scratch_shapes=[pltpu.VMEM((SUBLANES + Lc, CD), F32), pltpu.VMEM((H * P, N), F32)],
        compiler_params=_cparams("parallel", "arbitrary"), name="ssd",
    )(z, xbc, dtr, cbuf, h0.reshape(B, H * P, N), conv_w, conv_b.reshape(1, CD),
      padl(dt_bias), padl(a_log), padl(d_skip), norm_w.reshape(1, DI))
    return y, hn.reshape(B, H, P, N)


def _mamba_layer(Xp, Xs, dims, state_conv_i, state_ssm_i, params, ln_g, ln_b, alpha):
    Bp, Lp, Bs, Ls = dims
    w_in, conv_w, conv_b, dt_bias, a_log, d_skip, norm_w, w_out = params
    Tp = Bp * Lp
    DI = norm_w.shape[0]
    CD = conv_w.shape[1]
    _, H, P, N = state_ssm_i.shape
    G = (CD - DI) // (2 * N)
    KW = conv_w.shape[0]
    Ts = Bs * Ls
    wz = w_in[:, :DI].astype(BF16)
    wx = w_in[:, DI:DI + CD].astype(BF16)
    wd = _pad_cols(w_in[:, DI + CD:], LANES).astype(BF16)
    wo = w_out.astype(BF16)
    args = (conv_w, conv_b, dt_bias, a_log, d_skip, norm_w)

    xbc_p = _matmul(Xp, wx).reshape(Bp, Lp, CD)
    Lc = SSD_CHUNK if Lp % SSD_CHUNK == 0 else Lp
    yp, hp = _ssd(_matmul(Xp, wz).reshape(Bp, Lp, DI), xbc_p, _matmul(Xp, wd).reshape(Bp, Lp, LANES),
                  jnp.zeros((Bp, SUBLANES, CD), F32), jnp.zeros((Bp, H, P, N), F32), *args,
                  Lc=Lc, n_valid=Lc, G=G)
    conv_p = xbc_p[:, Lp - (KW - 1):]
    Xp = _matmul_ln(yp.reshape(Tp, DI), wo, Xp, ln_g, ln_b, alpha)

    Lsp = -(-Ls // SUBLANES) * SUBLANES
    pad_rows = lambda v: jnp.pad(v[:Ts].reshape(Bs, Ls, -1), ((0, 0), (0, Lsp - Ls), (0, 0)))
    xbc_s = _matmul(Xs, wx)[:Ts].reshape(Bs, Ls, CD)
    cbuf_s = jnp.pad(state_conv_i, ((0, 0), (SUBLANES - (KW - 1), 0), (0, 0)))
    ys, hs = _ssd(pad_rows(_matmul(Xs, wz)), pad_rows(xbc_s), pad_rows(_matmul(Xs, wd)), cbuf_s, state_ssm_i,
                  *args, Lc=Lsp, n_valid=Ls, G=G)
    conv_s = jnp.concatenate([state_conv_i, xbc_s], axis=1)[:, -(KW - 1):]
    ys = jnp.pad(ys[:, :Ls].reshape(Ts, DI), ((0, Xs.shape[0] - Ts), (0, 0)))
    Xs = _matmul_ln(ys, wo, Xs, ln_g, ln_b, alpha)
    return Xp, Xs, hp, hs, conv_p, conv_s


def _topk_rows(s, k):
    n = float(s.shape[0])
    rows = lax.broadcasted_iota(I32, s.shape, 0).astype(F32)
    vals, idxs = [], []
    for _ in range(k):
        m = jnp.max(s, axis=0, keepdims=True)
        i = jnp.min(jnp.where(s == m, rows, n), axis=0, keepdims=True)
        vals.append(m)
        idxs.append(i)
        s = jnp.where(rows == i, -jnp.inf, s)
    return vals, idxs


def _peer_topk_kernel(q_ref, k_ref, i1_ref, i2_ref, g_ref, *, heads, half, topk, n_keys):
    k1 = k_ref[0]
    k2 = k_ref[1]
    tb = q_ref.shape[0]
    cand_ab = [(a, b) for a in range(topk) for b in range(topk // (a + 1))]
    n_pad = -len(cand_ab) % SUBLANES
    codes, gates = [], []
    for h in range(heads):
        q1 = q_ref[:, (2 * h) * half:(2 * h + 1) * half]
        q2 = q_ref[:, (2 * h + 1) * half:(2 * h + 2) * half]
        s1 = lax.dot_general(k1, q1, _NT, preferred_element_type=F32)
        s2 = lax.dot_general(k2, q2, _NT, preferred_element_type=F32)
        v1, i1 = _topk_rows(s1, topk)
        v2, i2 = _topk_rows(s2, topk)
        cand = jnp.concatenate([v1[a] + v2[b] for a, b in cand_ab] +
                               [jnp.full((n_pad, tb), -jnp.inf, F32)], axis=0)
        code = jnp.concatenate([i1[a] * float(n_keys) + i2[b] for a, b in cand_ab] +
                               [jnp.zeros((n_pad, tb), F32)], axis=0)
        sc, pos = _topk_rows(cand, topk)
        crow = lax.broadcasted_iota(I32, cand.shape, 0).astype(F32)
        codes += [jnp.max(jnp.where(crow == p, code, -1.0), axis=0, keepdims=True) for p in pos]
        ex = jnp.exp(jnp.concatenate(sc, axis=0) - sc[0])
        gates.append(ex / jnp.sum(ex, axis=0, keepdims=True))
    code_t = jnp.concatenate(codes, axis=0).T.astype(I32)
    i1_ref[...] = code_t // n_keys
    i2_ref[...] = code_t % n_keys
    g_ref[...] = jnp.concatenate(gates, axis=0).T


def _peer_w_kernel(i1_ref, i2_ref, g_ref, w_ref, *, n_keys, nj):
    sub = lax.broadcasted_iota(I32, (n_keys, i1_ref.shape[1]), 0)

    def body(tb, carry):
        for u in range(W_TOKENS_PER_ITER):
            t = tb * W_TOKENS_PER_ITER + u
            i1 = i1_ref[pl.ds(t, 1), :]
            i2 = i2_ref[pl.ds(t, 1), :]
            g = g_ref[pl.ds(t, 1), :]
            onehot1 = jnp.where(sub == i1, 1.0, 0.0).astype(BF16)
            gated2 = jnp.where(sub == i2, g, 0.0).astype(BF16)
            w_t = lax.dot_general(onehot1, gated2, _NT, preferred_element_type=F32)
            r0 = pl.multiple_of(t * nj, nj)
            for k in range(n_keys // nj):
                w_ref[0, k, pl.ds(r0, nj), :] = w_t[k * nj:(k + 1) * nj, :]
        return carry

    lax.fori_loop(0, i1_ref.shape[0] // W_TOKENS_PER_ITER, body, 0)


def _gelu_tanh(x):
    hx = 0.5 * x
    inner = x * (GELU_C0 + GELU_C1 * (x * x))
    return hx * jnp.tanh(inner) + hx


def _peer_dense_kernel(x_ref, u_ref, v_ref, w_ref, g_ref, b_ref, o_ref, acc_ref, xb_ref, *, alpha, nj):
    k = pl.program_id(1)
    td = x_ref.shape[0]

    @pl.when(k == 0)
    def _():
        acc_ref[...] = jnp.zeros_like(acc_ref)
        xb_ref[...] = x_ref[...].astype(BF16)

    act = lax.dot_general(xb_ref[...], u_ref[...], _NT, preferred_element_type=F32)
    w = jnp.concatenate([w_ref[0, 0, pl.ds(j, td, stride=nj), :] for j in range(nj)], axis=1)
    coef = (_gelu_tanh(act) * w).astype(BF16)
    acc_ref[...] += jnp.dot(coef, v_ref[...], preferred_element_type=F32)

    @pl.when(k == pl.num_programs(1) - 1)
    def _():
        o_ref[...] = _deepnorm_ln(alpha * x_ref[...] + acc_ref[...], g_ref[...], b_ref[...])


def _peer_layer(X, w_q, sub_keys, u_tab, v_tab, ln_g, ln_b, alpha):
    T, D = X.shape
    _, n_keys, half = sub_keys.shape
    heads = w_q.shape[1] // (2 * half)
    nsel = heads * PEER_TOPK
    assert n_keys == LANES and nsel == LANES and half % LANES == 0
    q = _matmul(X, w_q)

    tb = LANES
    tok = lambda i: (i, 0)
    sel_shape = jax.ShapeDtypeStruct((T, nsel), I32)
    i1, i2, gate = pl.pallas_call(
        functools.partial(_peer_topk_kernel, heads=heads, half=half, topk=PEER_TOPK, n_keys=n_keys),
        grid=(T // tb,),
        in_specs=[pl.BlockSpec((tb, q.shape[1]), tok), pl.BlockSpec((2, n_keys, half), lambda i: (0, 0, 0))],
        out_specs=[pl.BlockSpec((tb, nsel), tok)] * 3,
        out_shape=[sel_shape, sel_shape, jax.ShapeDtypeStruct((T, nsel), F32)],
        compiler_params=_cparams("parallel"), name="peer_topk")(q, sub_keys)

    td = _pick(T, (1024, 768, 512, 256, 128))
    nj = SUBLANES
    et = nj * n_keys
    sub_blocks = td // tb
    w = pl.pallas_call(
        functools.partial(_peer_w_kernel, n_keys=n_keys, nj=nj), grid=(T // tb,),
        in_specs=[pl.BlockSpec((tb, nsel), tok)] * 3,
        out_specs=pl.BlockSpec((1, n_keys // nj, tb * nj, n_keys),
                               lambda i: (i // sub_blocks, 0, i % sub_blocks, 0)),
        out_shape=jax.ShapeDtypeStruct((T // td, n_keys // nj, td * nj, n_keys), F32),
        compiler_params=_cparams("parallel"), name="peer_w")(i1, i2, gate)

    return pl.pallas_call(
        functools.partial(_peer_dense_kernel, alpha=alpha, nj=nj),
        grid=(T // td, n_keys // nj),
        in_specs=[pl.BlockSpec((td, D), lambda i, k: (i, 0)),
                  pl.BlockSpec((et, D), lambda i, k: (k, 0)),
                  pl.BlockSpec((et, D), lambda i, k: (k, 0)),
                  pl.BlockSpec((1, 1, td * nj, n_keys), lambda i, k: (i, k, 0, 0)),
                  pl.BlockSpec((1, D), lambda i, k: (0, 0)), pl.BlockSpec((1, D), lambda i, k: (0, 0))],
        out_specs=pl.BlockSpec((td, D), lambda i, k: (i, 0)),
        out_shape=jax.ShapeDtypeStruct((T, D), F32),
        scratch_shapes=[pltpu.VMEM((td, D), F32), pltpu.VMEM((td, D), BF16)],
        compiler_params=_cparams("parallel", "arbitrary"), name="peer_dense",
    )(X, u_tab, v_tab, w, ln_g.reshape(1, D), ln_b.reshape(1, D))


def _gather_kernel(pt_ref, pa_ref, pb_ref, new_ref, kc_ref, vc_ref, ks_ref, vs_ref, col_scr, *, n_steps, cw, blk):
    del pt_ref
    p = pl.program_id(1)
    ps = pa_ref.shape[1]
    per_page = ps // blk

    @pl.when(p < n_steps)
    def _():
        for j, page in enumerate((pa_ref, pb_ref)):
            n_tiles = 2 * cw // LANES
            for c in range(n_tiles):
                col_scr[j * n_tiles + c] = page[0, :, c * LANES:(c + 1) * LANES]
            for r, ref in enumerate((kc_ref, vc_ref)):
                for l in range(blk):
                    for c in range(cw // LANES):
                        ref[0, j * per_page:(j + 1) * per_page, l * cw + c * LANES:l * cw + (c + 1) * LANES] = \
                            col_scr[j * n_tiles + r * (cw // LANES) + c, pl.ds(l, per_page, stride=blk), :]
            ks_ref[0, j * ps:(j + 1) * ps, :] = page[0, :, 2 * cw:3 * cw].astype(BF16)
            vs_ref[0, j * ps:(j + 1) * ps, :] = page[0, :, 3 * cw:4 * cw].astype(BF16)

    @pl.when(p == n_steps)
    def _():
        new = new_ref[0]
        n_new = new.shape[0]
        for r, ref in enumerate((kc_ref, vc_ref)):
            ref[0] = jnp.zeros(ref.shape[1:], F32)
            for l in range(n_new):
                ref[0, 0:1, l * cw:(l + 1) * cw] = new[l:l + 1, r * cw:(r + 1) * cw]
        for r, ref in ((2, ks_ref), (3, vs_ref)):
            rows = jnp.concatenate([new[:, r * cw:(r + 1) * cw], jnp.zeros((2 * ps - n_new, cw), F32)], axis=0)
            ref[0] = rows.astype(BF16)


def _gather_pages(pages, page_table, new_rows, blk):
    NP, PS, C = pages.shape
    B, n_pages = page_table.shape
    assert n_pages % 2 == 0 and PS % blk == 0 and (2 * PS // blk) % SUBLANES == 0 and new_rows.shape[1] <= blk
    cw = C // 4
    n_steps = n_pages // 2
    L = (n_pages + 2) * PS
    nb = 2 * PS // blk
    page_spec = lambda j: pl.BlockSpec(
        (1, PS, C), lambda b, p, pt: (pt[b, jnp.minimum(2 * p + j, n_pages - 2 + j)], 0, 0))
    cmp_spec = pl.BlockSpec((1, nb, blk * cw), lambda b, p, pt: (b, p, 0))
    sel_spec = pl.BlockSpec((1, 2 * PS, cw), lambda b, p, pt: (b, p, 0))
    return pl.pallas_call(
        functools.partial(_gather_kernel, n_steps=n_steps, cw=cw, blk=blk),
        grid_spec=pltpu.PrefetchScalarGridSpec(
            num_scalar_prefetch=1, grid=(B, n_steps + 1),
            in_specs=[page_spec(0), page_spec(1),
                      pl.BlockSpec((1, new_rows.shape[1], C), lambda b, p, pt: (b, 0, 0))],
            out_specs=[cmp_spec, cmp_spec, sel_spec, sel_spec],
            scratch_shapes=[pltpu.VMEM((2 * (2 * cw // LANES), PS, LANES), F32)]),
        out_shape=[jax.ShapeDtypeStruct((B, L // blk, blk * cw), F32)] * 2 +
                  [jax.ShapeDtypeStruct((B, L, cw), BF16)] * 2,
        compiler_params=_cparams("parallel", "arbitrary"), name="kv_gather")(page_table, pages, pages, new_rows)


def _compress_blocks(xb, w_c, pos_c, nc_pad):
    B, n, K = xb.shape
    blk, dh, _ = w_c.shape
    C = K // blk
    G = C // dh
    w_bd = jnp.einsum('lde,gh->lgdhe', w_c, jnp.eye(G, dtype=w_c.dtype)).reshape(K, C)
    xbias = jnp.broadcast_to(pos_c[:, None, :], (blk, G, dh)).reshape(1, K)
    out = _matmul(xb.reshape(B * n, K), w_bd.astype(BF16), xbias=xbias, tm_cap=128)
    return jnp.pad(out.reshape(B, n, C), ((0, 0), (0, nc_pad - n), (0, 0)))


def _compress(x, w_c, pos_c, nc_pad):
    B, L, C = x.shape
    blk = w_c.shape[0]
    return _compress_blocks(x.reshape(B, L // blk, blk * C), w_c, pos_c, nc_pad)


def _lane_tiles(x):
    return [x[:, j * LANES:(j + 1) * LANES] for j in range(x.shape[1] // LANES)]


def _row_max(x):
    return jnp.max(functools.reduce(jnp.maximum, _lane_tiles(x)), axis=-1, keepdims=True)


def _row_sum(x):
    return jnp.sum(functools.reduce(jnp.add, _lane_tiles(x)), axis=-1, keepdims=True)


def _split3_dot(x, m):
    x1 = x.astype(BF16)
    r1 = x - x1.astype(F32)
    x2 = r1.astype(BF16)
    x3 = (r1 - x2.astype(F32)).astype(BF16)
    d = lambda a: jnp.dot(a, m, preferred_element_type=F32)
    return d(x1) + d(x2) + d(x3)


def _nsa_staged_kernel(q_ref, gt_ref, kc_ref, vc_ref, ks_ref, vs_ref, kw_ref, vw_ref, ex_ref, o_ref, *,
                       seqs, nq, t0, tk, n_cmp, cmp_blk, n_sel, kvh, hg, dh, slopes, win_rows, w0):
    q0 = t0 + pl.program_id(1) * nq
    R = hg * nq
    NC = kc_ref.shape[1]
    Lk = ks_ref.shape[1]
    NSP = ex_ref.shape[0]
    assert tk == Lk
    qscale = dh ** -0.5 * LOG2E
    slopes2 = tuple(s * LOG2E for s in slopes)
    units = [(s, g) for s in range(seqs) for g in range(kvh)]
    col = lambda g: (g // 2) * 2 * dh

    def tq_like(shape):
        return q0 + lax.broadcasted_iota(I32, shape, 0)

    def attend(keys, vals, distms):
        s_alls = [lax.dot_general(qgs[u], keys[u], _NT, preferred_element_type=F32) for u in range(len(units))]
        s2s = [[s_alls[u][h * nq:(h + 1) * nq] - slopes2[g * hg + h] * distms[u] for h in range(hg)]
               for u, (_, g) in enumerate(units)]
        ms = [[_row_max(x) for x in row] for row in s2s]
        es = [[jnp.exp2(x - jnp.where(m == -jnp.inf, 0.0, m)) for x, m in zip(xr, mr)] for xr, mr in zip(s2s, ms)]
        ls = [[_row_sum(e) for e in row] for row in es]
        ps = [[e / jnp.maximum(l, 1e-30) for e, l in zip(er, lr)] for er, lr in zip(es, ls)]
        os = [jnp.dot(jnp.concatenate(ps[u], axis=0).astype(BF16), vals[u], preferred_element_type=F32)
              for u in range(len(units))]
        return ps, os

    n_idx = lax.broadcasted_iota(I32, (nq, NC), 1)
    dist_c = tq_like((nq, NC)) - (n_idx * cmp_blk + (cmp_blk - 1))
    distm_c = jnp.where((dist_c >= 0) & (n_idx < n_cmp), dist_c.astype(F32), jnp.inf)
    dist_w = tq_like((nq, win_rows)) - (w0 + lax.broadcasted_iota(I32, (nq, win_rows), 1))
    distm_w = jnp.where((dist_w >= 0) & (dist_w < WINDOW), dist_w.astype(F32), jnp.inf)
    dist_s = tq_like((nq, Lk)) - lax.broadcasted_iota(I32, (nq, Lk), 1)
    per_sel = SEL_BLOCK // cmp_blk
    pair = jnp.where(lax.broadcasted_iota(I32, (NC, NSP), 0) // per_sel ==
                     lax.broadcasted_iota(I32, (NC, NSP), 1), 1.0, 0.0).astype(BF16)
    blk = lax.broadcasted_iota(I32, (nq, NSP), 1)
    tqs = tq_like((nq, NSP))
    cur = tqs // SEL_BLOCK
    forced = (blk == 0) | (blk == cur) | (blk == cur - 1)
    sel_valid = blk * SEL_BLOCK <= tqs

    qgs = [(jnp.concatenate([q_ref[s, :, (g * hg + h) * 2 * dh:(g * hg + h + 1) * 2 * dh] for h in range(hg)],
                            axis=0) * qscale).astype(BF16) for s, g in units]
    tile = lambda ref, s, g: ref[s, :, col(g):col(g) + 2 * dh]
    p_cmp, o_cmp = attend([tile(kc_ref, s, g).astype(BF16) for s, g in units],
                          [tile(vc_ref, s, g).astype(BF16) for s, g in units], [distm_c] * len(units))
    _, o_win = attend([tile(kw_ref, s, g) for s, g in units], [tile(vw_ref, s, g) for s, g in units],
                      [distm_w] * len(units))

    psums = [functools.reduce(jnp.add, row) for row in p_cmp]
    imps = [_split3_dot(x, pair) for x in psums]
    scores = [jnp.where(blk < n_sel, jnp.where(forced, FORCE_SCORE, jnp.where(sel_valid, imp, -1.0)), -2.0)
              for imp in imps]
    selm = _top_blocks(jnp.concatenate(scores, axis=0), min(N_SEL, n_sel), NSP)
    selb = selm.astype(BF16)
    expand = ex_ref[...]
    selxs = [jnp.dot(selb[u * nq:(u + 1) * nq], expand, preferred_element_type=F32) for u in range(len(units))]
    distm_s = [jnp.where((dist_s >= 0) & (x > 0.5), dist_s.astype(F32), jnp.inf) for x in selxs]
    _, o_sel = attend([tile(ks_ref, s, g) for s, g in units], [tile(vs_ref, s, g) for s, g in units], distm_s)

    lane = lax.broadcasted_iota(I32, (nq, 2 * dh), 1)
    for s in range(seqs):
        gates = jax.nn.sigmoid(gt_ref[s])
        for g in range(kvh):
            u = s * kvh + g
            lo = (g % 2) * dh
            keep = (lane >= lo) & (lane < lo + dh)
            for h in range(hg):
                hd = g * hg + h
                rows = slice(h * nq, (h + 1) * nq)
                o = (gates[:, 3 * hd:3 * hd + 1] * o_cmp[u][rows] + gates[:, 3 * hd + 1:3 * hd + 2] * o_sel[u][rows] +
                     gates[:, 3 * hd + 2:3 * hd + 3] * o_win[u][rows])
                o_ref[s, :, hd * 2 * dh:(hd + 1) * 2 * dh] = jnp.where(keep, o, 0.0)


def _top_blocks(score, n_top, nsp):
    if score.shape[0] % LANES == 0 and nsp == LANES:
        st = score.T
        brow = lax.broadcasted_iota(I32, st.shape, 0).astype(F32)
        sel_t = jnp.zeros(st.shape, F32)
        for _ in range(n_top):
            m = jnp.max(st, axis=0, keepdims=True)
            idx = jnp.min(jnp.where(st == m, brow, float(nsp)), axis=0, keepdims=True)
            hit = brow == idx
            sel_t = jnp.where(hit, 1.0, sel_t)
            st = jnp.where(hit, -jnp.inf, st)
        return sel_t.T
    bcol = lax.broadcasted_iota(I32, score.shape, 1)
    selm = jnp.zeros(score.shape, F32)
    for _ in range(n_top):
        m = jnp.max(score, axis=-1, keepdims=True)
        idx = jnp.min(jnp.where(score == m, bcol, nsp), axis=-1, keepdims=True)
        hit = bcol == idx
        selm = jnp.where(hit, 1.0, selm)
        score = jnp.where(hit, -jnp.inf, score)
    return selm


def _nsa_multi_kernel(q_ref, gt_ref, kc_ref, vc_ref, ks_ref, vs_ref, kw_ref, vw_ref, ex_ref, o_ref, *scratch,
                      seqs, **cfg):
    for s in range(seqs):
        one = lambda r: r.at[pl.ds(s, 1)]
        _nsa_kernel(one(q_ref), one(gt_ref), one(kc_ref), one(vc_ref), one(ks_ref), one(vs_ref), one(kw_ref),
                    one(vw_ref), ex_ref, one(o_ref), *scratch, **cfg)


def _nsa_kernel(q_ref, gt_ref, kc_ref, vc_ref, ks_ref, vs_ref, kw_ref, vw_ref, ex_ref, o_ref,
                m_scr, l_scr, acc_scr, flag_ref, *,
                nq, t0, tk, n_cmp, cmp_blk, n_sel, kvh, hg, dh, slopes, win_rows, w0, causal_tiles):
    i = pl.program_id(1)
    q0 = t0 + i * nq
    R = hg * nq
    NC = kc_ref.shape[1]
    Lk = ks_ref.shape[1]
    NSP = ex_ref.shape[0]
    qscale = dh ** -0.5 * LOG2E
    slopes2 = tuple(s * LOG2E for s in slopes)
    gates = jax.nn.sigmoid(gt_ref[0])

    def tq_like(shape):
        return q0 + lax.broadcasted_iota(I32, shape, 0)

    if causal_tiles:
        wstart = pl.multiple_of(jnp.maximum(q0 - WINDOW, 0), nq)
    else:
        wstart = 0

    def col(g):
        return (g // 2) * 2 * dh

    groups = range(kvh)

    def attend(keys, vals, distm, want_p):
        s_alls = [lax.dot_general(qgs[g], keys[g], _NT, preferred_element_type=F32) for g in groups]
        s2s = [[s_alls[g][h * nq:(h + 1) * nq] - slopes2[g * hg + h] * distm for h in range(hg)] for g in groups]
        ms = [[_row_max(x) for x in row] for row in s2s]
        es = [[jnp.exp2(x - jnp.where(m == -jnp.inf, 0.0, m)) for x, m in zip(xr, mr)] for xr, mr in zip(s2s, ms)]
        ls = [[jnp.maximum(_row_sum(e), 1e-30) for e in row] for row in es]
        if want_p:
            ps = [[e / l for e, l in zip(er, lr)] for er, lr in zip(es, ls)]
            os = [jnp.dot(jnp.concatenate(ps[g], axis=0).astype(BF16), vals[g], preferred_element_type=F32)
                  for g in groups]
            return ps, os
        os = [jnp.dot(jnp.concatenate(es[g], axis=0).astype(BF16), vals[g], preferred_element_type=F32) /
              jnp.concatenate(ls[g], axis=0) for g in groups]
        return None, os

    n_idx = lax.broadcasted_iota(I32, (nq, NC), 1)
    dist_c = tq_like((nq, NC)) - (n_idx * cmp_blk + (cmp_blk - 1))
    distm_c = jnp.where((dist_c >= 0) & (n_idx < n_cmp), dist_c.astype(F32), jnp.inf)
    dist_w = tq_like((nq, win_rows)) - (w0 + wstart + lax.broadcasted_iota(I32, (nq, win_rows), 1))
    distm_w = jnp.where((dist_w >= 0) & (dist_w < WINDOW), dist_w.astype(F32), jnp.inf)
    per_sel = SEL_BLOCK // cmp_blk
    pair = jnp.where(lax.broadcasted_iota(I32, (NC, NSP), 0) // per_sel ==
                     lax.broadcasted_iota(I32, (NC, NSP), 1), 1.0, 0.0).astype(BF16)
    blk = lax.broadcasted_iota(I32, (nq, NSP), 1)
    tqs = tq_like((nq, NSP))
    cur = tqs // SEL_BLOCK
    forced = (blk == 0) | (blk == cur) | (blk == cur - 1)
    sel_valid = blk * SEL_BLOCK <= tqs

    qgs = [(jnp.concatenate([q_ref[0, :, (g * hg + h) * 2 * dh:(g * hg + h + 1) * 2 * dh] for h in range(hg)],
                            axis=0) * qscale).astype(BF16) for g in groups]
    p_cmps, o_cmps = attend([kc_ref[0, :, col(g):col(g) + 2 * dh].astype(BF16) for g in groups],
                            [vc_ref[0, :, col(g):col(g) + 2 * dh].astype(BF16) for g in groups], distm_c, True)
    _, o_wins = attend([kw_ref[0, pl.ds(wstart, win_rows), col(g):col(g) + 2 * dh] for g in groups],
                       [vw_ref[0, pl.ds(wstart, win_rows), col(g):col(g) + 2 * dh] for g in groups], distm_w, False)
    imps = [_split3_dot(functools.reduce(jnp.add, p_cmps[g]), pair) for g in groups]
    scores = [jnp.where(blk < n_sel, jnp.where(forced, FORCE_SCORE, jnp.where(sel_valid, imp, -1.0)), -2.0)
              for imp in imps]

    selm = _top_blocks(jnp.concatenate(scores, axis=0), min(N_SEL, n_sel), NSP)
    selb = selm.astype(BF16)

    def sel_tile(kt, carry):
        k0 = kt * tk
        dist = tq_like((nq, tk)) - (k0 + lax.broadcasted_iota(I32, (nq, tk), 1))
        causal = dist >= 0
        distf = dist.astype(F32)
        expand = ex_ref[:, pl.ds(k0, tk)]
        new = []
        for g in range(kvh):
            m_run, l_run, acc = carry[g]
            c0 = col(g)
            s_t = lax.dot_general(qgs[g], ks_ref[0, pl.ds(k0, tk), c0:c0 + 2 * dh], _NT,
                                  preferred_element_type=F32)
            selx = jnp.dot(selb[g * nq:(g + 1) * nq], expand, preferred_element_type=F32)
            distm = jnp.where(causal & (selx > 0.5), distf, jnp.inf)
            m_new, l_new, alphas, ps = [], [], [], []
            for h in range(hg):
                rows = slice(h * nq, (h + 1) * nq)
                s2 = s_t[rows] - slopes2[g * hg + h] * distm
                m_h = jnp.maximum(m_run[rows], _row_max(s2))
                m_safe = jnp.where(m_h == -jnp.inf, 0.0, m_h)
                a_h = jnp.exp2(m_run[rows] - m_safe)
                p_h = jnp.exp2(s2 - m_safe)
                m_new.append(m_h)
                l_new.append(a_h * l_run[rows] + _row_sum(p_h))
                alphas.append(a_h)
                ps.append(p_h)
            pv = jnp.dot(jnp.concatenate(ps, axis=0).astype(BF16), vs_ref[0, pl.ds(k0, tk), c0:c0 + 2 * dh],
                         preferred_element_type=F32)
            new.append((jnp.concatenate(m_new, axis=0), jnp.concatenate(l_new, axis=0),
                        jnp.concatenate(alphas, axis=0) * acc + pv))
        return tuple(new)

    def sel_tile_group(kt, g):
        k0 = pl.multiple_of(kt * tk, tk)
        c0 = col(g)
        dist = tq_like((nq, tk)) - (k0 + lax.broadcasted_iota(I32, (nq, tk), 1))
        selx = jnp.dot(selb[g * nq:(g + 1) * nq], ex_ref[:, pl.ds(k0, tk)], preferred_element_type=F32)
        distm = jnp.where((dist >= 0) & (selx > 0.5), dist.astype(F32), jnp.inf)
        s_t = lax.dot_general(qgs[g], ks_ref[0, pl.ds(k0, tk), c0:c0 + 2 * dh], _NT,
                              preferred_element_type=F32)
        m_run = m_scr[g]
        l_run = l_scr[g]
        heads = range(hg)
        rows = [slice(h * nq, (h + 1) * nq) for h in heads]
        s2s = [s_t[rows[h]] - slopes2[g * hg + h] * distm for h in heads]
        tmax = [_row_max(x) for x in s2s]
        m_new = [jnp.maximum(m_run[rows[h]], tmax[h]) for h in heads]
        m_safe = [jnp.where(m == -jnp.inf, 0.0, m) for m in m_new]
        alphas = [jnp.exp2(m_run[rows[h]] - m_safe[h]) for h in heads]
        ps = [jnp.exp2(s2s[h] - jnp.concatenate([m_safe[h]] * (tk // LANES), axis=1)) for h in heads]
        l_new = [alphas[h] * l_run[rows[h]] + _row_sum(ps[h]) for h in heads]
        pv = jnp.dot(jnp.concatenate(ps, axis=0).astype(BF16), vs_ref[0, pl.ds(k0, tk), c0:c0 + 2 * dh],
                     preferred_element_type=F32)
        m_scr[g] = jnp.concatenate(m_new, axis=0)
        l_scr[g] = jnp.concatenate(l_new, axis=0)
        acc_scr[g] = jnp.concatenate(alphas, axis=0) * acc_scr[g] + pv

    if causal_tiles:
        n_flag = Lk // tk
        blocks_per_tile = tk // SEL_BLOCK
        lane_b = lax.broadcasted_iota(I32, (1, NSP), 1)
        for g in range(kvh):
            any_q = jnp.max(selm[g * nq:(g + 1) * nq], axis=0, keepdims=True)
            for kt in range(n_flag):
                in_tile = (lane_b >= kt * blocks_per_tile) & (lane_b < (kt + 1) * blocks_per_tile)
                flag_ref[g * n_flag + kt] = (jnp.max(jnp.where(in_tile, any_q, 0.0)) > 0.5).astype(I32)
        m_scr[...] = jnp.full(m_scr.shape, -jnp.inf, F32)
        l_scr[...] = jnp.zeros(l_scr.shape, F32)
        acc_scr[...] = jnp.zeros(acc_scr.shape, F32)

        def visit(kt, carry):
            for g in range(kvh):
                @pl.when(flag_ref[g * n_flag + kt] > 0)
                def _():
                    sel_tile_group(kt, g)
            return carry

        lax.fori_loop(0, (q0 + nq + tk - 1) // tk, visit, 0)
        fin = tuple((None, l_scr[g], acc_scr[g]) for g in range(kvh))
    else:
        fin = tuple((jnp.full((R, 1), -jnp.inf, F32), jnp.zeros((R, 1), F32), jnp.zeros((R, 2 * dh), F32))
                    for _ in range(kvh))
        for kt in range(Lk // tk):
            fin = sel_tile(kt, fin)

    lane = lax.broadcasted_iota(I32, (nq, 2 * dh), 1)
    for g in range(kvh):
        lo = (g % 2) * dh
        keep = (lane >= lo) & (lane < lo + dh)
        _, l_fin, acc = fin[g]
        o_sel = acc / jnp.maximum(l_fin, 1e-30)
        for h in range(hg):
            hd = g * hg + h
            rows = slice(h * nq, (h + 1) * nq)
            o = (gates[:, 3 * hd:3 * hd + 1] * o_cmps[g][rows] + gates[:, 3 * hd + 1:3 * hd + 2] * o_sel[rows] +
                 gates[:, 3 * hd + 2:3 * hd + 3] * o_wins[g][rows])
            o_ref[0, :, hd * 2 * dh:(hd + 1) * 2 * dh] = jnp.where(keep, o, 0.0)


def _nsa_attend(q, gt, kc, vc, ks, vs, kw, vw, cols, *, nq, t0, tk, n_cmp, cmp_blk, n_sel, kvh, dh,
                win_rows, w0, causal_tiles, seqs=1):
    B, Lq, QW = q.shape
    H = QW // (2 * dh)
    hg = H // kvh
    cw = kvh * dh
    Lk = ks.shape[1]
    NC = kc.shape[1]
    nsp = LANES
    assert n_sel <= nsp and Lk % tk == 0 and Lq % nq == 0 and dh * 2 == LANES
    expand = (lax.broadcasted_iota(I32, (nsp, Lk), 1) // SEL_BLOCK ==
              lax.broadcasted_iota(I32, (nsp, Lk), 0)).astype(BF16)
    slopes = tuple(2.0 ** (-8.0 * (h + 1) / H) for h in range(H))
    assert B % seqs == 0
    cfg = dict(seqs=seqs, nq=nq, t0=t0, tk=tk, n_cmp=n_cmp, cmp_blk=cmp_blk, n_sel=n_sel, kvh=kvh, hg=hg, dh=dh,
               slopes=slopes, win_rows=win_rows, w0=w0)
    R = hg * nq
    if causal_tiles:
        kern = functools.partial(_nsa_multi_kernel, causal_tiles=True, **cfg)
        scratch = [pltpu.VMEM((kvh, R, LANES), F32), pltpu.VMEM((kvh, R, LANES), F32),
                   pltpu.VMEM((kvh, R, 2 * dh), F32), pltpu.SMEM((kvh * (Lk // tk),), I32)]
    else:
        assert win_rows == kw.shape[1]
        kern = functools.partial(_nsa_staged_kernel, **cfg)
        scratch = []
    qmap = lambda b, i: (b, i, 0)
    kv_spec = lambda arr, c: pl.BlockSpec((seqs, arr.shape[1], cw), lambda b, i, c=c: (b, 0, c))
    return pl.pallas_call(
        kern, grid=(B // seqs, Lq // nq),
        in_specs=[pl.BlockSpec((seqs, nq, QW), qmap), pl.BlockSpec((seqs, nq, LANES), qmap),
                  kv_spec(kc, cols[0]), kv_spec(vc, cols[1]), kv_spec(ks, cols[2]), kv_spec(vs, cols[3]),
                  kv_spec(kw, cols[4]), kv_spec(vw, cols[5]),
                  pl.BlockSpec((nsp, Lk), lambda b, i: (0, 0))],
        out_specs=pl.BlockSpec((seqs, nq, QW), qmap),
        out_shape=jax.ShapeDtypeStruct((B, Lq, QW), F32),
        scratch_shapes=scratch,
        compiler_params=_cparams("parallel", "arbitrary"), name="nsa_attend")(q, gt, kc, vc, ks, vs, kw, vw, expand)


def _head_tiles(w, H, kvh, dh, axis):
    hg = H // kvh
    parts = []
    for h in range(H):
        sl = [slice(None)] * w.ndim
        sl[axis] = slice(h * dh, (h + 1) * dh)
        piece = w[tuple(sl)]
        zero = jnp.zeros_like(piece)
        parts += [zero, piece] if ((h // hg) % 2) else [piece, zero]
    return jnp.concatenate(parts, axis=axis)


def kernel(x_prompt, x_sample, cache_kv_pages, cache_win, state_ssm, state_conv, page_table, ln_g, ln_b, m_w_in, m_conv_w, m_conv_b, m_dt_bias, m_a_log, m_d_skip, m_norm_w, m_w_out, w_kv_shared, w_cmp, pos_cmp, nsa_w_in, nsa_w_out, peer_w_q, peer_sub_keys, peer_u, peer_v):
    Bp, Lp, D = x_prompt.shape
    Bs, Ls, _ = x_sample.shape
    depth = ln_g.shape[0]
    n_a = m_w_in.shape[0]
    alpha = (2 * depth) ** DEPTH_ALPHA_POW
    Tp, Ts = Bp * Lp, Bs * Ls
    assert Tp % LANES == 0
    Tsp = -(-Ts // LANES) * LANES
    Xp = x_prompt.reshape(Tp, D)
    Xs = jnp.pad(x_sample.reshape(Ts, D), ((0, Tsp - Ts), (0, 0)))
    dims = (Bp, Lp, Bs, Ls)

    NP, PS, n_rows, kvh, dh = cache_kv_pages.shape
    cw = kvh * dh
    n_pages = page_table.shape[1]
    past_len = n_pages * PS
    win_cache = cache_win.shape[1]
    cmp_blk = w_cmp.shape[1]
    H = nsa_w_in.shape[2] // (dh + 3)

    ssm_p, ssm_s, conv_p, conv_s = [], [], [], []
    for i in range(depth):
        if i < n_a:
            mp = (m_w_in[i], m_conv_w[i], m_conv_b[i], m_dt_bias[i], m_a_log[i], m_d_skip[i], m_norm_w[i], m_w_out[i])
            Xp, Xs, hp, hs, cp, cs = _mamba_layer(Xp, Xs, dims, state_conv[i], state_ssm[i], mp,
                                                  ln_g[i, 0], ln_b[i, 0], alpha)
            ssm_p.append(hp)
            ssm_s.append(hs)
            conv_p.append(cp)
            conv_s.append(cs)
        else:
            if i == n_a:
                wkv = w_kv_shared.astype(BF16)
                kv_p = _matmul(Xp, wkv).reshape(Bp, Lp, 6 * cw)
                kv_s = _matmul(Xs, wkv)[:Ts].reshape(Bs, Ls, 6 * cw)
                kvb_p = kv_p.astype(BF16)
                n_sel_p = -(-Lp // SEL_BLOCK)
                Lpp = n_sel_p * SEL_BLOCK
                padp = lambda v: jnp.pad(v, ((0, 0), (0, Lpp - Lp), (0, 0)))
                nc_p = -(-(Lpp // cmp_blk) // LANES) * LANES
                kc_p = _compress(padp(kv_p[:, :, 0:cw]), w_cmp[0], pos_cmp[0], nc_p)
                vc_p = _compress(padp(kv_p[:, :, cw:2 * cw]), w_cmp[1], pos_cmp[1], nc_p)
                tk_p = _pick(Lp, (512, 256, 128))
                new_rows = jnp.pad(kv_s[:, :, :4 * cw], ((0, 0), (0, SUBLANES - Ls), (0, 0)))
                kcb_s, vcb_s, ks_s, vs_s = _gather_pages(cache_kv_pages.reshape(NP, PS, n_rows * cw), page_table,
                                                         new_rows, cmp_blk)
                Lk_s = ks_s.shape[1]
                n_sel_s = -(-(past_len + Ls) // SEL_BLOCK)
                nc_s = -(-(Lk_s // cmp_blk) // LANES) * LANES
                kc_s = _compress_blocks(kcb_s, w_cmp[0], pos_cmp[0], nc_s)
                vc_s = _compress_blocks(vcb_s, w_cmp[1], pos_cmp[1], nc_s)
                win_full_s = jnp.concatenate([cache_win, kv_s[:, :, 4 * cw:].reshape(Bs, Ls, 2, kvh, dh)], axis=1)
                Lw = win_cache + Ls
                Lwp = -(-Lw // LANES) * LANES
                wflat = jnp.pad(win_full_s.reshape(Bs, Lw, 2 * cw), ((0, 0), (0, Lwp - Lw), (0, 0))).astype(BF16)
            j = i - n_a
            wq = _head_tiles(nsa_w_in[j][:, :H * dh], H, kvh, dh, axis=1).astype(BF16)
            wg = _pad_cols(nsa_w_in[j][:, H * dh:], LANES).astype(BF16)
            wo = _head_tiles(nsa_w_out[j], H, kvh, dh, axis=0).astype(BF16)
            QW = wq.shape[1]
            o_p = _nsa_attend(_matmul(Xp, wq).reshape(Bp, Lp, QW), _matmul(Xp, wg).reshape(Bp, Lp, LANES),
                              kc_p, vc_p, kvb_p, kvb_p, kvb_p, kvb_p, (0, 0, 2, 3, 4, 5),
                              nq=Q_BLOCK, t0=0, tk=tk_p, n_cmp=Lpp // cmp_blk, cmp_blk=cmp_blk, n_sel=n_sel_p,
                              kvh=kvh, dh=dh, win_rows=min(WINDOW + Q_BLOCK, Lp), w0=0, causal_tiles=True)
            Xp = _matmul_ln(o_p.reshape(Tp, QW), wo, Xp, ln_g[i, 0], ln_b[i, 0], alpha)
            pad_q = lambda v: jnp.pad(v[:Ts].reshape(Bs, Ls, -1), ((0, 0), (0, SUBLANES - Ls), (0, 0)))
            o_s = _nsa_attend(pad_q(_matmul(Xs, wq)), pad_q(_matmul(Xs, wg)), kc_s, vc_s, ks_s, vs_s, wflat, wflat,
                              (0, 0, 0, 0, 0, 1),
                              nq=SUBLANES, t0=past_len, tk=Lk_s, n_cmp=-(-(past_len + Ls) // SEL_BLOCK) * (SEL_BLOCK // cmp_blk),
                              cmp_blk=cmp_blk, n_sel=n_sel_s, kvh=kvh, dh=dh, win_rows=Lwp,
                              w0=past_len - win_cache, causal_tiles=False, seqs=_pick(Bs, (4, 2, 1)))
            o_s = jnp.pad(o_s[:, :Ls].reshape(Ts, QW), ((0, Tsp - Ts), (0, 0)))
            Xs = _matmul_ln(o_s, wo, Xs, ln_g[i, 0], ln_b[i, 0], alpha)
        peer = (peer_w_q[i].astype(BF16), peer_sub_keys[i], peer_u[i].astype(BF16), peer_v[i].astype(BF16),
                ln_g[i, 1], ln_b[i, 1], alpha)
        Xp = _peer_layer(Xp, *peer)
        Xs = _peer_layer(Xs, *peer)

    rows_p = kv_p[:, :, :4 * cw].reshape(Bp, Lp, n_rows, kvh, dh)
    rows_s = kv_s[:, :, :4 * cw].reshape(Bs, Ls, n_rows, kvh, dh)
    win_p = kv_p[:, :, 4 * cw:].reshape(Bp, Lp, 2, kvh, dh)[:, -min(WINDOW, Lp):]
    return (Xp.reshape(Bp, Lp, D), Xs[:Ts].reshape(Bs, Ls, D), rows_p, rows_s, win_p,
            win_full_s[:, -win_cache:], jnp.stack(ssm_p), jnp.stack(ssm_s), jnp.stack(conv_p), jnp.stack(conv_s))
```

```python
import functools
import math

import jax
import jax.numpy as jnp
from jax import lax
from jax.experimental import pallas as pl
from jax.experimental.pallas import tpu as pltpu

F32 = jnp.float32
BF16 = jnp.bfloat16
I32 = jnp.int32

DEPTH_ALPHA_POW = 0.25
LN_EPS = 1e-5
SEL_BLOCK = 64
N_SEL = 16
WINDOW = 512
FORCE_SCORE = 1.0e4
PEER_TOPK = 16
SSD_CHUNK = 128
Q_BLOCK = 128
GELU_C0 = math.sqrt(2.0 / math.pi)
GELU_C1 = 0.044715 * GELU_C0
LOG2E = 1.0 / math.log(2.0)
W_TOKENS_PER_ITER = 16
MAX_FULL_N = 3072

LANES = 128
SUBLANES = 8
VMEM_LIMIT_BYTES = 56 * 1024 * 1024

_NT = (((1,), (1,)), ((), ()))
_TN = (((0,), (0,)), ((), ()))
_HI = lax.Precision.HIGHEST


def _cparams(*sem):
    return pltpu.CompilerParams(dimension_semantics=sem, vmem_limit_bytes=VMEM_LIMIT_BYTES)


def _pick(n, cands):
    for c in cands:
        if n % c == 0:
            return c
    raise ValueError(f"no tile in {cands} divides {n}")


def _pad_cols(w, n):
    return jnp.pad(w, ((0, 0), (0, n - w.shape[1])))


def _mm_kernel(x_ref, w_ref, o_ref):
    o_ref[...] = jnp.dot(x_ref[...].astype(BF16), w_ref[...],
                         preferred_element_type=F32).astype(o_ref.dtype)


def _mm_bias_kernel(x_ref, xb_ref, w_ref, o_ref):
    x = (x_ref[...] + xb_ref[...]).astype(BF16)
    o_ref[...] = jnp.dot(x, w_ref[...], preferred_element_type=F32).astype(o_ref.dtype)


def _matmul(x, w, *, xbias=None, out_dtype=F32, tm_cap=512):
    M, K = x.shape
    N = w.shape[1]
    tm = _pick(M, tuple(t for t in (512, 256, 128, 64, 32, 16, 8) if t <= tm_cap))
    tn = N if N <= MAX_FULL_N else _pick(N, (1024, 512, 256, 128))
    x_spec = pl.BlockSpec((tm, K), lambda j, i: (i, 0))
    w_spec = pl.BlockSpec((K, tn), lambda j, i: (0, j))
    o_spec = pl.BlockSpec((tm, tn), lambda j, i: (i, j))
    if xbias is None:
        body, specs, args = _mm_kernel, [x_spec, w_spec], (x, w)
    else:
        b_spec = pl.BlockSpec((1, K), lambda j, i: (0, 0))
        body, specs, args = _mm_bias_kernel, [x_spec, b_spec, w_spec], (x, xbias, w)
    return pl.pallas_call(
        body, grid=(N // tn, M // tm), in_specs=specs, out_specs=o_spec,
        out_shape=jax.ShapeDtypeStruct((M, N), out_dtype),
        compiler_params=_cparams("parallel", "parallel"), name="proj")(*args)


def _deepnorm_ln(v, g, b):
    mu = jnp.mean(v, axis=-1, keepdims=True)
    d = v - mu
    var = jnp.mean(d * d, axis=-1, keepdims=True)
    return d * lax.rsqrt(var + LN_EPS) * g + b


def _mm_ln_kernel(x_ref, w_ref, r_ref, g_ref, b_ref, o_ref, *, alpha):
    y = jnp.dot(x_ref[...].astype(BF16), w_ref[...], preferred_element_type=F32)
    o_ref[...] = _deepnorm_ln(alpha * r_ref[...] + y, g_ref[...], b_ref[...])


def _matmul_ln(x, w, resid, g, b, alpha):
    M, K = x.shape
    D = w.shape[1]
    tm = _pick(M, (512, 256, 128, 64, 32, 16, 8))
    row = lambda i: (i, 0)
    fixed = lambda i: (0, 0)
    return pl.pallas_call(
        functools.partial(_mm_ln_kernel, alpha=alpha), grid=(M // tm,),
        in_specs=[pl.BlockSpec((tm, K), row), pl.BlockSpec((K, D), fixed),
                  pl.BlockSpec((tm, D), row), pl.BlockSpec((1, D), fixed), pl.BlockSpec((1, D), fixed)],
        out_specs=pl.BlockSpec((tm, D), row),
        out_shape=jax.ShapeDtypeStruct((M, D), F32),
        compiler_params=_cparams("parallel"), name="proj_ln")(x, w, resid, g.reshape(1, D), b.reshape(1, D))


def _softplus(x):
    return jnp.maximum(x, 0.0) + jnp.log1p(jnp.exp(-jnp.abs(x)))


def _silu(x):
    return x * jax.nn.sigmoid(x)


def _ssd_kernel(z_ref, xbc_ref, dtr_ref, cbuf_ref, h0_ref, cw_ref, cb_ref, dtb_ref, alog_ref,
                dsk_ref, nw_ref, y_ref, hn_ref, xf_scr, h_scr, *, Lc, H, P, N, G, n_valid, conv_w):
    c = pl.program_id(1)
    DI = H * P
    HG = H // G
    GP = HG * P
    tail = xf_scr.shape[0] - Lc

    @pl.when(c == 0)
    def _():
        xf_scr[0:tail, :] = cbuf_ref[0]
        h_scr[...] = h0_ref[0]

    xf_scr[tail:tail + Lc, :] = xbc_ref[0]
    conv = cb_ref[...]
    for k in range(conv_w):
        conv = conv + xf_scr[pl.ds(tail - (conv_w - 1) + k, Lc), :] * cw_ref[k:k + 1, :]
    conv = _silu(conv)
    xf_scr[0:tail, :] = xf_scr[pl.ds(Lc, tail), :]

    xs = conv[:, :DI]
    Bm = conv[:, DI:DI + G * N]
    Cm = conv[:, DI + G * N:]

    lane = lax.broadcasted_iota(I32, (Lc, LANES), 1)
    rowi = lax.broadcasted_iota(I32, (Lc, LANES), 0)
    dt = jnp.where((lane < H) & (rowi < n_valid), _softplus(dtr_ref[0] + dtb_ref[...]), 0.0)
    a = dt * (-jnp.exp(alog_ref[...]))
    ti = lax.broadcasted_iota(I32, (Lc, Lc), 0)
    si = lax.broadcasted_iota(I32, (Lc, Lc), 1)
    causal = ti >= si
    tril = jnp.where(causal, 1.0, 0.0).astype(F32)
    eye = jnp.where(lax.broadcasted_iota(I32, (LANES, LANES), 0) ==
                    lax.broadcasted_iota(I32, (LANES, LANES), 1), 1.0, 0.0).astype(F32)
    acum = jnp.dot(tril, a, precision=_HI, preferred_element_type=F32)
    acum_t = lax.dot_general(eye, acum, _NT, precision=_HI, preferred_element_type=F32)
    dt_t = lax.dot_general(eye, dt, _NT, precision=_HI, preferred_element_type=F32)
    alast = acum[Lc - 1:Lc, :]
    wend = jnp.exp(alast - acum) * dt
    ea = jnp.exp(acum)
    edec = jnp.exp(alast)
    dsk = dsk_ref[...]
    first_half = lax.broadcasted_iota(I32, (Lc, 2 * P), 1) < P

    def pair(v, h):
        rows = v.shape[0]
        return jnp.where(first_half[:rows], jnp.broadcast_to(v[:, h:h + 1], (rows, 2 * P)),
                         jnp.broadcast_to(v[:, h + 1:h + 2], (rows, 2 * P)))

    y_parts = []
    for g in range(G):
        Bg = Bm[:, g * N:(g + 1) * N].astype(BF16)
        Cg = Cm[:, g * N:(g + 1) * N].astype(BF16)
        cb = lax.dot_general(Cg, Bg, _NT, preferred_element_type=F32)
        Sg = h_scr[g * GP:(g + 1) * GP, :]
        yoff = lax.dot_general(Cg, Sg.astype(BF16), _NT, preferred_element_type=F32)
        xw_parts = []
        for pr in range(HG // 2):
            h = g * HG + 2 * pr
            xs_pair = xs[:, h * P:(h + 2) * P]
            xs_pair_b = xs_pair.astype(BF16)
            outs = []
            for hh in (h, h + 1):
                seg = acum[:, hh:hh + 1] - acum_t[hh:hh + 1, :]
                dec = jnp.exp(jnp.where(causal, seg, -jnp.inf))
                mix = cb * dec * dt_t[hh:hh + 1, :]
                outs.append(jnp.dot(mix.astype(BF16), xs_pair_b, preferred_element_type=F32))
            ydiag = jnp.where(first_half, outs[0], outs[1])
            y_pair = ydiag + yoff[:, pr * 2 * P:(pr + 1) * 2 * P] * pair(ea, h) + pair(dsk, h) * xs_pair
            y_parts.append(y_pair)
            xw_parts.append(xs_pair * pair(wend, h))
        xw = jnp.concatenate(xw_parts, axis=1).astype(BF16)
        states = lax.dot_general(xw, Bg, _TN, preferred_element_type=F32)
        for hl in range(HG):
            hh = g * HG + hl
            r0 = g * GP + hl * P
            h_scr[r0:r0 + P, :] = edec[:, hh:hh + 1] * Sg[hl * P:(hl + 1) * P, :] + states[hl * P:(hl + 1) * P, :]

    z = z_ref[0]
    gn = DI // G
    outs = []
    for g in range(G):
        parts = y_parts[g * (HG // 2):(g + 1) * (HG // 2)]
        yg = jnp.concatenate(parts, axis=1) if len(parts) > 1 else parts[0]
        yg = yg * _silu(z[:, g * gn:(g + 1) * gn])
        ms = jnp.mean(yg * yg, axis=-1, keepdims=True)
        outs.append(yg * lax.rsqrt(ms + 1e-5))
    y_ref[0] = jnp.concatenate(outs, axis=1) * nw_ref[...]

    @pl.when(c == pl.num_programs(1) - 1)
    def _():
        hn_ref[0] = h_scr[...]


def _ssd(z, xbc, dtr, cbuf, h0, conv_w, conv_b, dt_bias, a_log, d_skip, norm_w, *, Lc, n_valid, G):
    B, L, DI = z.shape
    CD = xbc.shape[-1]
    _, H, P, N = h0.shape
    KW = conv_w.shape[0]
    assert P * 2 == LANES and (H // G) % 2 == 0 and H <= LANES and L % Lc == 0
    padl = lambda v: jnp.pad(v.astype(F32).reshape(1, -1), ((0, 0), (0, LANES - H)))
    seq = lambda b, c: (b, c, 0)
    per_b = lambda b, c: (b, 0, 0)
    fixed = lambda b, c: (0, 0)
    kern = functools.partial(_ssd_kernel, Lc=Lc, H=H, P=P, N=N, G=G, n_valid=n_valid, conv_w=KW)
    y, hn = pl.pallas_call(
        kern, grid=(B, L // Lc),
        in_specs=[pl.BlockSpec((1, Lc, DI), seq), pl.BlockSpec((1, Lc, CD), seq),
                  pl.BlockSpec((1, Lc, LANES), seq), pl.BlockSpec((1, SUBLANES, CD), per_b),
                  pl.BlockSpec((1, H * P, N), per_b),
                  pl.BlockSpec((KW, CD), fixed), pl.BlockSpec((1, CD), fixed),
                  pl.BlockSpec((1, LANES), fixed), pl.BlockSpec((1, LANES), fixed),
                  pl.BlockSpec((1, LANES), fixed), pl.BlockSpec((1, DI), fixed)],
        out_specs=[pl.BlockSpec((1, Lc, DI), seq), pl.BlockSpec((1, H * P, N), per_b)],
        out_shape=[jax.ShapeDtypeStruct((B, L, DI), F32), jax.ShapeDtypeStruct((B, H * P, N), F32)],
        scratch_shapes=[pltpu.VMEM((SUBLANES + Lc, CD), F32), pltpu.VMEM((H * P, N), F32)],
        compiler_params=_cparams("parallel", "arbitrary"), name="ssd",
    )(z, xbc, dtr, cbuf, h0.reshape(B, H * P, N), conv_w, conv_b.reshape(1, CD),
      padl(dt_bias), padl(a_log), padl(d_skip), norm_w.reshape(1, DI))
    return y, hn.reshape(B, H, P, N)


def _mamba_layer(Xp, Xs, dims, state_conv_i, state_ssm_i, params, ln_g, ln_b, alpha):
    Bp, Lp, Bs, Ls = dims
    w_in, conv_w, conv_b, dt_bias, a_log, d_skip, norm_w, w_out = params
    Tp = Bp * Lp
    DI = norm_w.shape[0]
    CD = conv_w.shape[1]
    _, H, P, N = state_ssm_i.shape
    G = (CD - DI) // (2 * N)
    KW = conv_w.shape[0]
    Ts = Bs * Ls
    wz = w_in[:, :DI].astype(BF16)
    wx = w_in[:, DI:DI + CD].astype(BF16)
    wd = _pad_cols(w_in[:, DI + CD:], LANES).astype(BF16)
    wo = w_out.astype(BF16)
    args = (conv_w, conv_b, dt_bias, a_log, d_skip, norm_w)

    xbc_p = _matmul(Xp, wx).reshape(Bp, Lp, CD)
    Lc = SSD_CHUNK if Lp % SSD_CHUNK == 0 else Lp
    yp, hp = _ssd(_matmul(Xp, wz).reshape(Bp, Lp, DI), xbc_p, _matmul(Xp, wd).reshape(Bp, Lp, LANES),
                  jnp.zeros((Bp, SUBLANES, CD), F32), jnp.zeros((Bp, H, P, N), F32), *args,
                  Lc=Lc, n_valid=Lc, G=G)
    conv_p = xbc_p[:, Lp - (KW - 1):]
    Xp = _matmul_ln(yp.reshape(Tp, DI), wo, Xp, ln_g, ln_b, alpha)

    Lsp = -(-Ls // SUBLANES) * SUBLANES
    pad_rows = lambda v: jnp.pad(v[:Ts].reshape(Bs, Ls, -1), ((0, 0), (0, Lsp - Ls), (0, 0)))
    xbc_s = _matmul(Xs, wx)[:Ts].reshape(Bs, Ls, CD)
    cbuf_s = jnp.pad(state_conv_i, ((0, 0), (SUBLANES - (KW - 1), 0), (0, 0)))
    ys, hs = _ssd(pad_rows(_matmul(Xs, wz)), pad_rows(xbc_s), pad_rows(_matmul(Xs, wd)), cbuf_s, state_ssm_i,
                  *args, Lc=Lsp, n_valid=Ls, G=G)
    conv_s = jnp.concatenate([state_conv_i, xbc_s], axis=1)[:, -(KW - 1):]
    ys = jnp.pad(ys[:, :Ls].reshape(Ts, DI), ((0, Xs.shape[0] - Ts), (0, 0)))
    Xs = _matmul_ln(ys, wo, Xs, ln_g, ln_b, alpha)
    return Xp, Xs, hp, hs, conv_p, conv_s


def _topk_rows(s, k, between_rounds=lambda: None):
    n = float(s.shape[0])
    rows = lax.broadcasted_iota(I32, s.shape, 0).astype(F32)
    vals, idxs = [], []
    for _ in range(k):
        m = jnp.max(s, axis=0, keepdims=True)
        i = jnp.min(jnp.where(s == m, rows, n), axis=0, keepdims=True)
        vals.append(m)
        idxs.append(i)
        s = jnp.where(rows == i, -jnp.inf, s)
        between_rounds()
    return vals, idxs


def _peer_route_kernel(q_ref, k_ref, w_ref, i1_scr, i2_scr, g_scr, *, heads, half, topk, n_keys, nj):
    tb = q_ref.shape[0]

    @pl.when(pl.program_id(0) == 0)
    def _():
        i1_scr[...] = jnp.zeros(i1_scr.shape, I32)
        i2_scr[...] = jnp.zeros(i2_scr.shape, I32)
        g_scr[...] = jnp.zeros(g_scr.shape, F32)

    sub = lax.broadcasted_iota(I32, (n_keys, i1_scr.shape[1]), 0)
    pending = list(range(tb))
    n_rounds = heads * 3 * topk
    every = max(1, n_rounds // tb)
    calls = [0]

    def w_token():
        t = pending.pop(0)
        onehot1 = jnp.where(sub == i1_scr[t:t + 1, :], 1.0, 0.0).astype(BF16)
        gated2 = jnp.where(sub == i2_scr[t:t + 1, :], g_scr[t:t + 1, :], 0.0).astype(BF16)
        w_t = lax.dot_general(onehot1, gated2, _NT, preferred_element_type=F32)
        for k in range(n_keys // nj):
            w_ref[0, k, t * nj:(t + 1) * nj, :] = w_t[k * nj:(k + 1) * nj, :]

    def between_rounds():
        calls[0] += 1
        if pending and calls[0] % every == 0:
            w_token()

    i1, i2, gate = _peer_topk_block(q_ref, k_ref, heads=heads, half=half, topk=topk, n_keys=n_keys,
                                    between_rounds=between_rounds)
    while pending:
        w_token()
    i1_scr[...] = i1
    i2_scr[...] = i2
    g_scr[...] = gate


def _peer_topk_block(q_ref, k_ref, *, heads, half, topk, n_keys, between_rounds):
    k1 = k_ref[0]
    k2 = k_ref[1]
    tb = q_ref.shape[0]
    cand_ab = [(a, b) for a in range(topk) for b in range(topk // (a + 1))]
    n_pad = -len(cand_ab) % SUBLANES
    codes, gates = [], []
    for h in range(heads):
        q1 = q_ref[:, (2 * h) * half:(2 * h + 1) * half]
        q2 = q_ref[:, (2 * h + 1) * half:(2 * h + 2) * half]
        s1 = lax.dot_general(k1, q1, _NT, preferred_element_type=F32)
        s2 = lax.dot_general(k2, q2, _NT, preferred_element_type=F32)
        v1, i1 = _topk_rows(s1, topk, between_rounds)
        v2, i2 = _topk_rows(s2, topk, between_rounds)
        cand = jnp.concatenate([v1[a] + v2[b] for a, b in cand_ab] +
                               [jnp.full((n_pad, tb), -jnp.inf, F32)], axis=0)
        code = jnp.concatenate([i1[a] * float(n_keys) + i2[b] for a, b in cand_ab] +
                               [jnp.zeros((n_pad, tb), F32)], axis=0)
        sc, pos = _topk_rows(cand, topk, between_rounds)
        crow = lax.broadcasted_iota(I32, cand.shape, 0).astype(F32)
        codes += [jnp.max(jnp.where(crow == p, code, -1.0), axis=0, keepdims=True) for p in pos]
        ex = jnp.exp(jnp.concatenate(sc, axis=0) - sc[0])
        gates.append(ex / jnp.sum(ex, axis=0, keepdims=True))
    code_t = jnp.concatenate(codes, axis=0).T.astype(I32)
    return code_t // n_keys, code_t % n_keys, jnp.concatenate(gates, axis=0).T


def _gelu_tanh(x):
    hx = 0.5 * x
    inner = x * (GELU_C0 + GELU_C1 * (x * x))
    return hx * jnp.tanh(inner) + hx


def _peer_dense_kernel(x_ref, u_ref, v_ref, w_ref, g_ref, b_ref, o_ref, acc_ref, xb_ref, *, alpha, nj):
    k = pl.program_id(1)
    td = x_ref.shape[0]

    @pl.when(k == 0)
    def _():
        acc_ref[...] = jnp.zeros_like(acc_ref)
        xb_ref[...] = x_ref[...].astype(BF16)

    act = lax.dot_general(xb_ref[...], u_ref[...], _NT, preferred_element_type=F32)
    w = jnp.concatenate([w_ref[0, 0, pl.ds(j, td, stride=nj), :] for j in range(nj)], axis=1)
    coef = (_gelu_tanh(act) * w).astype(BF16)
    acc_ref[...] += jnp.dot(coef, v_ref[...], preferred_element_type=F32)

    @pl.when(k == pl.num_programs(1) - 1)
    def _():
        o_ref[...] = _deepnorm_ln(alpha * x_ref[...] + acc_ref[...], g_ref[...], b_ref[...])


def _peer_layer(X, w_q, sub_keys, u_tab, v_tab, ln_g, ln_b, alpha):
    T, D = X.shape
    _, n_keys, half = sub_keys.shape
    heads = w_q.shape[1] // (2 * half)
    nsel = heads * PEER_TOPK
    assert n_keys == LANES and nsel == LANES and half % LANES == 0
    q = _matmul(X, w_q)

    tb = LANES
    td = _pick(T, (1024, 768, 512, 256, 128))
    nj = SUBLANES
    et = nj * n_keys
    sub_blocks = td // tb
    n_blocks = T // tb
    prev = lambda i: jnp.maximum(i - 1, 0)
    w = pl.pallas_call(
        functools.partial(_peer_route_kernel, heads=heads, half=half, topk=PEER_TOPK, n_keys=n_keys, nj=nj),
        grid=(n_blocks + 1,),
        in_specs=[pl.BlockSpec((tb, q.shape[1]), lambda i: (jnp.minimum(i, n_blocks - 1), 0)),
                  pl.BlockSpec((2, n_keys, half), lambda i: (0, 0, 0))],
        out_specs=pl.BlockSpec((1, n_keys // nj, tb * nj, n_keys),
                               lambda i: (prev(i) // sub_blocks, 0, prev(i) % sub_blocks, 0)),
        out_shape=jax.ShapeDtypeStruct((T // td, n_keys // nj, td * nj, n_keys), F32),
        scratch_shapes=[pltpu.VMEM((tb, nsel), I32), pltpu.VMEM((tb, nsel), I32), pltpu.VMEM((tb, nsel), F32)],
        compiler_params=_cparams("arbitrary"), name="peer_route")(q, sub_keys)

    return pl.pallas_call(
        functools.partial(_peer_dense_kernel, alpha=alpha, nj=nj),
        grid=(T // td, n_keys // nj),
        in_specs=[pl.BlockSpec((td, D), lambda i, k: (i, 0)),
                  pl.BlockSpec((et, D), lambda i, k: (k, 0)),
                  pl.BlockSpec((et, D), lambda i, k: (k, 0)),
                  pl.BlockSpec((1, 1, td * nj, n_keys), lambda i, k: (i, k, 0, 0)),
                  pl.BlockSpec((1, D), lambda i, k: (0, 0)), pl.BlockSpec((1, D), lambda i, k: (0, 0))],
        out_specs=pl.BlockSpec((td, D), lambda i, k: (i, 0)),
        out_shape=jax.ShapeDtypeStruct((T, D), F32),
        scratch_shapes=[pltpu.VMEM((td, D), F32), pltpu.VMEM((td, D), BF16)],
        compiler_params=_cparams("parallel", "arbitrary"), name="peer_dense",
    )(X, u_tab, v_tab, w, ln_g.reshape(1, D), ln_b.reshape(1, D))


def _gather_kernel(pt_ref, pa_ref, pb_ref, new_ref, kc_ref, vc_ref, ks_ref, vs_ref, col_scr, *, n_steps, cw, blk):
    del pt_ref
    p = pl.program_id(1)
    ps = pa_ref.shape[1]
    per_page = ps // blk

    @pl.when(p < n_steps)
    def _():
        for j, page in enumerate((pa_ref, pb_ref)):
            n_tiles = 2 * cw // LANES
            for c in range(n_tiles):
                col_scr[j * n_tiles + c] = page[0, :, c * LANES:(c + 1) * LANES]
            for r, ref in enumerate((kc_ref, vc_ref)):
                for l in range(blk):
                    for c in range(cw // LANES):
                        ref[0, j * per_page:(j + 1) * per_page, l * cw + c * LANES:l * cw + (c + 1) * LANES] = \
                            col_scr[j * n_tiles + r * (cw // LANES) + c, pl.ds(l, per_page, stride=blk), :]
            ks_ref[0, j * ps:(j + 1) * ps, :] = page[0, :, 2 * cw:3 * cw].astype(BF16)
            vs_ref[0, j * ps:(j + 1) * ps, :] = page[0, :, 3 * cw:4 * cw].astype(BF16)

    @pl.when(p == n_steps)
    def _():
        new = new_ref[0]
        n_new = new.shape[0]
        for r, ref in enumerate((kc_ref, vc_ref)):
            ref[0] = jnp.zeros(ref.shape[1:], F32)
            for l in range(n_new):
                ref[0, 0:1, l * cw:(l + 1) * cw] = new[l:l + 1, r * cw:(r + 1) * cw]
        for r, ref in ((2, ks_ref), (3, vs_ref)):
            rows = jnp.concatenate([new[:, r * cw:(r + 1) * cw], jnp.zeros((2 * ps - n_new, cw), F32)], axis=0)
            ref[0] = rows.astype(BF16)


def _gather_pages(pages, page_table, new_rows, blk):
    NP, PS, C = pages.shape
    B, n_pages = page_table.shape
    assert n_pages % 2 == 0 and PS % blk == 0 and (2 * PS // blk) % SUBLANES == 0 and new_rows.shape[1] <= blk
    cw = C // 4
    n_steps = n_pages // 2
    L = (n_pages + 2) * PS
    nb = 2 * PS // blk
    page_spec = lambda j: pl.BlockSpec(
        (1, PS, C), lambda b, p, pt: (pt[b, jnp.minimum(2 * p + j, n_pages - 2 + j)], 0, 0))
    cmp_spec = pl.BlockSpec((1, nb, blk * cw), lambda b, p, pt: (b, p, 0))
    sel_spec = pl.BlockSpec((1, 2 * PS, cw), lambda b, p, pt: (b, p, 0))
    return pl.pallas_call(
        functools.partial(_gather_kernel, n_steps=n_steps, cw=cw, blk=blk),
        grid_spec=pltpu.PrefetchScalarGridSpec(
            num_scalar_prefetch=1, grid=(B, n_steps + 1),
            in_specs=[page_spec(0), page_spec(1),
                      pl.BlockSpec((1, new_rows.shape[1], C), lambda b, p, pt: (b, 0, 0))],
            out_specs=[cmp_spec, cmp_spec, sel_spec, sel_spec],
            scratch_shapes=[pltpu.VMEM((2 * (2 * cw // LANES), PS, LANES), F32)]),
        out_shape=[jax.ShapeDtypeStruct((B, L // blk, blk * cw), F32)] * 2 +
                  [jax.ShapeDtypeStruct((B, L, cw), BF16)] * 2,
        compiler_params=_cparams("parallel", "arbitrary"), name="kv_gather")(page_table, pages, pages, new_rows)


def _compress_blocks(xb, w_c, pos_c, nc_pad):
    B, n, K = xb.shape
    blk, dh, _ = w_c.shape
    C = K // blk
    G = C // dh
    w_bd = jnp.einsum('lde,gh->lgdhe', w_c, jnp.eye(G, dtype=w_c.dtype)).reshape(K, C)
    xbias = jnp.broadcast_to(pos_c[:, None, :], (blk, G, dh)).reshape(1, K)
    out = _matmul(xb.reshape(B * n, K), w_bd.astype(BF16), xbias=xbias, tm_cap=128)
    return jnp.pad(out.reshape(B, n, C), ((0, 0), (0, nc_pad - n), (0, 0)))


def _compress(x, w_c, pos_c, nc_pad):
    B, L, C = x.shape
    blk = w_c.shape[0]
    return _compress_blocks(x.reshape(B, L // blk, blk * C), w_c, pos_c, nc_pad)


def _lane_tiles(x):
    return [x[:, j * LANES:(j + 1) * LANES] for j in range(x.shape[1] // LANES)]


def _row_max(x):
    return jnp.max(functools.reduce(jnp.maximum, _lane_tiles(x)), axis=-1, keepdims=True)


def _row_sum(x):
    return jnp.sum(functools.reduce(jnp.add, _lane_tiles(x)), axis=-1, keepdims=True)


def _split3_dot(x, m):
    x1 = x.astype(BF16)
    r1 = x - x1.astype(F32)
    x2 = r1.astype(BF16)
    x3 = (r1 - x2.astype(F32)).astype(BF16)
    d = lambda a: jnp.dot(a, m, preferred_element_type=F32)
    return d(x1) + d(x2) + d(x3)


def _nsa_staged_kernel(q_ref, gt_ref, kc_ref, vc_ref, ks_ref, vs_ref, kw_ref, vw_ref, ex_ref, o_ref, *,
                       seqs, nq, t0, tk, n_cmp, cmp_blk, n_sel, kvh, hg, dh, slopes, win_rows, w0):
    q0 = t0 + pl.program_id(1) * nq
    R = hg * nq
    NC = kc_ref.shape[1]
    Lk = ks_ref.shape[1]
    NSP = ex_ref.shape[0]
    assert tk == Lk
    qscale = dh ** -0.5 * LOG2E
    slopes2 = tuple(s * LOG2E for s in slopes)
    units = [(s, g) for s in range(seqs) for g in range(kvh)]
    col = lambda g: (g // 2) * 2 * dh

    def tq_like(shape):
        return q0 + lax.broadcasted_iota(I32, shape, 0)

    def attend(keys, vals, distms):
        s_alls = [lax.dot_general(qgs[u], keys[u], _NT, preferred_element_type=F32) for u in range(len(units))]
        s2s = [[s_alls[u][h * nq:(h + 1) * nq] - slopes2[g * hg + h] * distms[u] for h in range(hg)]
               for u, (_, g) in enumerate(units)]
        ms = [[_row_max(x) for x in row] for row in s2s]
        es = [[jnp.exp2(x - jnp.where(m == -jnp.inf, 0.0, m)) for x, m in zip(xr, mr)] for xr, mr in zip(s2s, ms)]
        ls = [[_row_sum(e) for e in row] for row in es]
        ps = [[e / jnp.maximum(l, 1e-30) for e, l in zip(er, lr)] for er, lr in zip(es, ls)]
        os = [jnp.dot(jnp.concatenate(ps[u], axis=0).astype(BF16), vals[u], preferred_element_type=F32)
              for u in range(len(units))]
        return ps, os

    n_idx = lax.broadcasted_iota(I32, (nq, NC), 1)
    dist_c = tq_like((nq, NC)) - (n_idx * cmp_blk + (cmp_blk - 1))
    distm_c = jnp.where((dist_c >= 0) & (n_idx < n_cmp), dist_c.astype(F32), jnp.inf)
    dist_w = tq_like((nq, win_rows)) - (w0 + lax.broadcasted_iota(I32, (nq, win_rows), 1))
    distm_w = jnp.where((dist_w >= 0) & (dist_w < WINDOW), dist_w.astype(F32), jnp.inf)
    dist_s = tq_like((nq, Lk)) - lax.broadcasted_iota(I32, (nq, Lk), 1)
    per_sel = SEL_BLOCK // cmp_blk
    pair = jnp.where(lax.broadcasted_iota(I32, (NC, NSP), 0) // per_sel ==
                     lax.broadcasted_iota(I32, (NC, NSP), 1), 1.0, 0.0).astype(BF16)
    blk = lax.broadcasted_iota(I32, (nq, NSP), 1)
    tqs = tq_like((nq, NSP))
    cur = tqs // SEL_BLOCK
    forced = (blk == 0) | (blk == cur) | (blk == cur - 1)
    sel_valid = blk * SEL_BLOCK <= tqs

    qgs = [(jnp.concatenate([q_ref[s, :, (g * hg + h) * 2 * dh:(g * hg + h + 1) * 2 * dh] for h in range(hg)],
                            axis=0) * qscale).astype(BF16) for s, g in units]
    tile = lambda ref, s, g: ref[s, :, col(g):col(g) + 2 * dh]
    p_cmp, o_cmp = attend([tile(kc_ref, s, g).astype(BF16) for s, g in units],
                          [tile(vc_ref, s, g).astype(BF16) for s, g in units], [distm_c] * len(units))
    _, o_win = attend([tile(kw_ref, s, g) for s, g in units], [tile(vw_ref, s, g) for s, g in units],
                      [distm_w] * len(units))

    psums = [functools.reduce(jnp.add, row) for row in p_cmp]
    imps = [_split3_dot(x, pair) for x in psums]
    scores = [jnp.where(blk < n_sel, jnp.where(forced, FORCE_SCORE, jnp.where(sel_valid, imp, -1.0)), -2.0)
              for imp in imps]
    selm = _top_blocks(jnp.concatenate(scores, axis=0), min(N_SEL, n_sel), NSP)
    selb = selm.astype(BF16)
    expand = ex_ref[...]
    selxs = [jnp.dot(selb[u * nq:(u + 1) * nq], expand, preferred_element_type=F32) for u in range(len(units))]
    distm_s = [jnp.where((dist_s >= 0) & (x > 0.5), dist_s.astype(F32), jnp.inf) for x in selxs]
    _, o_sel = attend([tile(ks_ref, s, g) for s, g in units], [tile(vs_ref, s, g) for s, g in units], distm_s)

    lane = lax.broadcasted_iota(I32, (nq, 2 * dh), 1)
    for s in range(seqs):
        gates = jax.nn.sigmoid(gt_ref[s])
        for g in range(kvh):
            u = s * kvh + g
            lo = (g % 2) * dh
            keep = (lane >= lo) & (lane < lo + dh)
            for h in range(hg):
                hd = g * hg + h
                rows = slice(h * nq, (h + 1) * nq)
                o = (gates[:, 3 * hd:3 * hd + 1] * o_cmp[u][rows] + gates[:, 3 * hd + 1:3 * hd + 2] * o_sel[u][rows] +
                     gates[:, 3 * hd + 2:3 * hd + 3] * o_win[u][rows])
                o_ref[s, :, hd * 2 * dh:(hd + 1) * 2 * dh] = jnp.where(keep, o, 0.0)


def _top_blocks(score, n_top, nsp):
    if score.shape[0] % LANES == 0 and nsp == LANES:
        st = score.T
        brow = lax.broadcasted_iota(I32, st.shape, 0).astype(F32)
        sel_t = jnp.zeros(st.shape, F32)
        for _ in range(n_top):
            m = jnp.max(st, axis=0, keepdims=True)
            idx = jnp.min(jnp.where(st == m, brow, float(nsp)), axis=0, keepdims=True)
            hit = brow == idx
            sel_t = jnp.where(hit, 1.0, sel_t)
            st = jnp.where(hit, -jnp.inf, st)
        return sel_t.T
    bcol = lax.broadcasted_iota(I32, score.shape, 1)
    selm = jnp.zeros(score.shape, F32)
    for _ in range(n_top):
        m = jnp.max(score, axis=-1, keepdims=True)
        idx = jnp.min(jnp.where(score == m, bcol, nsp), axis=-1, keepdims=True)
        hit = bcol == idx
        selm = jnp.where(hit, 1.0, selm)
        score = jnp.where(hit, -jnp.inf, score)
    return selm


def _nsa_multi_kernel(q_ref, gt_ref, kc_ref, vc_ref, ks_ref, vs_ref, kw_ref, vw_ref, ex_ref, o_ref, *scratch,
                      seqs, **cfg):
    for s in range(seqs):
        one = lambda r: r.at[pl.ds(s, 1)]
        _nsa_kernel(one(q_ref), one(gt_ref), one(kc_ref), one(vc_ref), one(ks_ref), one(vs_ref), one(kw_ref),
                    one(vw_ref), ex_ref, one(o_ref), *scratch, **cfg)


def _nsa_kernel(q_ref, gt_ref, kc_ref, vc_ref, ks_ref, vs_ref, kw_ref, vw_ref, ex_ref, o_ref,
                m_scr, l_scr, acc_scr, flag_ref, *,
                nq, t0, tk, n_cmp, cmp_blk, n_sel, kvh, hg, dh, slopes, win_rows, w0, causal_tiles):
    i = pl.program_id(1)
    q0 = t0 + i * nq
    R = hg * nq
    NC = kc_ref.shape[1]
    Lk = ks_ref.shape[1]
    NSP = ex_ref.shape[0]
    qscale = dh ** -0.5 * LOG2E
    slopes2 = tuple(s * LOG2E for s in slopes)
    gates = jax.nn.sigmoid(gt_ref[0])

    def tq_like(shape):
        return q0 + lax.broadcasted_iota(I32, shape, 0)

    if causal_tiles:
        wstart = pl.multiple_of(jnp.maximum(q0 - WINDOW, 0), nq)
    else:
        wstart = 0

    def col(g):
        return (g // 2) * 2 * dh

    groups = range(kvh)

    def attend(keys, vals, distm, want_p):
        s_alls = [lax.dot_general(qgs[g], keys[g], _NT, preferred_element_type=F32) for g in groups]
        s2s = [[s_alls[g][h * nq:(h + 1) * nq] - slopes2[g * hg + h] * distm for h in range(hg)] for g in groups]
        ms = [[_row_max(x) for x in row] for row in s2s]
        es = [[jnp.exp2(x - jnp.where(m == -jnp.inf, 0.0, m)) for x, m in zip(xr, mr)] for xr, mr in zip(s2s, ms)]
        ls = [[jnp.maximum(_row_sum(e), 1e-30) for e in row] for row in es]
        if want_p:
            ps = [[e / l for e, l in zip(er, lr)] for er, lr in zip(es, ls)]
            os = [jnp.dot(jnp.concatenate(ps[g], axis=0).astype(BF16), vals[g], preferred_element_type=F32)
                  for g in groups]
            return ps, os
        os = [jnp.dot(jnp.concatenate(es[g], axis=0).astype(BF16), vals[g], preferred_element_type=F32) /
              jnp.concatenate(ls[g], axis=0) for g in groups]
        return None, os

    n_idx = lax.broadcasted_iota(I32, (nq, NC), 1)
    dist_c = tq_like((nq, NC)) - (n_idx * cmp_blk + (cmp_blk - 1))
    distm_c = jnp.where((dist_c >= 0) & (n_idx < n_cmp), dist_c.astype(F32), jnp.inf)
    dist_w = tq_like((nq, win_rows)) - (w0 + wstart + lax.broadcasted_iota(I32, (nq, win_rows), 1))
    distm_w = jnp.where((dist_w >= 0) & (dist_w < WINDOW), dist_w.astype(F32), jnp.inf)
    per_sel = SEL_BLOCK // cmp_blk
    pair = jnp.where(lax.broadcasted_iota(I32, (NC, NSP), 0) // per_sel ==
                     lax.broadcasted_iota(I32, (NC, NSP), 1), 1.0, 0.0).astype(BF16)
    blk = lax.broadcasted_iota(I32, (nq, NSP), 1)
    tqs = tq_like((nq, NSP))
    cur = tqs // SEL_BLOCK
    forced = (blk == 0) | (blk == cur) | (blk == cur - 1)
    sel_valid = blk * SEL_BLOCK <= tqs

    qgs = [(jnp.concatenate([q_ref[0, :, (g * hg + h) * 2 * dh:(g * hg + h + 1) * 2 * dh] for h in range(hg)],
                            axis=0) * qscale).astype(BF16) for g in groups]
    p_cmps, o_cmps = attend([kc_ref[0, :, col(g):col(g) + 2 * dh].astype(BF16) for g in groups],
                            [vc_ref[0, :, col(g):col(g) + 2 * dh].astype(BF16) for g in groups], distm_c, True)
    _, o_wins = attend([kw_ref[0, pl.ds(wstart, win_rows), col(g):col(g) + 2 * dh] for g in groups],
                       [vw_ref[0, pl.ds(wstart, win_rows), col(g):col(g) + 2 * dh] for g in groups], distm_w, False)
    imps = [_split3_dot(functools.reduce(jnp.add, p_cmps[g]), pair) for g in groups]
    scores = [jnp.where(blk < n_sel, jnp.where(forced, FORCE_SCORE, jnp.where(sel_valid, imp, -1.0)), -2.0)
              for imp in imps]

    selm = _top_blocks(jnp.concatenate(scores, axis=0), min(N_SEL, n_sel), NSP)
    selb = selm.astype(BF16)

    def sel_tile(kt, carry):
        k0 = kt * tk
        dist = tq_like((nq, tk)) - (k0 + lax.broadcasted_iota(I32, (nq, tk), 1))
        causal = dist >= 0
        distf = dist.astype(F32)
        expand = ex_ref[:, pl.ds(k0, tk)]
        new = []
        for g in range(kvh):
            m_run, l_run, acc = carry[g]
            c0 = col(g)
            s_t = lax.dot_general(qgs[g], ks_ref[0, pl.ds(k0, tk), c0:c0 + 2 * dh], _NT,
                                  preferred_element_type=F32)
            selx = jnp.dot(selb[g * nq:(g + 1) * nq], expand, preferred_element_type=F32)
            distm = jnp.where(causal & (selx > 0.5), distf, jnp.inf)
            m_new, l_new, alphas, ps = [], [], [], []
            for h in range(hg):
                rows = slice(h * nq, (h + 1) * nq)
                s2 = s_t[rows] - slopes2[g * hg + h] * distm
                m_h = jnp.maximum(m_run[rows], _row_max(s2))
                m_safe = jnp.where(m_h == -jnp.inf, 0.0, m_h)
                a_h = jnp.exp2(m_run[rows] - m_safe)
                p_h = jnp.exp2(s2 - m_safe)
                m_new.append(m_h)
                l_new.append(a_h * l_run[rows] + _row_sum(p_h))
                alphas.append(a_h)
                ps.append(p_h)
            pv = jnp.dot(jnp.concatenate(ps, axis=0).astype(BF16), vs_ref[0, pl.ds(k0, tk), c0:c0 + 2 * dh],
                         preferred_element_type=F32)
            new.append((jnp.concatenate(m_new, axis=0), jnp.concatenate(l_new, axis=0),
                        jnp.concatenate(alphas, axis=0) * acc + pv))
        return tuple(new)

    def sel_tile_group(kt, g):
        k0 = pl.multiple_of(kt * tk, tk)
        c0 = col(g)
        dist = tq_like((nq, tk)) - (k0 + lax.broadcasted_iota(I32, (nq, tk), 1))
        selx = jnp.dot(selb[g * nq:(g + 1) * nq], ex_ref[:, pl.ds(k0, tk)], preferred_element_type=F32)
        distm = jnp.where((dist >= 0) & (selx > 0.5), dist.astype(F32), jnp.inf)
        s_t = lax.dot_general(qgs[g], ks_ref[0, pl.ds(k0, tk), c0:c0 + 2 * dh], _NT,
                              preferred_element_type=F32)
        m_run = m_scr[g]
        l_run = l_scr[g]
        heads = range(hg)
        rows = [slice(h * nq, (h + 1) * nq) for h in heads]
        s2s = [s_t[rows[h]] - slopes2[g * hg + h] * distm for h in heads]
        tmax = [_row_max(x) for x in s2s]
        m_new = [jnp.maximum(m_run[rows[h]], tmax[h]) for h in heads]
        m_safe = [jnp.where(m == -jnp.inf, 0.0, m) for m in m_new]
        alphas = [jnp.exp2(m_run[rows[h]] - m_safe[h]) for h in heads]
        ps = [jnp.exp2(s2s[h] - jnp.concatenate([m_safe[h]] * (tk // LANES), axis=1)) for h in heads]
        l_new = [alphas[h] * l_run[rows[h]] + _row_sum(ps[h]) for h in heads]
        pv = jnp.dot(jnp.concatenate(ps, axis=0).astype(BF16), vs_ref[0, pl.ds(k0, tk), c0:c0 + 2 * dh],
                     preferred_element_type=F32)
        m_scr[g] = jnp.concatenate(m_new, axis=0)
        l_scr[g] = jnp.concatenate(l_new, axis=0)
        acc_scr[g] = jnp.concatenate(alphas, axis=0) * acc_scr[g] + pv

    if causal_tiles:
        n_flag = Lk // tk
        blocks_per_tile = tk // SEL_BLOCK
        lane_b = lax.broadcasted_iota(I32, (1, NSP), 1)
        for g in range(kvh):
            any_q = jnp.max(selm[g * nq:(g + 1) * nq], axis=0, keepdims=True)
            for kt in range(n_flag):
                in_tile = (lane_b >= kt * blocks_per_tile) & (lane_b < (kt + 1) * blocks_per_tile)
                flag_ref[g * n_flag + kt] = (jnp.max(jnp.where(in_tile, any_q, 0.0)) > 0.5).astype(I32)
        m_scr[...] = jnp.full(m_scr.shape, -jnp.inf, F32)
        l_scr[...] = jnp.zeros(l_scr.shape, F32)
        acc_scr[...] = jnp.zeros(acc_scr.shape, F32)

        def visit(kt, carry):
            for g in range(kvh):
                @pl.when(flag_ref[g * n_flag + kt] > 0)
                def _():
                    sel_tile_group(kt, g)
            return carry

        lax.fori_loop(0, (q0 + nq + tk - 1) // tk, visit, 0)
        fin = tuple((None, l_scr[g], acc_scr[g]) for g in range(kvh))
    else:
        fin = tuple((jnp.full((R, 1), -jnp.inf, F32), jnp.zeros((R, 1), F32), jnp.zeros((R, 2 * dh), F32))
                    for _ in range(kvh))
        for kt in range(Lk // tk):
            fin = sel_tile(kt, fin)

    lane = lax.broadcasted_iota(I32, (nq, 2 * dh), 1)
    for g in range(kvh):
        lo = (g % 2) * dh
        keep = (lane >= lo) & (lane < lo + dh)
        _, l_fin, acc = fin[g]
        o_sel = acc / jnp.maximum(l_fin, 1e-30)
        for h in range(hg):
            hd = g * hg + h
            rows = slice(h * nq, (h + 1) * nq)
            o = (gates[:, 3 * hd:3 * hd + 1] * o_cmps[g][rows] + gates[:, 3 * hd + 1:3 * hd + 2] * o_sel[rows] +
                 gates[:, 3 * hd + 2:3 * hd + 3] * o_wins[g][rows])
            o_ref[0, :, hd * 2 * dh:(hd + 1) * 2 * dh] = jnp.where(keep, o, 0.0)


def _nsa_attend(q, gt, kc, vc, ks, vs, kw, vw, cols, *, nq, t0, tk, n_cmp, cmp_blk, n_sel, kvh, dh,
                win_rows, w0, causal_tiles, seqs=1):
    B, Lq, QW = q.shape
    H = QW // (2 * dh)
    hg = H // kvh
    cw = kvh * dh
    Lk = ks.shape[1]
    NC = kc.shape[1]
    nsp = LANES
    assert n_sel <= nsp and Lk % tk == 0 and Lq % nq == 0 and dh * 2 == LANES
    expand = (lax.broadcasted_iota(I32, (nsp, Lk), 1) // SEL_BLOCK ==
              lax.broadcasted_iota(I32, (nsp, Lk), 0)).astype(BF16)
    slopes = tuple(2.0 ** (-8.0 * (h + 1) / H) for h in range(H))
    assert B % seqs == 0
    cfg = dict(seqs=seqs, nq=nq, t0=t0, tk=tk, n_cmp=n_cmp, cmp_blk=cmp_blk, n_sel=n_sel, kvh=kvh, hg=hg, dh=dh,
               slopes=slopes, win_rows=win_rows, w0=w0)
    R = hg * nq
    if causal_tiles:
        kern = functools.partial(_nsa_multi_kernel, causal_tiles=True, **cfg)
        scratch = [pltpu.VMEM((kvh, R, LANES), F32), pltpu.VMEM((kvh, R, LANES), F32),
                   pltpu.VMEM((kvh, R, 2 * dh), F32), pltpu.SMEM((kvh * (Lk // tk),), I32)]
    else:
        assert win_rows == kw.shape[1]
        kern = functools.partial(_nsa_staged_kernel, **cfg)
        scratch = []
    qmap = lambda b, i: (b, i, 0)
    kv_spec = lambda arr, c: pl.BlockSpec((seqs, arr.shape[1], cw), lambda b, i, c=c: (b, 0, c))
    return pl.pallas_call(
        kern, grid=(B // seqs, Lq // nq),
        in_specs=[pl.BlockSpec((seqs, nq, QW), qmap), pl.BlockSpec((seqs, nq, LANES), qmap),
                  kv_spec(kc, cols[0]), kv_spec(vc, cols[1]), kv_spec(ks, cols[2]), kv_spec(vs, cols[3]),
                  kv_spec(kw, cols[4]), kv_spec(vw, cols[5]),
                  pl.BlockSpec((nsp, Lk), lambda b, i: (0, 0))],
        out_specs=pl.BlockSpec((seqs, nq, QW), qmap),
        out_shape=jax.ShapeDtypeStruct((B, Lq, QW), F32),
        scratch_shapes=scratch,
        compiler_params=_cparams("parallel", "arbitrary"), name="nsa_attend")(q, gt, kc, vc, ks, vs, kw, vw, expand)


def _head_tiles(w, H, kvh, dh, axis):
    hg = H // kvh
    parts = []
    for h in range(H):
        sl = [slice(None)] * w.ndim
        sl[axis] = slice(h * dh, (h + 1) * dh)
        piece = w[tuple(sl)]
        zero = jnp.zeros_like(piece)
        parts += [zero, piece] if ((h // hg) % 2) else [piece, zero]
    return jnp.concatenate(parts, axis=axis)


def kernel(x_prompt, x_sample, cache_kv_pages, cache_win, state_ssm, state_conv, page_table, ln_g, ln_b, m_w_in, m_conv_w, m_conv_b, m_dt_bias, m_a_log, m_d_skip, m_norm_w, m_w_out, w_kv_shared, w_cmp, pos_cmp, nsa_w_in, nsa_w_out, peer_w_q, peer_sub_keys, peer_u, peer_v):
    Bp, Lp, D = x_prompt.shape
    Bs, Ls, _ = x_sample.shape
    depth = ln_g.shape[0]
    n_a = m_w_in.shape[0]
    alpha = (2 * depth) ** DEPTH_ALPHA_POW
    Tp, Ts = Bp * Lp, Bs * Ls
    assert Tp % LANES == 0
    Tsp = -(-Ts // LANES) * LANES
    Xp = x_prompt.reshape(Tp, D)
    Xs = jnp.pad(x_sample.reshape(Ts, D), ((0, Tsp - Ts), (0, 0)))
    dims = (Bp, Lp, Bs, Ls)

    NP, PS, n_rows, kvh, dh = cache_kv_pages.shape
    cw = kvh * dh
    n_pages = page_table.shape[1]
    past_len = n_pages * PS
    win_cache = cache_win.shape[1]
    cmp_blk = w_cmp.shape[1]
    H = nsa_w_in.shape[2] // (dh + 3)

    ssm_p, ssm_s, conv_p, conv_s = [], [], [], []
    for i in range(depth):
        if i < n_a:
            mp = (m_w_in[i], m_conv_w[i], m_conv_b[i], m_dt_bias[i], m_a_log[i], m_d_skip[i], m_norm_w[i], m_w_out[i])
            Xp, Xs, hp, hs, cp, cs = _mamba_layer(Xp, Xs, dims, state_conv[i], state_ssm[i], mp,
                                                  ln_g[i, 0], ln_b[i, 0], alpha)
            ssm_p.append(hp)
            ssm_s.append(hs)
            conv_p.append(cp)
            conv_s.append(cs)
        else:
            if i == n_a:
                wkv = w_kv_shared.astype(BF16)
                kv_p = _matmul(Xp, wkv).reshape(Bp, Lp, 6 * cw)
                kv_s = _matmul(Xs, wkv)[:Ts].reshape(Bs, Ls, 6 * cw)
                kvb_p = kv_p.astype(BF16)
                n_sel_p = -(-Lp // SEL_BLOCK)
                Lpp = n_sel_p * SEL_BLOCK
                padp = lambda v: jnp.pad(v, ((0, 0), (0, Lpp - Lp), (0, 0)))
                nc_p = -(-(Lpp // cmp_blk) // LANES) * LANES
                kc_p = _compress(padp(kv_p[:, :, 0:cw]), w_cmp[0], pos_cmp[0], nc_p)
                vc_p = _compress(padp(kv_p[:, :, cw:2 * cw]), w_cmp[1], pos_cmp[1], nc_p)
                tk_p = _pick(Lp, (512, 256, 128))
                new_rows = jnp.pad(kv_s[:, :, :4 * cw], ((0, 0), (0, SUBLANES - Ls), (0, 0)))
                kcb_s, vcb_s, ks_s, vs_s = _gather_pages(cache_kv_pages.reshape(NP, PS, n_rows * cw), page_table,
                                                         new_rows, cmp_blk)
                Lk_s = ks_s.shape[1]
                n_sel_s = -(-(past_len + Ls) // SEL_BLOCK)
                nc_s = -(-(Lk_s // cmp_blk) // LANES) * LANES
                kc_s = _compress_blocks(kcb_s, w_cmp[0], pos_cmp[0], nc_s)
                vc_s = _compress_blocks(vcb_s, w_cmp[1], pos_cmp[1], nc_s)
                win_full_s = jnp.concatenate([cache_win, kv_s[:, :, 4 * cw:].reshape(Bs, Ls, 2, kvh, dh)], axis=1)
                Lw = win_cache + Ls
                Lwp = -(-Lw // LANES) * LANES
                wflat = jnp.pad(win_full_s.reshape(Bs, Lw, 2 * cw), ((0, 0), (0, Lwp - Lw), (0, 0))).astype(BF16)
            j = i - n_a
            wq = _head_tiles(nsa_w_in[j][:, :H * dh], H, kvh, dh, axis=1).astype(BF16)
            wg = _pad_cols(nsa_w_in[j][:, H * dh:], LANES).astype(BF16)
            wo = _head_tiles(nsa_w_out[j], H, kvh, dh, axis=0).astype(BF16)
            QW = wq.shape[1]
            o_p = _nsa_attend(_matmul(Xp, wq).reshape(Bp, Lp, QW), _matmul(Xp, wg).reshape(Bp, Lp, LANES),
                              kc_p, vc_p, kvb_p, kvb_p, kvb_p, kvb_p, (0, 0, 2, 3, 4, 5),
                              nq=Q_BLOCK, t0=0, tk=tk_p, n_cmp=Lpp // cmp_blk, cmp_blk=cmp_blk, n_sel=n_sel_p,
                              kvh=kvh, dh=dh, win_rows=min(WINDOW + Q_BLOCK, Lp), w0=0, causal_tiles=True)
            Xp = _matmul_ln(o_p.reshape(Tp, QW), wo, Xp, ln_g[i, 0], ln_b[i, 0], alpha)
            pad_q = lambda v: jnp.pad(v[:Ts].reshape(Bs, Ls, -1), ((0, 0), (0, SUBLANES - Ls), (0, 0)))
            o_s = _nsa_attend(pad_q(_matmul(Xs, wq)), pad_q(_matmul(Xs, wg)), kc_s, vc_s, ks_s, vs_s, wflat, wflat,
                              (0, 0, 0, 0, 0, 1),
                              nq=SUBLANES, t0=past_len, tk=Lk_s, n_cmp=-(-(past_len + Ls) // SEL_BLOCK) * (SEL_BLOCK // cmp_blk),
                              cmp_blk=cmp_blk, n_sel=n_sel_s, kvh=kvh, dh=dh, win_rows=Lwp,
                              w0=past_len - win_cache, causal_tiles=False, seqs=_pick(Bs, (4, 2, 1)))
            o_s = jnp.pad(o_s[:, :Ls].reshape(Ts, QW), ((0, Tsp - Ts), (0, 0)))
            Xs = _matmul_ln(o_s, wo, Xs, ln_g[i, 0], ln_b[i, 0], alpha)
        peer = (peer_w_q[i].astype(BF16), peer_sub_keys[i], peer_u[i].astype(BF16), peer_v[i].astype(BF16),
                ln_g[i, 1], ln_b[i, 1], alpha)
        Xp = _peer_layer(Xp, *peer)
        Xs = _peer_layer(Xs, *peer)

    rows_p = kv_p[:, :, :4 * cw].reshape(Bp, Lp, n_rows, kvh, dh)
    rows_s = kv_s[:, :, :4 * cw].reshape(Bs, Ls, n_rows, kvh, dh)
    win_p = kv_p[:, :, 4 * cw:].reshape(Bp, Lp, 2, kvh, dh)[:, -min(WINDOW, Lp):]
    return (Xp.reshape(Bp, Lp, D), Xs[:Ts].reshape(Bs, Ls, D), rows_p, rows_s, win_p,
            win_full_s[:, -win_cache:], jnp.stack(ssm_p), jnp.stack(ssm_s), jnp.stack(conv_p), jnp.stack(conv_s))
```

```python
import functools
import math

import jax
import jax.numpy as jnp
from jax import lax
from jax.experimental import pallas as pl
from jax.experimental.pallas import tpu as pltpu

F32 = jnp.float32
BF16 = jnp.bfloat16
I32 = jnp.int32

DEPTH_ALPHA_POW = 0.25
LN_EPS = 1e-5
SEL_BLOCK = 64
N_SEL = 16
WINDOW = 512
FORCE_SCORE = 1.0e4
PEER_TOPK = 16
SSD_CHUNK = 128
Q_BLOCK = 128
GELU_C0 = math.sqrt(2.0 / math.pi)
GELU_C1 = 0.044715 * GELU_C0
LOG2E = 1.0 / math.log(2.0)
MAX_FULL_N = 3072

LANES = 128
SUBLANES = 8
VMEM_LIMIT_BYTES = 56 * 1024 * 1024

_NT = (((1,), (1,)), ((), ()))
_TN = (((0,), (0,)), ((), ()))
_HI = lax.Precision.HIGHEST


def _cparams(*sem):
    return pltpu.CompilerParams(dimension_semantics=sem, vmem_limit_bytes=VMEM_LIMIT_BYTES)


def _pick(n, cands):
    for c in cands:
        if n % c == 0:
            return c
    raise ValueError(f"no tile in {cands} divides {n}")


def _pad_cols(w, n):
    return jnp.pad(w, ((0, 0), (0, n - w.shape[1])))


def _mm_kernel(x_ref, w_ref, o_ref):
    o_ref[...] = jnp.dot(x_ref[...].astype(BF16), w_ref[...],
                         preferred_element_type=F32).astype(o_ref.dtype)


def _mm_bias_kernel(x_ref, xb_ref, w_ref, o_ref):
    x = (x_ref[...] + xb_ref[...]).astype(BF16)
    o_ref[...] = jnp.dot(x, w_ref[...], preferred_element_type=F32).astype(o_ref.dtype)


def _matmul(x, w, *, xbias=None, out_dtype=F32, tm_cap=512):
    M, K = x.shape
    N = w.shape[1]
    tm = _pick(M, tuple(t for t in (512, 256, 128, 64, 32, 16, 8) if t <= tm_cap))
    tn = N if N <= MAX_FULL_N else _pick(N, (1024, 512, 256, 128))
    x_spec = pl.BlockSpec((tm, K), lambda j, i: (i, 0))
    w_spec = pl.BlockSpec((K, tn), lambda j, i: (0, j))
    o_spec = pl.BlockSpec((tm, tn), lambda j, i: (i, j))
    if xbias is None:
        body, specs, args = _mm_kernel, [x_spec, w_spec], (x, w)
    else:
        b_spec = pl.BlockSpec((1, K), lambda j, i: (0, 0))
        body, specs, args = _mm_bias_kernel, [x_spec, b_spec, w_spec], (x, xbias, w)
    return pl.pallas_call(
        body, grid=(N // tn, M // tm), in_specs=specs, out_specs=o_spec,
        out_shape=jax.ShapeDtypeStruct((M, N), out_dtype),
        compiler_params=_cparams("parallel", "parallel"), name="proj")(*args)


def _deepnorm_ln(v, g, b):
    mu = jnp.mean(v, axis=-1, keepdims=True)
    d = v - mu
    var = jnp.mean(d * d, axis=-1, keepdims=True)
    return d * lax.rsqrt(var + LN_EPS) * g + b


def _mm_ln_kernel(x_ref, w_ref, r_ref, g_ref, b_ref, o_ref, *, alpha):
    y = jnp.dot(x_ref[...].astype(BF16), w_ref[...], preferred_element_type=F32)
    o_ref[...] = _deepnorm_ln(alpha * r_ref[...] + y, g_ref[...], b_ref[...])


def _matmul_ln(x, w, resid, g, b, alpha):
    M, K = x.shape
    D = w.shape[1]
    tm = _pick(M, (512, 256, 128, 64, 32, 16, 8))
    row = lambda i: (i, 0)
    fixed = lambda i: (0, 0)
    return pl.pallas_call(
        functools.partial(_mm_ln_kernel, alpha=alpha), grid=(M // tm,),
        in_specs=[pl.BlockSpec((tm, K), row), pl.BlockSpec((K, D), fixed),
                  pl.BlockSpec((tm, D), row), pl.BlockSpec((1, D), fixed), pl.BlockSpec((1, D), fixed)],
        out_specs=pl.BlockSpec((tm, D), row),
        out_shape=jax.ShapeDtypeStruct((M, D), F32),
        compiler_params=_cparams("parallel"), name="proj_ln")(x, w, resid, g.reshape(1, D), b.reshape(1, D))


def _softplus(x):
    return jnp.maximum(x, 0.0) + jnp.log1p(jnp.exp(-jnp.abs(x)))


def _silu(x):
    return x * jax.nn.sigmoid(x)


def _ssd_kernel(z_ref, xbc_ref, dtr_ref, cbuf_ref, h0_ref, cw_ref, cb_ref, dtb_ref, alog_ref,
                dsk_ref, nw_ref, y_ref, hn_ref, xf_scr, h_scr, *, Lc, H, P, N, G, n_valid, conv_w):
    c = pl.program_id(1)
    DI = H * P
    HG = H // G
    GP = HG * P
    tail = xf_scr.shape[0] - Lc

    @pl.when(c == 0)
    def _():
        xf_scr[0:tail, :] = cbuf_ref[0]
        h_scr[...] = h0_ref[0]

    xf_scr[tail:tail + Lc, :] = xbc_ref[0]
    conv = cb_ref[...]
    for k in range(conv_w):
        conv = conv + xf_scr[pl.ds(tail - (conv_w - 1) + k, Lc), :] * cw_ref[k:k + 1, :]
    conv = _silu(conv)
    xf_scr[0:tail, :] = xf_scr[pl.ds(Lc, tail), :]

    xs = conv[:, :DI]
    Bm = conv[:, DI:DI + G * N]
    Cm = conv[:, DI + G * N:]

    lane = lax.broadcasted_iota(I32, (Lc, LANES), 1)
    rowi = lax.broadcasted_iota(I32, (Lc, LANES), 0)
    dt = jnp.where((lane < H) & (rowi < n_valid), _softplus(dtr_ref[0] + dtb_ref[...]), 0.0)
    a = dt * (-jnp.exp(alog_ref[...]))
    ti = lax.broadcasted_iota(I32, (Lc, Lc), 0)
    si = lax.broadcasted_iota(I32, (Lc, Lc), 1)
    causal = ti >= si
    tril = jnp.where(causal, 1.0, 0.0).astype(F32)
    eye = jnp.where(lax.broadcasted_iota(I32, (LANES, LANES), 0) ==
                    lax.broadcasted_iota(I32, (LANES, LANES), 1), 1.0, 0.0).astype(F32)
    acum = jnp.dot(tril, a, precision=_HI, preferred_element_type=F32)
    acum_t = lax.dot_general(eye, acum, _NT, precision=_HI, preferred_element_type=F32)
    dt_t = lax.dot_general(eye, dt, _NT, precision=_HI, preferred_element_type=F32)
    alast = acum[Lc - 1:Lc, :]
    wend = jnp.exp(alast - acum) * dt
    ea = jnp.exp(acum)
    edec = jnp.exp(alast)
    dsk = dsk_ref[...]
    first_half = lax.broadcasted_iota(I32, (Lc, 2 * P), 1) < P

    def pair(v, h):
        rows = v.shape[0]
        return jnp.where(first_half[:rows], jnp.broadcast_to(v[:, h:h + 1], (rows, 2 * P)),
                         jnp.broadcast_to(v[:, h + 1:h + 2], (rows, 2 * P)))

    y_parts = []
    for g in range(G):
        Bg = Bm[:, g * N:(g + 1) * N].astype(BF16)
        Cg = Cm[:, g * N:(g + 1) * N].astype(BF16)
        cb = lax.dot_general(Cg, Bg, _NT, preferred_element_type=F32)
        Sg = h_scr[g * GP:(g + 1) * GP, :]
        yoff = lax.dot_general(Cg, Sg.astype(BF16), _NT, preferred_element_type=F32)
        xw_parts = []
        for pr in range(HG // 2):
            h = g * HG + 2 * pr
            xs_pair = xs[:, h * P:(h + 2) * P]
            xs_pair_b = xs_pair.astype(BF16)
            outs = []
            for hh in (h, h + 1):
                seg = acum[:, hh:hh + 1] - acum_t[hh:hh + 1, :]
                dec = jnp.exp(jnp.where(causal, seg, -jnp.inf))
                mix = cb * dec * dt_t[hh:hh + 1, :]
                outs.append(jnp.dot(mix.astype(BF16), xs_pair_b, preferred_element_type=F32))
            ydiag = jnp.where(first_half, outs[0], outs[1])
            y_pair = ydiag + yoff[:, pr * 2 * P:(pr + 1) * 2 * P] * pair(ea, h) + pair(dsk, h) * xs_pair
            y_parts.append(y_pair)
            xw_parts.append(xs_pair * pair(wend, h))
        xw = jnp.concatenate(xw_parts, axis=1).astype(BF16)
        states = lax.dot_general(xw, Bg, _TN, preferred_element_type=F32)
        for hl in range(HG):
            hh = g * HG + hl
            r0 = g * GP + hl * P
            h_scr[r0:r0 + P, :] = edec[:, hh:hh + 1] * Sg[hl * P:(hl + 1) * P, :] + states[hl * P:(hl + 1) * P, :]

    z = z_ref[0]
    gn = DI // G
    outs = []
    for g in range(G):
        parts = y_parts[g * (HG // 2):(g + 1) * (HG // 2)]
        yg = jnp.concatenate(parts, axis=1) if len(parts) > 1 else parts[0]
        yg = yg * _silu(z[:, g * gn:(g + 1) * gn])
        ms = jnp.mean(yg * yg, axis=-1, keepdims=True)
        outs.append(yg * lax.rsqrt(ms + 1e-5))
    y_ref[0] = jnp.concatenate(outs, axis=1) * nw_ref[...]

    @pl.when(c == pl.num_programs(1) - 1)
    def _():
        hn_ref[0] = h_scr[...]


def _ssd(z, xbc, dtr, cbuf, h0, conv_w, conv_b, dt_bias, a_log, d_skip, norm_w, *, Lc, n_valid, G):
    B, L, DI = z.shape
    CD = xbc.shape[-1]
    _, H, P, N = h0.shape
    KW = conv_w.shape[0]
    assert P * 2 == LANES and (H // G) % 2 == 0 and H <= LANES and L % Lc == 0
    padl = lambda v: jnp.pad(v.astype(F32).reshape(1, -1), ((0, 0), (0, LANES - H)))
    seq = lambda b, c: (b, c, 0)
    per_b = lambda b, c: (b, 0, 0)
    fixed = lambda b, c: (0, 0)
    kern = functools.partial(_ssd_kernel, Lc=Lc, H=H, P=P, N=N, G=G, n_valid=n_valid, conv_w=KW)
    y, hn = pl.pallas_call(
        kern, grid=(B, L // Lc),
        in_specs=[pl.BlockSpec((1, Lc, DI), seq), pl.BlockSpec((1, Lc, CD), seq),
                  pl.BlockSpec((1, Lc, LANES), seq), pl.BlockSpec((1, SUBLANES, CD), per_b),
                  pl.BlockSpec((1, H * P, N), per_b),
                  pl.BlockSpec((KW, CD), fixed), pl.BlockSpec((1, CD), fixed),
                  pl.BlockSpec((1, LANES), fixed), pl.BlockSpec((1, LANES), fixed),
                  pl.BlockSpec((1, LANES), fixed), pl.BlockSpec((1, DI), fixed)],
        out_specs=[pl.BlockSpec((1, Lc, DI), seq), pl.BlockSpec((1, H * P, N), per_b)],
        out_shape=[jax.ShapeDtypeStruct((B, L, DI), F32), jax.ShapeDtypeStruct((B, H * P, N), F32)],
        scratch_shapes=[pltpu.VMEM((SUBLANES + Lc, CD), F32), pltpu.VMEM((H * P, N), F32)],
        compiler_params=_cparams("parallel", "arbitrary"), name="ssd",
    )(z, xbc, dtr, cbuf, h0.reshape(B, H * P, N), conv_w, conv_b.reshape(1, CD),
      padl(dt_bias), padl(a_log), padl(d_skip), norm_w.reshape(1, DI))
    return y, hn.reshape(B, H, P, N)


def _mamba_layer(Xp, Xs, dims, state_conv_i, state_ssm_i, params, ln_g, ln_b, alpha):
    Bp, Lp, Bs, Ls = dims
    w_in, conv_w, conv_b, dt_bias, a_log, d_skip, norm_w, w_out = params
    Tp = Bp * Lp
    DI = norm_w.shape[0]
    CD = conv_w.shape[1]
    _, H, P, N = state_ssm_i.shape
    G = (CD - DI) // (2 * N)
    KW = conv_w.shape[0]
    Ts = Bs * Ls
    wz = w_in[:, :DI].astype(BF16)
    wx = w_in[:, DI:DI + CD].astype(BF16)
    wd = _pad_cols(w_in[:, DI + CD:], LANES).astype(BF16)
    wo = w_out.astype(BF16)
    args = (conv_w, conv_b, dt_bias, a_log, d_skip, norm_w)

    xbc_p = _matmul(Xp, wx).reshape(Bp, Lp, CD)
    Lc = SSD_CHUNK if Lp % SSD_CHUNK == 0 else Lp
    yp, hp = _ssd(_matmul(Xp, wz).reshape(Bp, Lp, DI), xbc_p, _matmul(Xp, wd).reshape(Bp, Lp, LANES),
                  jnp.zeros((Bp, SUBLANES, CD), F32), jnp.zeros((Bp, H, P, N), F32), *args,
                  Lc=Lc, n_valid=Lc, G=G)
    conv_p = xbc_p[:, Lp - (KW - 1):]
    Xp = _matmul_ln(yp.reshape(Tp, DI), wo, Xp, ln_g, ln_b, alpha)

    Lsp = -(-Ls // SUBLANES) * SUBLANES
    pad_rows = lambda v: jnp.pad(v[:Ts].reshape(Bs, Ls, -1), ((0, 0), (0, Lsp - Ls), (0, 0)))
    xbc_s = _matmul(Xs, wx)[:Ts].reshape(Bs, Ls, CD)
    cbuf_s = jnp.pad(state_conv_i, ((0, 0), (SUBLANES - (KW - 1), 0), (0, 0)))
    ys, hs = _ssd(pad_rows(_matmul(Xs, wz)), pad_rows(xbc_s), pad_rows(_matmul(Xs, wd)), cbuf_s, state_ssm_i,
                  *args, Lc=Lsp, n_valid=Ls, G=G)
    conv_s = jnp.concatenate([state_conv_i, xbc_s], axis=1)[:, -(KW - 1):]
    ys = jnp.pad(ys[:, :Ls].reshape(Ts, DI), ((0, Xs.shape[0] - Ts), (0, 0)))
    Xs = _matmul_ln(ys, wo, Xs, ln_g, ln_b, alpha)
    return Xp, Xs, hp, hs, conv_p, conv_s


def _topk_rows(s, k, between_rounds=lambda: None):
    n = float(s.shape[0])
    rows = lax.broadcasted_iota(I32, s.shape, 0).astype(F32)
    vals, idxs = [], []
    for _ in range(k):
        m = jnp.max(s, axis=0, keepdims=True)
        i = jnp.min(jnp.where(s == m, rows, n), axis=0, keepdims=True)
        vals.append(m)
        idxs.append(i)
        s = jnp.where(rows == i, -jnp.inf, s)
        between_rounds()
    return vals, idxs


def _peer_route_kernel(q_ref, k_ref, w_ref, i1_scr, i2_scr, g_scr, *, heads, half, topk, n_keys, nj):
    tb = q_ref.shape[0]

    @pl.when(pl.program_id(0) == 0)
    def _():
        i1_scr[...] = jnp.zeros(i1_scr.shape, I32)
        i2_scr[...] = jnp.zeros(i2_scr.shape, I32)
        g_scr[...] = jnp.zeros(g_scr.shape, F32)

    sub = lax.broadcasted_iota(I32, (n_keys, i1_scr.shape[1]), 0)
    pending = list(range(tb))
    n_rounds = heads * 3 * topk
    every = max(1, n_rounds // tb)
    calls = [0]

    def w_token():
        t = pending.pop(0)
        onehot1 = jnp.where(sub == i1_scr[t:t + 1, :], 1.0, 0.0).astype(BF16)
        gated2 = jnp.where(sub == i2_scr[t:t + 1, :], g_scr[t:t + 1, :], 0.0).astype(BF16)
        w_t = lax.dot_general(onehot1, gated2, _NT, preferred_element_type=F32)
        for k in range(n_keys // nj):
            w_ref[0, k, t * nj:(t + 1) * nj, :] = w_t[k * nj:(k + 1) * nj, :]

    def between_rounds():
        calls[0] += 1
        if pending and calls[0] % every == 0:
            w_token()

    i1, i2, gate = _peer_topk_block(q_ref, k_ref, heads=heads, half=half, topk=topk, n_keys=n_keys,
                                    between_rounds=between_rounds)
    while pending:
        w_token()
    i1_scr[...] = i1
    i2_scr[...] = i2
    g_scr[...] = gate


def _peer_topk_block(q_ref, k_ref, *, heads, half, topk, n_keys, between_rounds):
    k1 = k_ref[0]
    k2 = k_ref[1]
    tb = q_ref.shape[0]
    cand_ab = [(a, b) for a in range(topk) for b in range(topk // (a + 1))]
    n_pad = -len(cand_ab) % SUBLANES
    codes, gates = [], []
    for h in range(heads):
        q1 = q_ref[:, (2 * h) * half:(2 * h + 1) * half]
        q2 = q_ref[:, (2 * h + 1) * half:(2 * h + 2) * half]
        s1 = lax.dot_general(k1, q1, _NT, preferred_element_type=F32)
        s2 = lax.dot_general(k2, q2, _NT, preferred_element_type=F32)
        v1, i1 = _topk_rows(s1, topk, between_rounds)
        v2, i2 = _topk_rows(s2, topk, between_rounds)
        cand = jnp.concatenate([v1[a] + v2[b] for a, b in cand_ab] +
                               [jnp.full((n_pad, tb), -jnp.inf, F32)], axis=0)
        code = jnp.concatenate([i1[a] * float(n_keys) + i2[b] for a, b in cand_ab] +
                               [jnp.zeros((n_pad, tb), F32)], axis=0)
        sc, pos = _topk_rows(cand, topk, between_rounds)
        crow = lax.broadcasted_iota(I32, cand.shape, 0).astype(F32)
        codes += [jnp.max(jnp.where(crow == p, code, -1.0), axis=0, keepdims=True) for p in pos]
        ex = jnp.exp(jnp.concatenate(sc, axis=0) - sc[0])
        gates.append(ex / jnp.sum(ex, axis=0, keepdims=True))
    code_t = jnp.concatenate(codes, axis=0).T.astype(I32)
    return code_t // n_keys, code_t % n_keys, jnp.concatenate(gates, axis=0).T


def _gelu_tanh(x):
    hx = 0.5 * x
    inner = x * (GELU_C0 + GELU_C1 * (x * x))
    return hx * jnp.tanh(inner) + hx


def _peer_dense_kernel(x_ref, u_ref, v_ref, w_ref, g_ref, b_ref, o_ref, acc_ref, xb_ref, *, alpha, nj):
    k = pl.program_id(1)
    td = x_ref.shape[0]

    @pl.when(k == 0)
    def _():
        acc_ref[...] = jnp.zeros_like(acc_ref)
        xb_ref[...] = x_ref[...].astype(BF16)

    act = lax.dot_general(xb_ref[...], u_ref[...], _NT, preferred_element_type=F32)
    w = jnp.concatenate([w_ref[0, 0, pl.ds(j, td, stride=nj), :] for j in range(nj)], axis=1)
    coef = (_gelu_tanh(act) * w).astype(BF16)
    acc_ref[...] += jnp.dot(coef, v_ref[...], preferred_element_type=F32)

    @pl.when(k == pl.num_programs(1) - 1)
    def _():
        o_ref[...] = _deepnorm_ln(alpha * x_ref[...] + acc_ref[...], g_ref[...], b_ref[...])


def _peer_layer(X, w_q, sub_keys, u_tab, v_tab, ln_g, ln_b, alpha):
    T, D = X.shape
    _, n_keys, half = sub_keys.shape
    heads = w_q.shape[1] // (2 * half)
    nsel = heads * PEER_TOPK
    assert n_keys == LANES and nsel == LANES and half % LANES == 0
    q = _matmul(X, w_q)

    tb = LANES
    td = _pick(T, (1024, 768, 512, 256, 128))
    nj = SUBLANES
    et = nj * n_keys
    sub_blocks = td // tb
    n_blocks = T // tb
    prev = lambda i: jnp.maximum(i - 1, 0)
    w = pl.pallas_call(
        functools.partial(_peer_route_kernel, heads=heads, half=half, topk=PEER_TOPK, n_keys=n_keys, nj=nj),
        grid=(n_blocks + 1,),
        in_specs=[pl.BlockSpec((tb, q.shape[1]), lambda i: (jnp.minimum(i, n_blocks - 1), 0)),
                  pl.BlockSpec((2, n_keys, half), lambda i: (0, 0, 0))],
        out_specs=pl.BlockSpec((1, n_keys // nj, tb * nj, n_keys),
                               lambda i: (prev(i) // sub_blocks, 0, prev(i) % sub_blocks, 0)),
        out_shape=jax.ShapeDtypeStruct((T // td, n_keys // nj, td * nj, n_keys), F32),
        scratch_shapes=[pltpu.VMEM((tb, nsel), I32), pltpu.VMEM((tb, nsel), I32), pltpu.VMEM((tb, nsel), F32)],
        compiler_params=_cparams("arbitrary"), name="peer_route")(q, sub_keys)

    return pl.pallas_call(
        functools.partial(_peer_dense_kernel, alpha=alpha, nj=nj),
        grid=(T // td, n_keys // nj),
        in_specs=[pl.BlockSpec((td, D), lambda i, k: (i, 0)),
                  pl.BlockSpec((et, D), lambda i, k: (k, 0)),
                  pl.BlockSpec((et, D), lambda i, k: (k, 0)),
                  pl.BlockSpec((1, 1, td * nj, n_keys), lambda i, k: (i, k, 0, 0)),
                  pl.BlockSpec((1, D), lambda i, k: (0, 0)), pl.BlockSpec((1, D), lambda i, k: (0, 0))],
        out_specs=pl.BlockSpec((td, D), lambda i, k: (i, 0)),
        out_shape=jax.ShapeDtypeStruct((T, D), F32),
        scratch_shapes=[pltpu.VMEM((td, D), F32), pltpu.VMEM((td, D), BF16)],
        compiler_params=_cparams("parallel", "arbitrary"), name="peer_dense",
    )(X, u_tab, v_tab, w, ln_g.reshape(1, D), ln_b.reshape(1, D))


def _gather_kernel(pt_ref, pa_ref, pb_ref, new_ref, kc_ref, vc_ref, ks_ref, vs_ref, col_scr, *, n_steps, cw, blk):
    del pt_ref
    p = pl.program_id(1)
    ps = pa_ref.shape[1]
    per_page = ps // blk

    @pl.when(p < n_steps)
    def _():
        for j, page in enumerate((pa_ref, pb_ref)):
            n_tiles = 2 * cw // LANES
            for c in range(n_tiles):
                col_scr[j * n_tiles + c] = page[0, :, c * LANES:(c + 1) * LANES]
            for r, ref in enumerate((kc_ref, vc_ref)):
                for l in range(blk):
                    for c in range(cw // LANES):
                        ref[0, j * per_page:(j + 1) * per_page, l * cw + c * LANES:l * cw + (c + 1) * LANES] = \
                            col_scr[j * n_tiles + r * (cw // LANES) + c, pl.ds(l, per_page, stride=blk), :]
            ks_ref[0, j * ps:(j + 1) * ps, :] = page[0, :, 2 * cw:3 * cw].astype(BF16)
            vs_ref[0, j * ps:(j + 1) * ps, :] = page[0, :, 3 * cw:4 * cw].astype(BF16)

    @pl.when(p == n_steps)
    def _():
        new = new_ref[0]
        n_new = new.shape[0]
        for r, ref in enumerate((kc_ref, vc_ref)):
            ref[0] = jnp.zeros(ref.shape[1:], F32)
            for l in range(n_new):
                ref[0, 0:1, l * cw:(l + 1) * cw] = new[l:l + 1, r * cw:(r + 1) * cw]
        for r, ref in ((2, ks_ref), (3, vs_ref)):
            rows = jnp.concatenate([new[:, r * cw:(r + 1) * cw], jnp.zeros((2 * ps - n_new, cw), F32)], axis=0)
            ref[0] = rows.astype(BF16)


def _gather_pages(pages, page_table, new_rows, blk):
    NP, PS, C = pages.shape
    B, n_pages = page_table.shape
    assert n_pages % 2 == 0 and PS % blk == 0 and (2 * PS // blk) % SUBLANES == 0 and new_rows.shape[1] <= blk
    cw = C // 4
    n_steps = n_pages // 2
    L = (n_pages + 2) * PS
    nb = 2 * PS // blk
    page_spec = lambda j: pl.BlockSpec(
        (1, PS, C), lambda b, p, pt: (pt[b, jnp.minimum(2 * p + j, n_pages - 2 + j)], 0, 0))
    cmp_spec = pl.BlockSpec((1, nb, blk * cw), lambda b, p, pt: (b, p, 0))
    sel_spec = pl.BlockSpec((1, 2 * PS, cw), lambda b, p, pt: (b, p, 0))
    return pl.pallas_call(
        functools.partial(_gather_kernel, n_steps=n_steps, cw=cw, blk=blk),
        grid_spec=pltpu.PrefetchScalarGridSpec(
            num_scalar_prefetch=1, grid=(B, n_steps + 1),
            in_specs=[page_spec(0), page_spec(1),
                      pl.BlockSpec((1, new_rows.shape[1], C), lambda b, p, pt: (b, 0, 0))],
            out_specs=[cmp_spec, cmp_spec, sel_spec, sel_spec],
            scratch_shapes=[pltpu.VMEM((2 * (2 * cw // LANES), PS, LANES), F32)]),
        out_shape=[jax.ShapeDtypeStruct((B, L // blk, blk * cw), F32)] * 2 +
                  [jax.ShapeDtypeStruct((B, L, cw), BF16)] * 2,
        compiler_params=_cparams("parallel", "arbitrary"), name="kv_gather")(page_table, pages, pages, new_rows)


def _compress_blocks(xb, w_c, pos_c, nc_pad):
    B, n, K = xb.shape
    blk, dh, _ = w_c.shape
    C = K // blk
    G = C // dh
    w_bd = jnp.einsum('lde,gh->lgdhe', w_c, jnp.eye(G, dtype=w_c.dtype)).reshape(K, C)
    xbias = jnp.broadcast_to(pos_c[:, None, :], (blk, G, dh)).reshape(1, K)
    out = _matmul(xb.reshape(B * n, K), w_bd.astype(BF16), xbias=xbias, tm_cap=128)
    return jnp.pad(out.reshape(B, n, C), ((0, 0), (0, nc_pad - n), (0, 0)))


def _compress(x, w_c, pos_c, nc_pad):
    B, L, C = x.shape
    blk = w_c.shape[0]
    return _compress_blocks(x.reshape(B, L // blk, blk * C), w_c, pos_c, nc_pad)


def _lane_tiles(x):
    return [x[:, j * LANES:(j + 1) * LANES] for j in range(x.shape[1] // LANES)]


def _row_max(x):
    return jnp.max(functools.reduce(jnp.maximum, _lane_tiles(x)), axis=-1, keepdims=True)


def _row_sum(x):
    return jnp.sum(functools.reduce(jnp.add, _lane_tiles(x)), axis=-1, keepdims=True)


def _split3_dot(x, m):
    x1 = x.astype(BF16)
    r1 = x - x1.astype(F32)
    x2 = r1.astype(BF16)
    x3 = (r1 - x2.astype(F32)).astype(BF16)
    d = lambda a: jnp.dot(a, m, preferred_element_type=F32)
    return d(x1) + d(x2) + d(x3)


def _nsa_staged_kernel(q_ref, gt_ref, kc_ref, vc_ref, ks_ref, vs_ref, kw_ref, vw_ref, ex_ref, o_ref, *,
                       seqs, nq, t0, tk, n_cmp, cmp_blk, n_sel, kvh, hg, dh, slopes, win_rows, w0):
    q0 = t0 + pl.program_id(1) * nq
    R = hg * nq
    NC = kc_ref.shape[1]
    Lk = ks_ref.shape[1]
    NSP = ex_ref.shape[0]
    assert tk == Lk
    qscale = dh ** -0.5 * LOG2E
    slopes2 = tuple(s * LOG2E for s in slopes)
    units = [(s, g) for s in range(seqs) for g in range(kvh)]
    col = lambda g: (g // 2) * 2 * dh

    def tq_like(shape):
        return q0 + lax.broadcasted_iota(I32, shape, 0)

    def attend(keys, vals, distms):
        s_alls = [lax.dot_general(qgs[u], keys[u], _NT, preferred_element_type=F32) for u in range(len(units))]
        s2s = [[s_alls[u][h * nq:(h + 1) * nq] - slopes2[g * hg + h] * distms[u] for h in range(hg)]
               for u, (_, g) in enumerate(units)]
        ms = [[_row_max(x) for x in row] for row in s2s]
        es = [[jnp.exp2(x - jnp.where(m == -jnp.inf, 0.0, m)) for x, m in zip(xr, mr)] for xr, mr in zip(s2s, ms)]
        ls = [[_row_sum(e) for e in row] for row in es]
        ps = [[e / jnp.maximum(l, 1e-30) for e, l in zip(er, lr)] for er, lr in zip(es, ls)]
        os = [jnp.dot(jnp.concatenate(ps[u], axis=0).astype(BF16), vals[u], preferred_element_type=F32)
              for u in range(len(units))]
        return ps, os

    n_idx = lax.broadcasted_iota(I32, (nq, NC), 1)
    dist_c = tq_like((nq, NC)) - (n_idx * cmp_blk + (cmp_blk - 1))
    distm_c = jnp.where((dist_c >= 0) & (n_idx < n_cmp), dist_c.astype(F32), jnp.inf)
    dist_w = tq_like((nq, win_rows)) - (w0 + lax.broadcasted_iota(I32, (nq, win_rows), 1))
    distm_w = jnp.where((dist_w >= 0) & (dist_w < WINDOW), dist_w.astype(F32), jnp.inf)
    dist_s = tq_like((nq, Lk)) - lax.broadcasted_iota(I32, (nq, Lk), 1)
    per_sel = SEL_BLOCK // cmp_blk
    pair = jnp.where(lax.broadcasted_iota(I32, (NC, NSP), 0) // per_sel ==
                     lax.broadcasted_iota(I32, (NC, NSP), 1), 1.0, 0.0).astype(BF16)
    blk = lax.broadcasted_iota(I32, (nq, NSP), 1)
    tqs = tq_like((nq, NSP))
    cur = tqs // SEL_BLOCK
    forced = (blk == 0) | (blk == cur) | (blk == cur - 1)
    sel_valid = blk * SEL_BLOCK <= tqs

    qgs = [(jnp.concatenate([q_ref[s, :, (g * hg + h) * 2 * dh:(g * hg + h + 1) * 2 * dh] for h in range(hg)],
                            axis=0) * qscale).astype(BF16) for s, g in units]
    tile = lambda ref, s, g: ref[s, :, col(g):col(g) + 2 * dh]
    p_cmp, o_cmp = attend([tile(kc_ref, s, g).astype(BF16) for s, g in units],
                          [tile(vc_ref, s, g).astype(BF16) for s, g in units], [distm_c] * len(units))
    _, o_win = attend([tile(kw_ref, s, g) for s, g in units], [tile(vw_ref, s, g) for s, g in units],
                      [distm_w] * len(units))

    psums = [functools.reduce(jnp.add, row) for row in p_cmp]
    imps = [_split3_dot(x, pair) for x in psums]
    scores = [jnp.where(blk < n_sel, jnp.where(forced, FORCE_SCORE, jnp.where(sel_valid, imp, -1.0)), -2.0)
              for imp in imps]
    selm = _top_blocks(jnp.concatenate(scores, axis=0), min(N_SEL, n_sel), NSP)
    selb = selm.astype(BF16)
    expand = ex_ref[...]
    selxs = [jnp.dot(selb[u * nq:(u + 1) * nq], expand, preferred_element_type=F32) for u in range(len(units))]
    distm_s = [jnp.where((dist_s >= 0) & (x > 0.5), dist_s.astype(F32), jnp.inf) for x in selxs]
    _, o_sel = attend([tile(ks_ref, s, g) for s, g in units], [tile(vs_ref, s, g) for s, g in units], distm_s)

    lane = lax.broadcasted_iota(I32, (nq, 2 * dh), 1)
    for s in range(seqs):
        gates = jax.nn.sigmoid(gt_ref[s])
        for g in range(kvh):
            u = s * kvh + g
            lo = (g % 2) * dh
            keep = (lane >= lo) & (lane < lo + dh)
            for h in range(hg):
                hd = g * hg + h
                rows = slice(h * nq, (h + 1) * nq)
                o = (gates[:, 3 * hd:3 * hd + 1] * o_cmp[u][rows] + gates[:, 3 * hd + 1:3 * hd + 2] * o_sel[u][rows] +
                     gates[:, 3 * hd + 2:3 * hd + 3] * o_win[u][rows])
                o_ref[s, :, hd * 2 * dh:(hd + 1) * 2 * dh] = jnp.where(keep, o, 0.0)


def _top_blocks(score, n_top, nsp):
    if score.shape[0] % LANES == 0 and nsp == LANES:
        st = score.T
        brow = lax.broadcasted_iota(I32, st.shape, 0).astype(F32)
        sel_t = jnp.zeros(st.shape, F32)
        for _ in range(n_top):
            m = jnp.max(st, axis=0, keepdims=True)
            idx = jnp.min(jnp.where(st == m, brow, float(nsp)), axis=0, keepdims=True)
            hit = brow == idx
            sel_t = jnp.where(hit, 1.0, sel_t)
            st = jnp.where(hit, -jnp.inf, st)
        return sel_t.T
    bcol = lax.broadcasted_iota(I32, score.shape, 1)
    selm = jnp.zeros(score.shape, F32)
    for _ in range(n_top):
        m = jnp.max(score, axis=-1, keepdims=True)
        idx = jnp.min(jnp.where(score == m, bcol, nsp), axis=-1, keepdims=True)
        hit = bcol == idx
        selm = jnp.where(hit, 1.0, selm)
        score = jnp.where(hit, -jnp.inf, score)
    return selm


def _nsa_tiled_kernel(q_ref, gt_ref, kc_ref, vc_ref, ks_ref, vs_ref, kw_ref, vw_ref, ex_ref, o_ref,
                      m_scr, l_scr, acc_scr, flag_ref, *,
                      seqs, nq, t0, tk, n_cmp, cmp_blk, n_sel, kvh, hg, dh, slopes, win_rows, w0):
    assert seqs == 1
    i = pl.program_id(1)
    q0 = t0 + i * nq
    R = hg * nq
    NC = kc_ref.shape[1]
    Lk = ks_ref.shape[1]
    NSP = ex_ref.shape[0]
    qscale = dh ** -0.5 * LOG2E
    slopes2 = tuple(s * LOG2E for s in slopes)
    gates = jax.nn.sigmoid(gt_ref[0])

    def tq_like(shape):
        return q0 + lax.broadcasted_iota(I32, shape, 0)

    wstart = pl.multiple_of(jnp.maximum(q0 - WINDOW, 0), nq)

    def col(g):
        return (g // 2) * 2 * dh

    groups = range(kvh)

    def attend(keys, vals, distm, want_p):
        s_alls = [lax.dot_general(qgs[g], keys[g], _NT, preferred_element_type=F32) for g in groups]
        s2s = [[s_alls[g][h * nq:(h + 1) * nq] - slopes2[g * hg + h] * distm for h in range(hg)] for g in groups]
        ms = [[_row_max(x) for x in row] for row in s2s]
        es = [[jnp.exp2(x - jnp.where(m == -jnp.inf, 0.0, m)) for x, m in zip(xr, mr)] for xr, mr in zip(s2s, ms)]
        ls = [[jnp.maximum(_row_sum(e), 1e-30) for e in row] for row in es]
        if want_p:
            ps = [[e / l for e, l in zip(er, lr)] for er, lr in zip(es, ls)]
            os = [jnp.dot(jnp.concatenate(ps[g], axis=0).astype(BF16), vals[g], preferred_element_type=F32)
                  for g in groups]
            return ps, os
        os = [jnp.dot(jnp.concatenate(es[g], axis=0).astype(BF16), vals[g], preferred_element_type=F32) /
              jnp.concatenate(ls[g], axis=0) for g in groups]
        return None, os

    n_idx = lax.broadcasted_iota(I32, (nq, NC), 1)
    dist_c = tq_like((nq, NC)) - (n_idx * cmp_blk + (cmp_blk - 1))
    distm_c = jnp.where((dist_c >= 0) & (n_idx < n_cmp), dist_c.astype(F32), jnp.inf)
    dist_w = tq_like((nq, win_rows)) - (w0 + wstart + lax.broadcasted_iota(I32, (nq, win_rows), 1))
    distm_w = jnp.where((dist_w >= 0) & (dist_w < WINDOW), dist_w.astype(F32), jnp.inf)
    per_sel = SEL_BLOCK // cmp_blk
    pair = jnp.where(lax.broadcasted_iota(I32, (NC, NSP), 0) // per_sel ==
                     lax.broadcasted_iota(I32, (NC, NSP), 1), 1.0, 0.0).astype(BF16)
    blk = lax.broadcasted_iota(I32, (nq, NSP), 1)
    tqs = tq_like((nq, NSP))
    cur = tqs // SEL_BLOCK
    forced = (blk == 0) | (blk == cur) | (blk == cur - 1)
    sel_valid = blk * SEL_BLOCK <= tqs

    qgs = [(jnp.concatenate([q_ref[0, :, (g * hg + h) * 2 * dh:(g * hg + h + 1) * 2 * dh] for h in range(hg)],
                            axis=0) * qscale).astype(BF16) for g in groups]
    p_cmps, o_cmps = attend([kc_ref[0, :, col(g):col(g) + 2 * dh].astype(BF16) for g in groups],
                            [vc_ref[0, :, col(g):col(g) + 2 * dh].astype(BF16) for g in groups], distm_c, True)
    _, o_wins = attend([kw_ref[0, pl.ds(wstart, win_rows), col(g):col(g) + 2 * dh] for g in groups],
                       [vw_ref[0, pl.ds(wstart, win_rows), col(g):col(g) + 2 * dh] for g in groups], distm_w, False)
    imps = [_split3_dot(functools.reduce(jnp.add, p_cmps[g]), pair) for g in groups]
    scores = [jnp.where(blk < n_sel, jnp.where(forced, FORCE_SCORE, jnp.where(sel_valid, imp, -1.0)), -2.0)
              for imp in imps]

    selm = _top_blocks(jnp.concatenate(scores, axis=0), min(N_SEL, n_sel), NSP)
    selb = selm.astype(BF16)

    def sel_tile_group(kt, g):
        k0 = pl.multiple_of(kt * tk, tk)
        c0 = col(g)
        dist = tq_like((nq, tk)) - (k0 + lax.broadcasted_iota(I32, (nq, tk), 1))
        selx = jnp.dot(selb[g * nq:(g + 1) * nq], ex_ref[:, pl.ds(k0, tk)], preferred_element_type=F32)
        distm = jnp.where((dist >= 0) & (selx > 0.5), dist.astype(F32), jnp.inf)
        s_t = lax.dot_general(qgs[g], ks_ref[0, pl.ds(k0, tk), c0:c0 + 2 * dh], _NT,
                              preferred_element_type=F32)
        m_run = m_scr[g]
        l_run = l_scr[g]
        heads = range(hg)
        rows = [slice(h * nq, (h + 1) * nq) for h in heads]
        s2s = [s_t[rows[h]] - slopes2[g * hg + h] * distm for h in heads]
        tmax = [_row_max(x) for x in s2s]
        m_new = [jnp.maximum(m_run[rows[h]], tmax[h]) for h in heads]
        m_safe = [jnp.where(m == -jnp.inf, 0.0, m) for m in m_new]
        alphas = [jnp.exp2(m_run[rows[h]] - m_safe[h]) for h in heads]
        ps = [jnp.exp2(s2s[h] - jnp.concatenate([m_safe[h]] * (tk // LANES), axis=1)) for h in heads]
        l_new = [alphas[h] * l_run[rows[h]] + _row_sum(ps[h]) for h in heads]
        pv = jnp.dot(jnp.concatenate(ps, axis=0).astype(BF16), vs_ref[0, pl.ds(k0, tk), c0:c0 + 2 * dh],
                     preferred_element_type=F32)
        m_scr[g] = jnp.concatenate(m_new, axis=0)
        l_scr[g] = jnp.concatenate(l_new, axis=0)
        acc_scr[g] = jnp.concatenate(alphas, axis=0) * acc_scr[g] + pv

    n_flag = Lk // tk
    blocks_per_tile = tk // SEL_BLOCK
    lane_b = lax.broadcasted_iota(I32, (1, NSP), 1)
    for g in range(kvh):
        any_q = jnp.max(selm[g * nq:(g + 1) * nq], axis=0, keepdims=True)
        for kt in range(n_flag):
            in_tile = (lane_b >= kt * blocks_per_tile) & (lane_b < (kt + 1) * blocks_per_tile)
            flag_ref[g * n_flag + kt] = (jnp.max(jnp.where(in_tile, any_q, 0.0)) > 0.5).astype(I32)
    m_scr[...] = jnp.full(m_scr.shape, -jnp.inf, F32)
    l_scr[...] = jnp.zeros(l_scr.shape, F32)
    acc_scr[...] = jnp.zeros(acc_scr.shape, F32)

    def visit(kt, carry):
        for g in range(kvh):
            @pl.when(flag_ref[g * n_flag + kt] > 0)
            def _():
                sel_tile_group(kt, g)
        return carry

    lax.fori_loop(0, (q0 + nq + tk - 1) // tk, visit, 0)

    lane = lax.broadcasted_iota(I32, (nq, 2 * dh), 1)
    for g in range(kvh):
        lo = (g % 2) * dh
        keep = (lane >= lo) & (lane < lo + dh)
        o_sel = acc_scr[g] / jnp.maximum(l_scr[g], 1e-30)
        for h in range(hg):
            hd = g * hg + h
            rows = slice(h * nq, (h + 1) * nq)
            o = (gates[:, 3 * hd:3 * hd + 1] * o_cmps[g][rows] + gates[:, 3 * hd + 1:3 * hd + 2] * o_sel[rows] +
                 gates[:, 3 * hd + 2:3 * hd + 3] * o_wins[g][rows])
            o_ref[0, :, hd * 2 * dh:(hd + 1) * 2 * dh] = jnp.where(keep, o, 0.0)


def _nsa_attend(q, gt, kc, vc, ks, vs, kw, vw, cols, *, nq, t0, tk, n_cmp, cmp_blk, n_sel, kvh, dh,
                win_rows, w0, causal_tiles, seqs=1):
    B, Lq, QW = q.shape
    H = QW // (2 * dh)
    hg = H // kvh
    cw = kvh * dh
    Lk = ks.shape[1]
    NC = kc.shape[1]
    nsp = LANES
    assert n_sel <= nsp and Lk % tk == 0 and Lq % nq == 0 and dh * 2 == LANES
    expand = (lax.broadcasted_iota(I32, (nsp, Lk), 1) // SEL_BLOCK ==
              lax.broadcasted_iota(I32, (nsp, Lk), 0)).astype(BF16)
    slopes = tuple(2.0 ** (-8.0 * (h + 1) / H) for h in range(H))
    assert B % seqs == 0
    cfg = dict(seqs=seqs, nq=nq, t0=t0, tk=tk, n_cmp=n_cmp, cmp_blk=cmp_blk, n_sel=n_sel, kvh=kvh, hg=hg, dh=dh,
               slopes=slopes, win_rows=win_rows, w0=w0)
    R = hg * nq
    if causal_tiles:
        kern = functools.partial(_nsa_tiled_kernel, **cfg)
        scratch = [pltpu.VMEM((kvh, R, LANES), F32), pltpu.VMEM((kvh, R, LANES), F32),
                   pltpu.VMEM((kvh, R, 2 * dh), F32), pltpu.SMEM((kvh * (Lk // tk),), I32)]
    else:
        assert win_rows == kw.shape[1]
        kern = functools.partial(_nsa_staged_kernel, **cfg)
        scratch = []
    qmap = lambda b, i: (b, i, 0)
    kv_spec = lambda arr, c: pl.BlockSpec((seqs, arr.shape[1], cw), lambda b, i, c=c: (b, 0, c))
    return pl.pallas_call(
        kern, grid=(B // seqs, Lq // nq),
        in_specs=[pl.BlockSpec((seqs, nq, QW), qmap), pl.BlockSpec((seqs, nq, LANES), qmap),
                  kv_spec(kc, cols[0]), kv_spec(vc, cols[1]), kv_spec(ks, cols[2]), kv_spec(vs, cols[3]),
                  kv_spec(kw, cols[4]), kv_spec(vw, cols[5]),
                  pl.BlockSpec((nsp, Lk), lambda b, i: (0, 0))],
        out_specs=pl.BlockSpec((seqs, nq, QW), qmap),
        out_shape=jax.ShapeDtypeStruct((B, Lq, QW), F32),
        scratch_shapes=scratch,
        compiler_params=_cparams("parallel", "arbitrary"), name="nsa_attend")(q, gt, kc, vc, ks, vs, kw, vw, expand)


def _head_tiles(w, H, kvh, dh, axis):
    hg = H // kvh
    parts = []
    for h in range(H):
        sl = [slice(None)] * w.ndim
        sl[axis] = slice(h * dh, (h + 1) * dh)
        piece = w[tuple(sl)]
        zero = jnp.zeros_like(piece)
        parts += [zero, piece] if ((h // hg) % 2) else [piece, zero]
    return jnp.concatenate(parts, axis=axis)


def kernel(x_prompt, x_sample, cache_kv_pages, cache_win, state_ssm, state_conv, page_table, ln_g, ln_b, m_w_in, m_conv_w, m_conv_b, m_dt_bias, m_a_log, m_d_skip, m_norm_w, m_w_out, w_kv_shared, w_cmp, pos_cmp, nsa_w_in, nsa_w_out, peer_w_q, peer_sub_keys, peer_u, peer_v):
    Bp, Lp, D = x_prompt.shape
    Bs, Ls, _ = x_sample.shape
    depth = ln_g.shape[0]
    n_a = m_w_in.shape[0]
    alpha = (2 * depth) ** DEPTH_ALPHA_POW
    Tp, Ts = Bp * Lp, Bs * Ls
    assert Tp % LANES == 0
    Tsp = -(-Ts // LANES) * LANES
    Xp = x_prompt.reshape(Tp, D)
    Xs = jnp.pad(x_sample.reshape(Ts, D), ((0, Tsp - Ts), (0, 0)))
    dims = (Bp, Lp, Bs, Ls)

    NP, PS, n_rows, kvh, dh = cache_kv_pages.shape
    cw = kvh * dh
    n_pages = page_table.shape[1]
    past_len = n_pages * PS
    win_cache = cache_win.shape[1]
    cmp_blk = w_cmp.shape[1]
    H = nsa_w_in.shape[2] // (dh + 3)

    ssm_p, ssm_s, conv_p, conv_s = [], [], [], []
    for i in range(depth):
        if i < n_a:
            mp = (m_w_in[i], m_conv_w[i], m_conv_b[i], m_dt_bias[i], m_a_log[i], m_d_skip[i], m_norm_w[i], m_w_out[i])
            Xp, Xs, hp, hs, cp, cs = _mamba_layer(Xp, Xs, dims, state_conv[i], state_ssm[i], mp,
                                                  ln_g[i, 0], ln_b[i, 0], alpha)
            ssm_p.append(hp)
            ssm_s.append(hs)
            conv_p.append(cp)
            conv_s.append(cs)
        else:
            if i == n_a:
                wkv = w_kv_shared.astype(BF16)
                kv_p = _matmul(Xp, wkv).reshape(Bp, Lp, 6 * cw)
                kv_s = _matmul(Xs, wkv)[:Ts].reshape(Bs, Ls, 6 * cw)
                kvb_p = kv_p.astype(BF16)
                n_sel_p = -(-Lp // SEL_BLOCK)
                Lpp = n_sel_p * SEL_BLOCK
                padp = lambda v: jnp.pad(v, ((0, 0), (0, Lpp - Lp), (0, 0)))
                nc_p = -(-(Lpp // cmp_blk) // LANES) * LANES
                kc_p = _compress(padp(kv_p[:, :, 0:cw]), w_cmp[0], pos_cmp[0], nc_p)
                vc_p = _compress(padp(kv_p[:, :, cw:2 * cw]), w_cmp[1], pos_cmp[1], nc_p)
                tk_p = _pick(Lp, (512, 256, 128))
                new_rows = jnp.pad(kv_s[:, :, :4 * cw], ((0, 0), (0, SUBLANES - Ls), (0, 0)))
                kcb_s, vcb_s, ks_s, vs_s = _gather_pages(cache_kv_pages.reshape(NP, PS, n_rows * cw), page_table,
                                                         new_rows, cmp_blk)
                Lk_s = ks_s.shape[1]
                n_sel_s = -(-(past_len + Ls) // SEL_BLOCK)
                nc_s = -(-(Lk_s // cmp_blk) // LANES) * LANES
                kc_s = _compress_blocks(kcb_s, w_cmp[0], pos_cmp[0], nc_s)
                vc_s = _compress_blocks(vcb_s, w_cmp[1], pos_cmp[1], nc_s)
                win_full_s = jnp.concatenate([cache_win, kv_s[:, :, 4 * cw:].reshape(Bs, Ls, 2, kvh, dh)], axis=1)
                Lw = win_cache + Ls
                Lwp = -(-Lw // LANES) * LANES
                wflat = jnp.pad(win_full_s.reshape(Bs, Lw, 2 * cw), ((0, 0), (0, Lwp - Lw), (0, 0))).astype(BF16)
            j = i - n_a
            wq = _head_tiles(nsa_w_in[j][:, :H * dh], H, kvh, dh, axis=1).astype(BF16)
            wg = _pad_cols(nsa_w_in[j][:, H * dh:], LANES).astype(BF16)
            wo = _head_tiles(nsa_w_out[j], H, kvh, dh, axis=0).astype(BF16)
            QW = wq.shape[1]
            o_p = _nsa_attend(_matmul(Xp, wq).reshape(Bp, Lp, QW), _matmul(Xp, wg).reshape(Bp, Lp, LANES),
                              kc_p, vc_p, kvb_p, kvb_p, kvb_p, kvb_p, (0, 0, 2, 3, 4, 5),
                              nq=Q_BLOCK, t0=0, tk=tk_p, n_cmp=Lpp // cmp_blk, cmp_blk=cmp_blk, n_sel=n_sel_p,
                              kvh=kvh, dh=dh, win_rows=min(WINDOW + Q_BLOCK, Lp), w0=0, causal_tiles=True)
            Xp = _matmul_ln(o_p.reshape(Tp, QW), wo, Xp, ln_g[i, 0], ln_b[i, 0], alpha)
            pad_q = lambda v: jnp.pad(v[:Ts].reshape(Bs, Ls, -1), ((0, 0), (0, SUBLANES - Ls), (0, 0)))
            o_s = _nsa_attend(pad_q(_matmul(Xs, wq)), pad_q(_matmul(Xs, wg)), kc_s, vc_s, ks_s, vs_s, wflat, wflat,
                              (0, 0, 0, 0, 0, 1),
                              nq=SUBLANES, t0=past_len, tk=Lk_s, n_cmp=-(-(past_len + Ls) // SEL_BLOCK) * (SEL_BLOCK // cmp_blk),
                              cmp_blk=cmp_blk, n_sel=n_sel_s, kvh=kvh, dh=dh, win_rows=Lwp,
                              w0=past_len - win_cache, causal_tiles=False, seqs=_pick(Bs, (4, 2, 1)))
            o_s = jnp.pad(o_s[:, :Ls].reshape(Ts, QW), ((0, Tsp - Ts), (0, 0)))
            Xs = _matmul_ln(o_s, wo, Xs, ln_g[i, 0], ln_b[i, 0], alpha)
        peer = (peer_w_q[i].astype(BF16), peer_sub_keys[i], peer_u[i].astype(BF16), peer_v[i].astype(BF16),
                ln_g[i, 1], ln_b[i, 1], alpha)
        Xp = _peer_layer(Xp, *peer)
        Xs = _peer_layer(Xs, *peer)

    rows_p = kv_p[:, :, :4 * cw].reshape(Bp, Lp, n_rows, kvh, dh)
    rows_s = kv_s[:, :, :4 * cw].reshape(Bs, Ls, n_rows, kvh, dh)
    win_p = kv_p[:, :, 4 * cw:].reshape(Bp, Lp, 2, kvh, dh)[:, -min(WINDOW, Lp):]
    return (Xp.reshape(Bp, Lp, D), Xs[:Ts].reshape(Bs, Ls, D), rows_p, rows_s, win_p,
            win_full_s[:, -win_cache:], jnp.stack(ssm_p), jnp.stack(ssm_s), jnp.stack(conv_p), jnp.stack(conv_s))
```

```python
import functools
import math

import jax
import jax.numpy as jnp
from jax import lax
from jax.experimental import pallas as pl
from jax.experimental.pallas import tpu as pltpu

F32 = jnp.float32
BF16 = jnp.bfloat16
I32 = jnp.int32

DEPTH_ALPHA_POW = 0.25
LN_EPS = 1e-5
SEL_BLOCK = 64
N_SEL = 16
WINDOW = 512
FORCE_SCORE = 1.0e4
PEER_TOPK = 16
SSD_CHUNK = 128
Q_BLOCK = 128
GELU_C0 = math.sqrt(2.0 / math.pi)
GELU_C1 = 0.044715 * GELU_C0
LOG2E = 1.0 / math.log(2.0)
MAX_FULL_N = 3072

LANES = 128
SUBLANES = 8
VMEM_LIMIT_BYTES = 56 * 1024 * 1024

_NT = (((1,), (1,)), ((), ()))
_TN = (((0,), (0,)), ((), ()))
_HI = lax.Precision.HIGHEST


def _cparams(*sem):
    return pltpu.CompilerParams(dimension_semantics=sem, vmem_limit_bytes=VMEM_LIMIT_BYTES)


def _pick(n, cands):
    for c in cands:
        if n % c == 0:
            return c
    raise ValueError(f"no tile in {cands} divides {n}")


def _pad_cols(w, n):
    return jnp.pad(w, ((0, 0), (0, n - w.shape[1])))


def _mm_kernel(x_ref, w_ref, o_ref):
    o_ref[...] = jnp.dot(x_ref[...].astype(BF16), w_ref[...],
                         preferred_element_type=F32).astype(o_ref.dtype)


def _mm_bias_kernel(x_ref, xb_ref, w_ref, o_ref):
    x = (x_ref[...] + xb_ref[...]).astype(BF16)
    o_ref[...] = jnp.dot(x, w_ref[...], preferred_element_type=F32).astype(o_ref.dtype)


def _matmul(x, w, *, xbias=None, out_dtype=F32, tm_cap=512):
    M, K = x.shape
    N = w.shape[1]
    tm = _pick(M, tuple(t for t in (512, 256, 128, 64, 32, 16, 8) if t <= tm_cap))
    tn = N if N <= MAX_FULL_N else _pick(N, (1024, 512, 256, 128))
    x_spec = pl.BlockSpec((tm, K), lambda j, i: (i, 0))
    w_spec = pl.BlockSpec((K, tn), lambda j, i: (0, j))
    o_spec = pl.BlockSpec((tm, tn), lambda j, i: (i, j))
    if xbias is None:
        body, specs, args = _mm_kernel, [x_spec, w_spec], (x, w)
    else:
        b_spec = pl.BlockSpec((1, K), lambda j, i: (0, 0))
        body, specs, args = _mm_bias_kernel, [x_spec, b_spec, w_spec], (x, xbias, w)
    return pl.pallas_call(
        body, grid=(N // tn, M // tm), in_specs=specs, out_specs=o_spec,
        out_shape=jax.ShapeDtypeStruct((M, N), out_dtype),
        compiler_params=_cparams("parallel", "parallel"), name="proj")(*args)


def _deepnorm_ln(v, g, b):
    mu = jnp.mean(v, axis=-1, keepdims=True)
    d = v - mu
    var = jnp.mean(d * d, axis=-1, keepdims=True)
    return d * lax.rsqrt(var + LN_EPS) * g + b


def _mm_ln_kernel(x_ref, w_ref, r_ref, g_ref, b_ref, o_ref, *, alpha):
    y = jnp.dot(x_ref[...].astype(BF16), w_ref[...], preferred_element_type=F32)
    o_ref[...] = _deepnorm_ln(alpha * r_ref[...] + y, g_ref[...], b_ref[...])


def _matmul_ln(x, w, resid, g, b, alpha):
    M, K = x.shape
    D = w.shape[1]
    tm = _pick(M, (512, 256, 128, 64, 32, 16, 8))
    row = lambda i: (i, 0)
    fixed = lambda i: (0, 0)
    return pl.pallas_call(
        functools.partial(_mm_ln_kernel, alpha=alpha), grid=(M // tm,),
        in_specs=[pl.BlockSpec((tm, K), row), pl.BlockSpec((K, D), fixed),
                  pl.BlockSpec((tm, D), row), pl.BlockSpec((1, D), fixed), pl.BlockSpec((1, D), fixed)],
        out_specs=pl.BlockSpec((tm, D), row),
        out_shape=jax.ShapeDtypeStruct((M, D), F32),
        compiler_params=_cparams("parallel"), name="proj_ln")(x, w, resid, g.reshape(1, D), b.reshape(1, D))


def _softplus(x):
    return jnp.maximum(x, 0.0) + jnp.log1p(jnp.exp(-jnp.abs(x)))


def _silu(x):
    return x * jax.nn.sigmoid(x)


def _ssd_kernel(z_ref, xbc_ref, dtr_ref, cbuf_ref, h0_ref, cw_ref, cb_ref, dtb_ref, alog_ref,
                dsk_ref, nw_ref, y_ref, hn_ref, xf_scr, h_scr, *, Lc, H, P, N, G, n_valid, conv_w):
    c = pl.program_id(1)
    DI = H * P
    HG = H // G
    GP = HG * P
    tail = xf_scr.shape[0] - Lc

    @pl.when(c == 0)
    def _():
        xf_scr[0:tail, :] = cbuf_ref[0]
        h_scr[...] = h0_ref[0]

    xf_scr[tail:tail + Lc, :] = xbc_ref[0]
    conv = cb_ref[...]
    for k in range(conv_w):
        conv = conv + xf_scr[pl.ds(tail - (conv_w - 1) + k, Lc), :] * cw_ref[k:k + 1, :]
    conv = _silu(conv)
    xf_scr[0:tail, :] = xf_scr[pl.ds(Lc, tail), :]

    xs = conv[:, :DI]
    Bm = conv[:, DI:DI + G * N]
    Cm = conv[:, DI + G * N:]

    lane = lax.broadcasted_iota(I32, (Lc, LANES), 1)
    rowi = lax.broadcasted_iota(I32, (Lc, LANES), 0)
    dt = jnp.where((lane < H) & (rowi < n_valid), _softplus(dtr_ref[0] + dtb_ref[...]), 0.0)
    a = dt * (-jnp.exp(alog_ref[...]))
    ti = lax.broadcasted_iota(I32, (Lc, Lc), 0)
    si = lax.broadcasted_iota(I32, (Lc, Lc), 1)
    causal = ti >= si
    tril = jnp.where(causal, 1.0, 0.0).astype(F32)
    eye = jnp.where(lax.broadcasted_iota(I32, (LANES, LANES), 0) ==
                    lax.broadcasted_iota(I32, (LANES, LANES), 1), 1.0, 0.0).astype(F32)
    acum = jnp.dot(tril, a, precision=_HI, preferred_element_type=F32)
    acum_t = lax.dot_general(eye, acum, _NT, precision=_HI, preferred_element_type=F32)
    dt_t = lax.dot_general(eye, dt, _NT, precision=_HI, preferred_element_type=F32)
    alast = acum[Lc - 1:Lc, :]
    wend = jnp.exp(alast - acum) * dt
    ea = jnp.exp(acum)
    edec = jnp.exp(alast)
    dsk = dsk_ref[...]
    first_half = lax.broadcasted_iota(I32, (Lc, 2 * P), 1) < P

    def pair(v, h):
        rows = v.shape[0]
        return jnp.where(first_half[:rows], jnp.broadcast_to(v[:, h:h + 1], (rows, 2 * P)),
                         jnp.broadcast_to(v[:, h + 1:h + 2], (rows, 2 * P)))

    y_parts = []
    for g in range(G):
        Bg = Bm[:, g * N:(g + 1) * N].astype(BF16)
        Cg = Cm[:, g * N:(g + 1) * N].astype(BF16)
        cb = lax.dot_general(Cg, Bg, _NT, preferred_element_type=F32)
        Sg = h_scr[g * GP:(g + 1) * GP, :]
        yoff = lax.dot_general(Cg, Sg.astype(BF16), _NT, preferred_element_type=F32)
        xw_parts = []
        for pr in range(HG // 2):
            h = g * HG + 2 * pr
            xs_pair = xs[:, h * P:(h + 2) * P]
            xs_pair_b = xs_pair.astype(BF16)
            outs = []
            for hh in (h, h + 1):
                seg = acum[:, hh:hh + 1] - acum_t[hh:hh + 1, :]
                dec = jnp.exp(jnp.where(causal, seg, -jnp.inf))
                mix = cb * dec * dt_t[hh:hh + 1, :]
                outs.append(jnp.dot(mix.astype(BF16), xs_pair_b, preferred_element_type=F32))
            ydiag = jnp.where(first_half, outs[0], outs[1])
            y_pair = ydiag + yoff[:, pr * 2 * P:(pr + 1) * 2 * P] * pair(ea, h) + pair(dsk, h) * xs_pair
            y_parts.append(y_pair)
            xw_parts.append(xs_pair * pair(wend, h))
        xw = jnp.concatenate(xw_parts, axis=1).astype(BF16)
        states = lax.dot_general(xw, Bg, _TN, preferred_element_type=F32)
        for hl in range(HG):
            hh = g * HG + hl
            r0 = g * GP + hl * P
            h_scr[r0:r0 + P, :] = edec[:, hh:hh + 1] * Sg[hl * P:(hl + 1) * P, :] + states[hl * P:(hl + 1) * P, :]

    z = z_ref[0]
    gn = DI // G
    outs = []
    for g in range(G):
        parts = y_parts[g * (HG // 2):(g + 1) * (HG // 2)]
        yg = jnp.concatenate(parts, axis=1) if len(parts) > 1 else parts[0]
        yg = yg * _silu(z[:, g * gn:(g + 1) * gn])
        ms = jnp.mean(yg * yg, axis=-1, keepdims=True)
        outs.append(yg * lax.rsqrt(ms + 1e-5))
    y_ref[0] = jnp.concatenate(outs, axis=1) * nw_ref[...]

    @pl.when(c == pl.num_programs(1) - 1)
    def _():
        hn_ref[0] = h_scr[...]


def _ssd(z, xbc, dtr, cbuf, h0, conv_w, conv_b, dt_bias, a_log, d_skip, norm_w, *, Lc, n_valid, G):
    B, L, DI = z.shape
    CD = xbc.shape[-1]
    _, H, P, N = h0.shape
    KW = conv_w.shape[0]
    assert P * 2 == LANES and (H // G) % 2 == 0 and H <= LANES and L % Lc == 0
    padl = lambda v: jnp.pad(v.astype(F32).reshape(1, -1), ((0, 0), (0, LANES - H)))
    seq = lambda b, c: (b, c, 0)
    per_b = lambda b, c: (b, 0, 0)
    fixed = lambda b, c: (0, 0)
    kern = functools.partial(_ssd_kernel, Lc=Lc, H=H, P=P, N=N, G=G, n_valid=n_valid, conv_w=KW)
    y, hn = pl.pallas_call(
        kern, grid=(B, L // Lc),
        in_specs=[pl.BlockSpec((1, Lc, DI), seq), pl.BlockSpec((1, Lc, CD), seq),
                  pl.BlockSpec((1, Lc, LANES), seq), pl.BlockSpec((1, SUBLANES, CD), per_b),
                  pl.BlockSpec((1, H * P, N), per_b),
                  pl.BlockSpec((KW, CD), fixed), pl.BlockSpec((1, CD), fixed),
                  pl.BlockSpec((1, LANES), fixed), pl.BlockSpec((1, LANES), fixed),
                  pl.BlockSpec((1, LANES), fixed), pl.BlockSpec((1, DI), fixed)],
        out_specs=[pl.BlockSpec((1, Lc, DI), seq), pl.BlockSpec((1, H * P, N), per_b)],
        out_shape=[jax.ShapeDtypeStruct((B, L, DI), F32), jax.ShapeDtypeStruct((B, H * P, N), F32)],
        scratch_shapes=[pltpu.VMEM((SUBLANES + Lc, CD), F32), pltpu.VMEM((H * P, N), F32)],
        compiler_params=_cparams("parallel", "arbitrary"), name="ssd",
    )(z, xbc, dtr, cbuf, h0.reshape(B, H * P, N), conv_w, conv_b.reshape(1, CD),
      padl(dt_bias), padl(a_log), padl(d_skip), norm_w.reshape(1, DI))
    return y, hn.reshape(B, H, P, N)


def _mamba_layer(Xp, Xs, dims, state_conv_i, state_ssm_i, params, ln_g, ln_b, alpha):
    Bp, Lp, Bs, Ls = dims
    w_in, conv_w, conv_b, dt_bias, a_log, d_skip, norm_w, w_out = params
    Tp = Bp * Lp
    DI = norm_w.shape[0]
    CD = conv_w.shape[1]
    _, H, P, N = state_ssm_i.shape
    G = (CD - DI) // (2 * N)
    KW = conv_w.shape[0]
    Ts = Bs * Ls
    wz = w_in[:, :DI].astype(BF16)
    wx = w_in[:, DI:DI + CD].astype(BF16)
    wd = _pad_cols(w_in[:, DI + CD:], LANES).astype(BF16)
    wo = w_out.astype(BF16)
    args = (conv_w, conv_b, dt_bias, a_log, d_skip, norm_w)

    xbc_p = _matmul(Xp, wx).reshape(Bp, Lp, CD)
    Lc = SSD_CHUNK if Lp % SSD_CHUNK == 0 else Lp
    yp, hp = _ssd(_matmul(Xp, wz).reshape(Bp, Lp, DI), xbc_p, _matmul(Xp, wd).reshape(Bp, Lp, LANES),
                  jnp.zeros((Bp, SUBLANES, CD), F32), jnp.zeros((Bp, H, P, N), F32), *args,
                  Lc=Lc, n_valid=Lc, G=G)
    conv_p = xbc_p[:, Lp - (KW - 1):]
    Xp = _matmul_ln(yp.reshape(Tp, DI), wo, Xp, ln_g, ln_b, alpha)

    Lsp = -(-Ls // SUBLANES) * SUBLANES
    pad_rows = lambda v: jnp.pad(v[:Ts].reshape(Bs, Ls, -1), ((0, 0), (0, Lsp - Ls), (0, 0)))
    xbc_s = _matmul(Xs, wx)[:Ts].reshape(Bs, Ls, CD)
    cbuf_s = jnp.pad(state_conv_i, ((0, 0), (SUBLANES - (KW - 1), 0), (0, 0)))
    ys, hs = _ssd(pad_rows(_matmul(Xs, wz)), pad_rows(xbc_s), pad_rows(_matmul(Xs, wd)), cbuf_s, state_ssm_i,
                  *args, Lc=Lsp, n_valid=Ls, G=G)
    conv_s = jnp.concatenate([state_conv_i, xbc_s], axis=1)[:, -(KW - 1):]
    ys = jnp.pad(ys[:, :Ls].reshape(Ts, DI), ((0, Xs.shape[0] - Ts), (0, 0)))
    Xs = _matmul_ln(ys, wo, Xs, ln_g, ln_b, alpha)
    return Xp, Xs, hp, hs, conv_p, conv_s


def _topk_rows(s, k, between_rounds=lambda: None):
    n = float(s.shape[0])
    rows = lax.broadcasted_iota(I32, s.shape, 0).astype(F32)
    vals, idxs = [], []
    for _ in range(k):
        m = jnp.max(s, axis=0, keepdims=True)
        i = jnp.min(jnp.where(s == m, rows, n), axis=0, keepdims=True)
        vals.append(m)
        idxs.append(i)
        s = jnp.where(rows == i, -jnp.inf, s)
        between_rounds()
    return vals, idxs


def _peer_route_kernel(q_ref, k_ref, w_ref, i1_scr, i2_scr, g_scr, *, heads, half, topk, n_keys, nj):
    tb = q_ref.shape[0]

    @pl.when(pl.program_id(0) == 0)
    def _():
        i1_scr[...] = jnp.zeros(i1_scr.shape, I32)
        i2_scr[...] = jnp.zeros(i2_scr.shape, I32)
        g_scr[...] = jnp.zeros(g_scr.shape, F32)

    sub = lax.broadcasted_iota(I32, (n_keys, i1_scr.shape[1]), 0)
    pending = list(range(tb))
    n_rounds = heads * 3 * topk
    every = max(1, n_rounds // tb - 1)
    calls = [0]

    def w_token():
        t = pending.pop(0)
        onehot1 = jnp.where(sub == i1_scr[t:t + 1, :], 1.0, 0.0).astype(BF16)
        gated2 = jnp.where(sub == i2_scr[t:t + 1, :], g_scr[t:t + 1, :], 0.0).astype(BF16)
        w_t = lax.dot_general(onehot1, gated2, _NT, preferred_element_type=F32)
        for k in range(n_keys // nj):
            w_ref[0, k, t * nj:(t + 1) * nj, :] = w_t[k * nj:(k + 1) * nj, :]

    def between_rounds():
        calls[0] += 1
        if pending and calls[0] % every == 0:
            w_token()

    i1, i2, gate = _peer_topk_block(q_ref, k_ref, heads=heads, half=half, topk=topk, n_keys=n_keys,
                                    between_rounds=between_rounds)
    while pending:
        w_token()
    i1_scr[...] = i1
    i2_scr[...] = i2
    g_scr[...] = gate


def _peer_topk_block(q_ref, k_ref, *, heads, half, topk, n_keys, between_rounds):
    k1 = k_ref[0]
    k2 = k_ref[1]
    tb = q_ref.shape[0]
    cand_ab = [(a, b) for a in range(topk) for b in range(topk // (a + 1))]
    n_pad = -len(cand_ab) % SUBLANES
    codes, gates = [], []
    for h in range(heads):
        q1 = q_ref[:, (2 * h) * half:(2 * h + 1) * half]
        q2 = q_ref[:, (2 * h + 1) * half:(2 * h + 2) * half]
        s1 = lax.dot_general(k1, q1, _NT, preferred_element_type=F32)
        s2 = lax.dot_general(k2, q2, _NT, preferred_element_type=F32)
        v1, i1 = _topk_rows(s1, topk, between_rounds)
        v2, i2 = _topk_rows(s2, topk, between_rounds)
        cand = jnp.concatenate([v1[a] + v2[b] for a, b in cand_ab] +
                               [jnp.full((n_pad, tb), -jnp.inf, F32)], axis=0)
        code = jnp.concatenate([i1[a] * float(n_keys) + i2[b] for a, b in cand_ab] +
                               [jnp.zeros((n_pad, tb), F32)], axis=0)
        sc, pos = _topk_rows(cand, topk, between_rounds)
        crow = lax.broadcasted_iota(I32, cand.shape, 0).astype(F32)
        codes += [jnp.max(jnp.where(crow == p, code, -1.0), axis=0, keepdims=True) for p in pos]
        ex = jnp.exp(jnp.concatenate(sc, axis=0) - sc[0])
        gates.append(ex / jnp.sum(ex, axis=0, keepdims=True))
    code_t = jnp.concatenate(codes, axis=0).T.astype(I32)
    return code_t // n_keys, code_t % n_keys, jnp.concatenate(gates, axis=0).T


def _gelu_tanh(x):
    hx = 0.5 * x
    inner = x * (GELU_C0 + GELU_C1 * (x * x))
    return hx * jnp.tanh(inner) + hx


def _peer_dense_kernel(x_ref, u_ref, v_ref, w_ref, g_ref, b_ref, o_ref, acc_ref, xb_ref, *, alpha, nj):
    k = pl.program_id(1)
    td = x_ref.shape[0]

    @pl.when(k == 0)
    def _():
        acc_ref[...] = jnp.zeros_like(acc_ref)
        xb_ref[...] = x_ref[...].astype(BF16)

    act = lax.dot_general(xb_ref[...], u_ref[...], _NT, preferred_element_type=F32)
    w = jnp.concatenate([w_ref[0, 0, pl.ds(j, td, stride=nj), :] for j in range(nj)], axis=1)
    coef = (_gelu_tanh(act) * w).astype(BF16)
    acc_ref[...] += jnp.dot(coef, v_ref[...], preferred_element_type=F32)

    @pl.when(k == pl.num_programs(1) - 1)
    def _():
        o_ref[...] = _deepnorm_ln(alpha * x_ref[...] + acc_ref[...], g_ref[...], b_ref[...])


def _peer_layer(X, w_q, sub_keys, u_tab, v_tab, ln_g, ln_b, alpha):
    T, D = X.shape
    _, n_keys, half = sub_keys.shape
    heads = w_q.shape[1] // (2 * half)
    nsel = heads * PEER_TOPK
    assert n_keys == LANES and nsel == LANES and half % LANES == 0
    q = _matmul(X, w_q)

    tb = LANES
    td = _pick(T, (1024, 768, 512, 256, 128))
    nj = SUBLANES
    et = nj * n_keys
    sub_blocks = td // tb
    n_blocks = T // tb
    prev = lambda i: jnp.maximum(i - 1, 0)
    w = pl.pallas_call(
        functools.partial(_peer_route_kernel, heads=heads, half=half, topk=PEER_TOPK, n_keys=n_keys, nj=nj),
        grid=(n_blocks + 1,),
        in_specs=[pl.BlockSpec((tb, q.shape[1]), lambda i: (jnp.minimum(i, n_blocks - 1), 0)),
                  pl.BlockSpec((2, n_keys, half), lambda i: (0, 0, 0))],
        out_specs=pl.BlockSpec((1, n_keys // nj, tb * nj, n_keys),
                               lambda i: (prev(i) // sub_blocks, 0, prev(i) % sub_blocks, 0)),
        out_shape=jax.ShapeDtypeStruct((T // td, n_keys // nj, td * nj, n_keys), F32),
        scratch_shapes=[pltpu.VMEM((tb, nsel), I32), pltpu.VMEM((tb, nsel), I32), pltpu.VMEM((tb, nsel), F32)],
        compiler_params=_cparams("arbitrary"), name="peer_route")(q, sub_keys)

    return pl.pallas_call(
        functools.partial(_peer_dense_kernel, alpha=alpha, nj=nj),
        grid=(T // td, n_keys // nj),
        in_specs=[pl.BlockSpec((td, D), lambda i, k: (i, 0)),
                  pl.BlockSpec((et, D), lambda i, k: (k, 0)),
                  pl.BlockSpec((et, D), lambda i, k: (k, 0)),
                  pl.BlockSpec((1, 1, td * nj, n_keys), lambda i, k: (i, k, 0, 0)),
                  pl.BlockSpec((1, D), lambda i, k: (0, 0)), pl.BlockSpec((1, D), lambda i, k: (0, 0))],
        out_specs=pl.BlockSpec((td, D), lambda i, k: (i, 0)),
        out_shape=jax.ShapeDtypeStruct((T, D), F32),
        scratch_shapes=[pltpu.VMEM((td, D), F32), pltpu.VMEM((td, D), BF16)],
        compiler_params=_cparams("parallel", "arbitrary"), name="peer_dense",
    )(X, u_tab, v_tab, w, ln_g.reshape(1, D), ln_b.reshape(1, D))


def _gather_kernel(pt_ref, pa_ref, pb_ref, new_ref, kc_ref, vc_ref, ks_ref, vs_ref, col_scr, *, n_steps, cw, blk):
    del pt_ref
    p = pl.program_id(1)
    ps = pa_ref.shape[1]
    per_page = ps // blk

    @pl.when(p < n_steps)
    def _():
        for j, page in enumerate((pa_ref, pb_ref)):
            n_tiles = 2 * cw // LANES
            for c in range(n_tiles):
                col_scr[j * n_tiles + c] = page[0, :, c * LANES:(c + 1) * LANES]
            for r, ref in enumerate((kc_ref, vc_ref)):
                for l in range(blk):
                    for c in range(cw // LANES):
                        ref[0, j * per_page:(j + 1) * per_page, l * cw + c * LANES:l * cw + (c + 1) * LANES] = \
                            col_scr[j * n_tiles + r * (cw // LANES) + c, pl.ds(l, per_page, stride=blk), :]
            ks_ref[0, j * ps:(j + 1) * ps, :] = page[0, :, 2 * cw:3 * cw].astype(BF16)
            vs_ref[0, j * ps:(j + 1) * ps, :] = page[0, :, 3 * cw:4 * cw].astype(BF16)

    @pl.when(p == n_steps)
    def _():
        new = new_ref[0]
        n_new = new.shape[0]
        for r, ref in enumerate((kc_ref, vc_ref)):
            ref[0] = jnp.zeros(ref.shape[1:], F32)
            for l in range(n_new):
                ref[0, 0:1, l * cw:(l + 1) * cw] = new[l:l + 1, r * cw:(r + 1) * cw]
        for r, ref in ((2, ks_ref), (3, vs_ref)):
            rows = jnp.concatenate([new[:, r * cw:(r + 1) * cw], jnp.zeros((2 * ps - n_new, cw), F32)], axis=0)
            ref[0] = rows.astype(BF16)


def _gather_pages(pages, page_table, new_rows, blk):
    NP, PS, C = pages.shape
    B, n_pages = page_table.shape
    assert n_pages % 2 == 0 and PS % blk == 0 and (2 * PS // blk) % SUBLANES == 0 and new_rows.shape[1] <= blk
    cw = C // 4
    n_steps = n_pages // 2
    L = (n_pages + 2) * PS
    nb = 2 * PS // blk
    page_spec = lambda j: pl.BlockSpec(
        (1, PS, C), lambda b, p, pt: (pt[b, jnp.minimum(2 * p + j, n_pages - 2 + j)], 0, 0))
    cmp_spec = pl.BlockSpec((1, nb, blk * cw), lambda b, p, pt: (b, p, 0))
    sel_spec = pl.BlockSpec((1, 2 * PS, cw), lambda b, p, pt: (b, p, 0))
    return pl.pallas_call(
        functools.partial(_gather_kernel, n_steps=n_steps, cw=cw, blk=blk),
        grid_spec=pltpu.PrefetchScalarGridSpec(
            num_scalar_prefetch=1, grid=(B, n_steps + 1),
            in_specs=[page_spec(0), page_spec(1),
                      pl.BlockSpec((1, new_rows.shape[1], C), lambda b, p, pt: (b, 0, 0))],
            out_specs=[cmp_spec, cmp_spec, sel_spec, sel_spec],
            scratch_shapes=[pltpu.VMEM((2 * (2 * cw // LANES), PS, LANES), F32)]),
        out_shape=[jax.ShapeDtypeStruct((B, L // blk, blk * cw), F32)] * 2 +
                  [jax.ShapeDtypeStruct((B, L, cw), BF16)] * 2,
        compiler_params=_cparams("parallel", "arbitrary"), name="kv_gather")(page_table, pages, pages, new_rows)


def _compress_blocks(xb, w_c, pos_c, nc_pad):
    B, n, K = xb.shape
    blk, dh, _ = w_c.shape
    C = K // blk
    G = C // dh
    w_bd = jnp.einsum('lde,gh->lgdhe', w_c, jnp.eye(G, dtype=w_c.dtype)).reshape(K, C)
    xbias = jnp.broadcast_to(pos_c[:, None, :], (blk, G, dh)).reshape(1, K)
    out = _matmul(xb.reshape(B * n, K), w_bd.astype(BF16), xbias=xbias, tm_cap=128)
    return jnp.pad(out.reshape(B, n, C), ((0, 0), (0, nc_pad - n), (0, 0)))


def _compress(x, w_c, pos_c, nc_pad):
    B, L, C = x.shape
    blk = w_c.shape[0]
    return _compress_blocks(x.reshape(B, L // blk, blk * C), w_c, pos_c, nc_pad)


def _lane_tiles(x):
    return [x[:, j * LANES:(j + 1) * LANES] for j in range(x.shape[1] // LANES)]


def _row_max(x):
    return jnp.max(functools.reduce(jnp.maximum, _lane_tiles(x)), axis=-1, keepdims=True)


def _row_sum(x):
    return jnp.sum(functools.reduce(jnp.add, _lane_tiles(x)), axis=-1, keepdims=True)


def _split3_dot(x, m):
    x1 = x.astype(BF16)
    r1 = x - x1.astype(F32)
    x2 = r1.astype(BF16)
    x3 = (r1 - x2.astype(F32)).astype(BF16)
    d = lambda a: jnp.dot(a, m, preferred_element_type=F32)
    return d(x1) + d(x2) + d(x3)


def _nsa_staged_kernel(q_ref, gt_ref, kc_ref, vc_ref, ks_ref, vs_ref, kw_ref, vw_ref, ex_ref, o_ref, *,
                       seqs, nq, t0, tk, n_cmp, cmp_blk, n_sel, kvh, hg, dh, slopes, win_rows, w0):
    q0 = t0 + pl.program_id(1) * nq
    R = hg * nq
    NC = kc_ref.shape[1]
    Lk = ks_ref.shape[1]
    NSP = ex_ref.shape[0]
    assert tk == Lk
    qscale = dh ** -0.5 * LOG2E
    slopes2 = tuple(s * LOG2E for s in slopes)
    units = [(s, g) for s in range(seqs) for g in range(kvh)]
    col = lambda g: (g // 2) * 2 * dh

    def tq_like(shape):
        return q0 + lax.broadcasted_iota(I32, shape, 0)

    def attend(keys, vals, distms):
        s_alls = [lax.dot_general(qgs[u], keys[u], _NT, preferred_element_type=F32) for u in range(len(units))]
        s2s = [[s_alls[u][h * nq:(h + 1) * nq] - slopes2[g * hg + h] * distms[u] for h in range(hg)]
               for u, (_, g) in enumerate(units)]
        ms = [[_row_max(x) for x in row] for row in s2s]
        es = [[jnp.exp2(x - jnp.where(m == -jnp.inf, 0.0, m)) for x, m in zip(xr, mr)] for xr, mr in zip(s2s, ms)]
        ls = [[_row_sum(e) for e in row] for row in es]
        ps = [[e / jnp.maximum(l, 1e-30) for e, l in zip(er, lr)] for er, lr in zip(es, ls)]
        os = [jnp.dot(jnp.concatenate(ps[u], axis=0).astype(BF16), vals[u], preferred_element_type=F32)
              for u in range(len(units))]
        return ps, os

    n_idx = lax.broadcasted_iota(I32, (nq, NC), 1)
    dist_c = tq_like((nq, NC)) - (n_idx * cmp_blk + (cmp_blk - 1))
    distm_c = jnp.where((dist_c >= 0) & (n_idx < n_cmp), dist_c.astype(F32), jnp.inf)
    dist_w = tq_like((nq, win_rows)) - (w0 + lax.broadcasted_iota(I32, (nq, win_rows), 1))
    distm_w = jnp.where((dist_w >= 0) & (dist_w < WINDOW), dist_w.astype(F32), jnp.inf)
    dist_s = tq_like((nq, Lk)) - lax.broadcasted_iota(I32, (nq, Lk), 1)
    per_sel = SEL_BLOCK // cmp_blk
    pair = jnp.where(lax.broadcasted_iota(I32, (NC, NSP), 0) // per_sel ==
                     lax.broadcasted_iota(I32, (NC, NSP), 1), 1.0, 0.0).astype(BF16)
    blk = lax.broadcasted_iota(I32, (nq, NSP), 1)
    tqs = tq_like((nq, NSP))
    cur = tqs // SEL_BLOCK
    forced = (blk == 0) | (blk == cur) | (blk == cur - 1)
    sel_valid = blk * SEL_BLOCK <= tqs

    qgs = [(jnp.concatenate([q_ref[s, :, (g * hg + h) * 2 * dh:(g * hg + h + 1) * 2 * dh] for h in range(hg)],
                            axis=0) * qscale).astype(BF16) for s, g in units]
    tile = lambda ref, s, g: ref[s, :, col(g):col(g) + 2 * dh]
    p_cmp, o_cmp = attend([tile(kc_ref, s, g).astype(BF16) for s, g in units],
                          [tile(vc_ref, s, g).astype(BF16) for s, g in units], [distm_c] * len(units))
    _, o_win = attend([tile(kw_ref, s, g) for s, g in units], [tile(vw_ref, s, g) for s, g in units],
                      [distm_w] * len(units))

    psums = [functools.reduce(jnp.add, row) for row in p_cmp]
    imps = [_split3_dot(x, pair) for x in psums]
    scores = [jnp.where(blk < n_sel, jnp.where(forced, FORCE_SCORE, jnp.where(sel_valid, imp, -1.0)), -2.0)
              for imp in imps]
    selm = _top_blocks(jnp.concatenate(scores, axis=0), min(N_SEL, n_sel), NSP)
    selb = selm.astype(BF16)
    expand = ex_ref[...]
    selxs = [jnp.dot(selb[u * nq:(u + 1) * nq], expand, preferred_element_type=F32) for u in range(len(units))]
    distm_s = [jnp.where((dist_s >= 0) & (x > 0.5), dist_s.astype(F32), jnp.inf) for x in selxs]
    _, o_sel = attend([tile(ks_ref, s, g) for s, g in units], [tile(vs_ref, s, g) for s, g in units], distm_s)

    lane = lax.broadcasted_iota(I32, (nq, 2 * dh), 1)
    for s in range(seqs):
        gates = jax.nn.sigmoid(gt_ref[s])
        for g in range(kvh):
            u = s * kvh + g
            lo = (g % 2) * dh
            keep = (lane >= lo) & (lane < lo + dh)
            for h in range(hg):
                hd = g * hg + h
                rows = slice(h * nq, (h + 1) * nq)
                o = (gates[:, 3 * hd:3 * hd + 1] * o_cmp[u][rows] + gates[:, 3 * hd + 1:3 * hd + 2] * o_sel[u][rows] +
                     gates[:, 3 * hd + 2:3 * hd + 3] * o_win[u][rows])
                o_ref[s, :, hd * 2 * dh:(hd + 1) * 2 * dh] = jnp.where(keep, o, 0.0)


def _top_blocks(score, n_top, nsp):
    if score.shape[0] % LANES == 0 and nsp == LANES:
        st = score.T
        brow = lax.broadcasted_iota(I32, st.shape, 0).astype(F32)
        sel_t = jnp.zeros(st.shape, F32)
        for _ in range(n_top):
            m = jnp.max(st, axis=0, keepdims=True)
            idx = jnp.min(jnp.where(st == m, brow, float(nsp)), axis=0, keepdims=True)
            hit = brow == idx
            sel_t = jnp.where(hit, 1.0, sel_t)
            st = jnp.where(hit, -jnp.inf, st)
        return sel_t.T
    bcol = lax.broadcasted_iota(I32, score.shape, 1)
    selm = jnp.zeros(score.shape, F32)
    for _ in range(n_top):
        m = jnp.max(score, axis=-1, keepdims=True)
        idx = jnp.min(jnp.where(score == m, bcol, nsp), axis=-1, keepdims=True)
        hit = bcol == idx
        selm = jnp.where(hit, 1.0, selm)
        score = jnp.where(hit, -jnp.inf, score)
    return selm


def _nsa_tiled_kernel(q_ref, gt_ref, kc_ref, vc_ref, ks_ref, vs_ref, kw_ref, vw_ref, ex_ref, o_ref,
                      m_scr, l_scr, acc_scr, flag_ref, *,
                      seqs, nq, t0, tk, n_cmp, cmp_blk, n_sel, kvh, hg, dh, slopes, win_rows, w0):
    assert seqs == 1
    i = pl.program_id(1)
    q0 = t0 + i * nq
    R = hg * nq
    NC = kc_ref.shape[1]
    Lk = ks_ref.shape[1]
    NSP = ex_ref.shape[0]
    qscale = dh ** -0.5 * LOG2E
    slopes2 = tuple(s * LOG2E for s in slopes)
    gates = jax.nn.sigmoid(gt_ref[0])

    def tq_like(shape):
        return q0 + lax.broadcasted_iota(I32, shape, 0)

    wstart = pl.multiple_of(jnp.maximum(q0 - WINDOW, 0), nq)

    def col(g):
        return (g // 2) * 2 * dh

    groups = range(kvh)

    def attend(keys, vals, distm, want_p):
        s_alls = [lax.dot_general(qgs[g], keys[g], _NT, preferred_element_type=F32) for g in groups]
        s2s = [[s_alls[g][h * nq:(h + 1) * nq] - slopes2[g * hg + h] * distm for h in range(hg)] for g in groups]
        ms = [[_row_max(x) for x in row] for row in s2s]
        es = [[jnp.exp2(x - jnp.where(m == -jnp.inf, 0.0, m)) for x, m in zip(xr, mr)] for xr, mr in zip(s2s, ms)]
        ls = [[jnp.maximum(_row_sum(e), 1e-30) for e in row] for row in es]
        if want_p:
            ps = [[e / l for e, l in zip(er, lr)] for er, lr in zip(es, ls)]
            os = [jnp.dot(jnp.concatenate(ps[g], axis=0).astype(BF16), vals[g], preferred_element_type=F32)
                  for g in groups]
            return ps, os
        os = [jnp.dot(jnp.concatenate(es[g], axis=0).astype(BF16), vals[g], preferred_element_type=F32) /
              jnp.concatenate(ls[g], axis=0) for g in groups]
        return None, os

    n_idx = lax.broadcasted_iota(I32, (nq, NC), 1)
    dist_c = tq_like((nq, NC)) - (n_idx * cmp_blk + (cmp_blk - 1))
    distm_c = jnp.where((dist_c >= 0) & (n_idx < n_cmp), dist_c.astype(F32), jnp.inf)
    dist_w = tq_like((nq, win_rows)) - (w0 + wstart + lax.broadcasted_iota(I32, (nq, win_rows), 1))
    distm_w = jnp.where((dist_w >= 0) & (dist_w < WINDOW), dist_w.astype(F32), jnp.inf)
    per_sel = SEL_BLOCK // cmp_blk
    pair = jnp.where(lax.broadcasted_iota(I32, (NC, NSP), 0) // per_sel ==
                     lax.broadcasted_iota(I32, (NC, NSP), 1), 1.0, 0.0).astype(BF16)
    blk = lax.broadcasted_iota(I32, (nq, NSP), 1)
    tqs = tq_like((nq, NSP))
    cur = tqs // SEL_BLOCK
    forced = (blk == 0) | (blk == cur) | (blk == cur - 1)
    sel_valid = blk * SEL_BLOCK <= tqs

    qgs = [(jnp.concatenate([q_ref[0, :, (g * hg + h) * 2 * dh:(g * hg + h + 1) * 2 * dh] for h in range(hg)],
                            axis=0) * qscale).astype(BF16) for g in groups]
    p_cmps, o_cmps = attend([kc_ref[0, :, col(g):col(g) + 2 * dh].astype(BF16) for g in groups],
                            [vc_ref[0, :, col(g):col(g) + 2 * dh].astype(BF16) for g in groups], distm_c, True)
    _, o_wins = attend([kw_ref[0, pl.ds(wstart, win_rows), col(g):col(g) + 2 * dh] for g in groups],
                       [vw_ref[0, pl.ds(wstart, win_rows), col(g):col(g) + 2 * dh] for g in groups], distm_w, False)
    imps = [_split3_dot(functools.reduce(jnp.add, p_cmps[g]), pair) for g in groups]
    scores = [jnp.where(blk < n_sel, jnp.where(forced, FORCE_SCORE, jnp.where(sel_valid, imp, -1.0)), -2.0)
              for imp in imps]

    selm = _top_blocks(jnp.concatenate(scores, axis=0), min(N_SEL, n_sel), NSP)
    selb = selm.astype(BF16)

    def sel_tile_group(kt, g):
        k0 = pl.multiple_of(kt * tk, tk)
        c0 = col(g)
        dist = tq_like((nq, tk)) - (k0 + lax.broadcasted_iota(I32, (nq, tk), 1))
        selx = jnp.dot(selb[g * nq:(g + 1) * nq], ex_ref[:, pl.ds(k0, tk)], preferred_element_type=F32)
        distm = jnp.where((dist >= 0) & (selx > 0.5), dist.astype(F32), jnp.inf)
        s_t = lax.dot_general(qgs[g], ks_ref[0, pl.ds(k0, tk), c0:c0 + 2 * dh], _NT,
                              preferred_element_type=F32)
        m_run = m_scr[g]
        l_run = l_scr[g]
        heads = range(hg)
        rows = [slice(h * nq, (h + 1) * nq) for h in heads]
        s2s = [s_t[rows[h]] - slopes2[g * hg + h] * distm for h in heads]
        tmax = [_row_max(x) for x in s2s]
        m_new = [jnp.maximum(m_run[rows[h]], tmax[h]) for h in heads]
        m_safe = [jnp.where(m == -jnp.inf, 0.0, m) for m in m_new]
        alphas = [jnp.exp2(m_run[rows[h]] - m_safe[h]) for h in heads]
        ps = [jnp.exp2(s2s[h] - jnp.concatenate([m_safe[h]] * (tk // LANES), axis=1)) for h in heads]
        l_new = [alphas[h] * l_run[rows[h]] + _row_sum(ps[h]) for h in heads]
        pv = jnp.dot(jnp.concatenate(ps, axis=0).astype(BF16), vs_ref[0, pl.ds(k0, tk), c0:c0 + 2 * dh],
                     preferred_element_type=F32)
        m_scr[g] = jnp.concatenate(m_new, axis=0)
        l_scr[g] = jnp.concatenate(l_new, axis=0)
        acc_scr[g] = jnp.concatenate(alphas, axis=0) * acc_scr[g] + pv

    n_flag = Lk // tk
    blocks_per_tile = tk // SEL_BLOCK
    lane_b = lax.broadcasted_iota(I32, (1, NSP), 1)
    for g in range(kvh):
        any_q = jnp.max(selm[g * nq:(g + 1) * nq], axis=0, keepdims=True)
        for kt in range(n_flag):
            in_tile = (lane_b >= kt * blocks_per_tile) & (lane_b < (kt + 1) * blocks_per_tile)
            flag_ref[g * n_flag + kt] = (jnp.max(jnp.where(in_tile, any_q, 0.0)) > 0.5).astype(I32)
    m_scr[...] = jnp.full(m_scr.shape, -jnp.inf, F32)
    l_scr[...] = jnp.zeros(l_scr.shape, F32)
    acc_scr[...] = jnp.zeros(acc_scr.shape, F32)

    def visit(kt, carry):
        for g in range(kvh):
            @pl.when(flag_ref[g * n_flag + kt] > 0)
            def _():
                sel_tile_group(kt, g)
        return carry

    lax.fori_loop(0, (q0 + nq + tk - 1) // tk, visit, 0)

    lane = lax.broadcasted_iota(I32, (nq, 2 * dh), 1)
    for g in range(kvh):
        lo = (g % 2) * dh
        keep = (lane >= lo) & (lane < lo + dh)
        o_sel = acc_scr[g] / jnp.maximum(l_scr[g], 1e-30)
        for h in range(hg):
            hd = g * hg + h
            rows = slice(h * nq, (h + 1) * nq)
            o = (gates[:, 3 * hd:3 * hd + 1] * o_cmps[g][rows] + gates[:, 3 * hd + 1:3 * hd + 2] * o_sel[rows] +
                 gates[:, 3 * hd + 2:3 * hd + 3] * o_wins[g][rows])
            o_ref[0, :, hd * 2 * dh:(hd + 1) * 2 * dh] = jnp.where(keep, o, 0.0)


def _nsa_attend(q, gt, kc, vc, ks, vs, kw, vw, cols, *, nq, t0, tk, n_cmp, cmp_blk, n_sel, kvh, dh,
                win_rows, w0, causal_tiles, seqs=1):
    B, Lq, QW = q.shape
    H = QW // (2 * dh)
    hg = H // kvh
    cw = kvh * dh
    Lk = ks.shape[1]
    NC = kc.shape[1]
    nsp = LANES
    assert n_sel <= nsp and Lk % tk == 0 and Lq % nq == 0 and dh * 2 == LANES
    expand = (lax.broadcasted_iota(I32, (nsp, Lk), 1) // SEL_BLOCK ==
              lax.broadcasted_iota(I32, (nsp, Lk), 0)).astype(BF16)
    slopes = tuple(2.0 ** (-8.0 * (h + 1) / H) for h in range(H))
    assert B % seqs == 0
    cfg = dict(seqs=seqs, nq=nq, t0=t0, tk=tk, n_cmp=n_cmp, cmp_blk=cmp_blk, n_sel=n_sel, kvh=kvh, hg=hg, dh=dh,
               slopes=slopes, win_rows=win_rows, w0=w0)
    R = hg * nq
    if causal_tiles:
        kern = functools.partial(_nsa_tiled_kernel, **cfg)
        scratch = [pltpu.VMEM((kvh, R, LANES), F32), pltpu.VMEM((kvh, R, LANES), F32),
                   pltpu.VMEM((kvh, R, 2 * dh), F32), pltpu.SMEM((kvh * (Lk // tk),), I32)]
    else:
        assert win_rows == kw.shape[1]
        kern = functools.partial(_nsa_staged_kernel, **cfg)
        scratch = []
    qmap = lambda b, i: (b, i, 0)
    kv_spec = lambda arr, c: pl.BlockSpec((seqs, arr.shape[1], cw), lambda b, i, c=c: (b, 0, c))
    return pl.pallas_call(
        kern, grid=(B // seqs, Lq // nq),
        in_specs=[pl.BlockSpec((seqs, nq, QW), qmap), pl.BlockSpec((seqs, nq, LANES), qmap),
                  kv_spec(kc, cols[0]), kv_spec(vc, cols[1]), kv_spec(ks, cols[2]), kv_spec(vs, cols[3]),
                  kv_spec(kw, cols[4]), kv_spec(vw, cols[5]),
                  pl.BlockSpec((nsp, Lk), lambda b, i: (0, 0))],
        out_specs=pl.BlockSpec((seqs, nq, QW), qmap),
        out_shape=jax.ShapeDtypeStruct((B, Lq, QW), F32),
        scratch_shapes=scratch,
        compiler_params=_cparams("parallel", "arbitrary"), name="nsa_attend")(q, gt, kc, vc, ks, vs, kw, vw, expand)


def _head_tiles(w, H, kvh, dh, axis):
    hg = H // kvh
    parts = []
    for h in range(H):
        sl = [slice(None)] * w.ndim
        sl[axis] = slice(h * dh, (h + 1) * dh)
        piece = w[tuple(sl)]
        zero = jnp.zeros_like(piece)
        parts += [zero, piece] if ((h // hg) % 2) else [piece, zero]
    return jnp.concatenate(parts, axis=axis)


def kernel(x_prompt, x_sample, cache_kv_pages, cache_win, state_ssm, state_conv, page_table, ln_g, ln_b, m_w_in, m_conv_w, m_conv_b, m_dt_bias, m_a_log, m_d_skip, m_norm_w, m_w_out, w_kv_shared, w_cmp, pos_cmp, nsa_w_in, nsa_w_out, peer_w_q, peer_sub_keys, peer_u, peer_v):
    Bp, Lp, D = x_prompt.shape
    Bs, Ls, _ = x_sample.shape
    depth = ln_g.shape[0]
    n_a = m_w_in.shape[0]
    alpha = (2 * depth) ** DEPTH_ALPHA_POW
    Tp, Ts = Bp * Lp, Bs * Ls
    assert Tp % LANES == 0
    Tsp = -(-Ts // LANES) * LANES
    Xp = x_prompt.reshape(Tp, D)
    Xs = jnp.pad(x_sample.reshape(Ts, D), ((0, Tsp - Ts), (0, 0)))
    dims = (Bp, Lp, Bs, Ls)

    NP, PS, n_rows, kvh, dh = cache_kv_pages.shape
    cw = kvh * dh
    n_pages = page_table.shape[1]
    past_len = n_pages * PS
    win_cache = cache_win.shape[1]
    cmp_blk = w_cmp.shape[1]
    H = nsa_w_in.shape[2] // (dh + 3)

    ssm_p, ssm_s, conv_p, conv_s = [], [], [], []
    for i in range(depth):
        if i < n_a:
            mp = (m_w_in[i], m_conv_w[i], m_conv_b[i], m_dt_bias[i], m_a_log[i], m_d_skip[i], m_norm_w[i], m_w_out[i])
            Xp, Xs, hp, hs, cp, cs = _mamba_layer(Xp, Xs, dims, state_conv[i], state_ssm[i], mp,
                                                  ln_g[i, 0], ln_b[i, 0], alpha)
            ssm_p.append(hp)
            ssm_s.append(hs)
            conv_p.append(cp)
            conv_s.append(cs)
        else:
            if i == n_a:
                wkv = w_kv_shared.astype(BF16)
                kv_p = _matmul(Xp, wkv).reshape(Bp, Lp, 6 * cw)
                kv_s = _matmul(Xs, wkv)[:Ts].reshape(Bs, Ls, 6 * cw)
                kvb_p = kv_p.astype(BF16)
                n_sel_p = -(-Lp // SEL_BLOCK)
                Lpp = n_sel_p * SEL_BLOCK
                padp = lambda v: jnp.pad(v, ((0, 0), (0, Lpp - Lp), (0, 0)))
                nc_p = -(-(Lpp // cmp_blk) // LANES) * LANES
                kc_p = _compress(padp(kv_p[:, :, 0:cw]), w_cmp[0], pos_cmp[0], nc_p)
                vc_p = _compress(padp(kv_p[:, :, cw:2 * cw]), w_cmp[1], pos_cmp[1], nc_p)
                tk_p = _pick(Lp, (512, 256, 128))
                new_rows = jnp.pad(kv_s[:, :, :4 * cw], ((0, 0), (0, SUBLANES - Ls), (0, 0)))
                kcb_s, vcb_s, ks_s, vs_s = _gather_pages(cache_kv_pages.reshape(NP, PS, n_rows * cw), page_table,
                                                         new_rows, cmp_blk)
                Lk_s = ks_s.shape[1]
                n_sel_s = -(-(past_len + Ls) // SEL_BLOCK)
                nc_s = -(-(Lk_s // cmp_blk) // LANES) * LANES
                kc_s = _compress_blocks(kcb_s, w_cmp[0], pos_cmp[0], nc_s)
                vc_s = _compress_blocks(vcb_s, w_cmp[1], pos_cmp[1], nc_s)
                win_full_s = jnp.concatenate([cache_win, kv_s[:, :, 4 * cw:].reshape(Bs, Ls, 2, kvh, dh)], axis=1)
                Lw = win_cache + Ls
                Lwp = -(-Lw // LANES) * LANES
                wflat = jnp.pad(win_full_s.reshape(Bs, Lw, 2 * cw), ((0, 0), (0, Lwp - Lw), (0, 0))).astype(BF16)
            j = i - n_a
            wq = _head_tiles(nsa_w_in[j][:, :H * dh], H, kvh, dh, axis=1).astype(BF16)
            wg = _pad_cols(nsa_w_in[j][:, H * dh:], LANES).astype(BF16)
            wo = _head_tiles(nsa_w_out[j], H, kvh, dh, axis=0).astype(BF16)
            QW = wq.shape[1]
            o_p = _nsa_attend(_matmul(Xp, wq).reshape(Bp, Lp, QW), _matmul(Xp, wg).reshape(Bp, Lp, LANES),
                              kc_p, vc_p, kvb_p, kvb_p, kvb_p, kvb_p, (0, 0, 2, 3, 4, 5),
                              nq=Q_BLOCK, t0=0, tk=tk_p, n_cmp=Lpp // cmp_blk, cmp_blk=cmp_blk, n_sel=n_sel_p,
                              kvh=kvh, dh=dh, win_rows=min(WINDOW + Q_BLOCK, Lp), w0=0, causal_tiles=True)
            Xp = _matmul_ln(o_p.reshape(Tp, QW), wo, Xp, ln_g[i, 0], ln_b[i, 0], alpha)
            pad_q = lambda v: jnp.pad(v[:Ts].reshape(Bs, Ls, -1), ((0, 0), (0, SUBLANES - Ls), (0, 0)))
            o_s = _nsa_attend(pad_q(_matmul(Xs, wq)), pad_q(_matmul(Xs, wg)), kc_s, vc_s, ks_s, vs_s, wflat, wflat,
                              (0, 0, 0, 0, 0, 1),
                              nq=SUBLANES, t0=past_len, tk=Lk_s, n_cmp=-(-(past_len + Ls) // SEL_BLOCK) * (SEL_BLOCK // cmp_blk),
                              cmp_blk=cmp_blk, n_sel=n_sel_s, kvh=kvh, dh=dh, win_rows=Lwp,
                              w0=past_len - win_cache, causal_tiles=False, seqs=_pick(Bs, (4, 2, 1)))
            o_s = jnp.pad(o_s[:, :Ls].reshape(Ts, QW), ((0, Tsp - Ts), (0, 0)))
            Xs = _matmul_ln(o_s, wo, Xs, ln_g[i, 0], ln_b[i, 0], alpha)
        peer = (peer_w_q[i].astype(BF16), peer_sub_keys[i], peer_u[i].astype(BF16), peer_v[i].astype(BF16),
                ln_g[i, 1], ln_b[i, 1], alpha)
        Xp = _peer_layer(Xp, *peer)
        Xs = _peer_layer(Xs, *peer)

    rows_p = kv_p[:, :, :4 * cw].reshape(Bp, Lp, n_rows, kvh, dh)
    rows_s = kv_s[:, :, :4 * cw].reshape(Bs, Ls, n_rows, kvh, dh)
    win_p = kv_p[:, :, 4 * cw:].reshape(Bp, Lp, 2, kvh, dh)[:, -min(WINDOW, Lp):]
    return (Xp.reshape(Bp, Lp, D), Xs[:Ts].reshape(Bs, Ls, D), rows_p, rows_s, win_p,
            win_full_s[:, -win_cache:], jnp.stack(ssm_p), jnp.stack(ssm_s), jnp.stack(conv_p), jnp.stack(conv_s))
```

```python
import functools
import math

import jax
import jax.numpy as jnp
from jax import lax
from jax.experimental import pallas as pl
from jax.experimental.pallas import tpu as pltpu

F32 = jnp.float32
BF16 = jnp.bfloat16
I32 = jnp.int32

DEPTH_ALPHA_POW = 0.25
LN_EPS = 1e-5
SEL_BLOCK = 64
N_SEL = 16
WINDOW = 512
FORCE_SCORE = 1.0e4
PEER_TOPK = 16
SSD_CHUNK = 128
Q_BLOCK = 128
GELU_C0 = math.sqrt(2.0 / math.pi)
GELU_C1 = 0.044715 * GELU_C0
LOG2E = 1.0 / math.log(2.0)
MAX_FULL_N = 3072

LANES = 128
SUBLANES = 8
VMEM_LIMIT_BYTES = 56 * 1024 * 1024

_NT = (((1,), (1,)), ((), ()))
_TN = (((0,), (0,)), ((), ()))
_HI = lax.Precision.HIGHEST


def _cparams(*sem):
    return pltpu.CompilerParams(dimension_semantics=sem, vmem_limit_bytes=VMEM_LIMIT_BYTES)


def _pick(n, cands):
    for c in cands:
        if n % c == 0:
            return c
    raise ValueError(f"no tile in {cands} divides {n}")


def _pad_cols(w, n):
    return jnp.pad(w, ((0, 0), (0, n - w.shape[1])))


def _mm_kernel(x_ref, w_ref, o_ref):
    o_ref[...] = jnp.dot(x_ref[...].astype(BF16), w_ref[...],
                         preferred_element_type=F32).astype(o_ref.dtype)


def _mm_bias_kernel(x_ref, xb_ref, w_ref, o_ref):
    x = (x_ref[...] + xb_ref[...]).astype(BF16)
    o_ref[...] = jnp.dot(x, w_ref[...], preferred_element_type=F32).astype(o_ref.dtype)


def _matmul(x, w, *, xbias=None, out_dtype=F32, tm_cap=512):
    M, K = x.shape
    N = w.shape[1]
    tm = _pick(M, tuple(t for t in (512, 256, 128, 64, 32, 16, 8) if t <= tm_cap))
    tn = N if N <= MAX_FULL_N else _pick(N, (1024, 512, 256, 128))
    x_spec = pl.BlockSpec((tm, K), lambda j, i: (i, 0))
    w_spec = pl.BlockSpec((K, tn), lambda j, i: (0, j))
    o_spec = pl.BlockSpec((tm, tn), lambda j, i: (i, j))
    if xbias is None:
        body, specs, args = _mm_kernel, [x_spec, w_spec], (x, w)
    else:
        b_spec = pl.BlockSpec((1, K), lambda j, i: (0, 0))
        body, specs, args = _mm_bias_kernel, [x_spec, b_spec, w_spec], (x, xbias, w)
    return pl.pallas_call(
        body, grid=(N // tn, M // tm), in_specs=specs, out_specs=o_spec,
        out_shape=jax.ShapeDtypeStruct((M, N), out_dtype),
        compiler_params=_cparams("parallel", "parallel"), name="proj")(*args)


def _deepnorm_ln(v, g, b):
    mu = jnp.mean(v, axis=-1, keepdims=True)
    d = v - mu
    var = jnp.mean(d * d, axis=-1, keepdims=True)
    return d * lax.rsqrt(var + LN_EPS) * g + b


def _mm_ln_kernel(x_ref, w_ref, r_ref, g_ref, b_ref, o_ref, *, alpha):
    y = jnp.dot(x_ref[...].astype(BF16), w_ref[...], preferred_element_type=F32)
    o_ref[...] = _deepnorm_ln(alpha * r_ref[...] + y, g_ref[...], b_ref[...])


def _matmul_ln(x, w, resid, g, b, alpha):
    M, K = x.shape
    D = w.shape[1]
    tm = _pick(M, (512, 256, 128, 64, 32, 16, 8))
    row = lambda i: (i, 0)
    fixed = lambda i: (0, 0)
    return pl.pallas_call(
        functools.partial(_mm_ln_kernel, alpha=alpha), grid=(M // tm,),
        in_specs=[pl.BlockSpec((tm, K), row), pl.BlockSpec((K, D), fixed),
                  pl.BlockSpec((tm, D), row), pl.BlockSpec((1, D), fixed), pl.BlockSpec((1, D), fixed)],
        out_specs=pl.BlockSpec((tm, D), row),
        out_shape=jax.ShapeDtypeStruct((M, D), F32),
        compiler_params=_cparams("parallel"), name="proj_ln")(x, w, resid, g.reshape(1, D), b.reshape(1, D))


def _softplus(x):
    return jnp.maximum(x, 0.0) + jnp.log1p(jnp.exp(-jnp.abs(x)))


def _silu(x):
    return x * jax.nn.sigmoid(x)


def _ssd_kernel(z_ref, xbc_ref, dtr_ref, cbuf_ref, h0_ref, cw_ref, cb_ref, dtb_ref, alog_ref,
                dsk_ref, nw_ref, y_ref, hn_ref, xf_scr, h_scr, *, Lc, H, P, N, G, n_valid, conv_w):
    c = pl.program_id(1)
    DI = H * P
    HG = H // G
    GP = HG * P
    tail = xf_scr.shape[0] - Lc

    @pl.when(c == 0)
    def _():
        xf_scr[0:tail, :] = cbuf_ref[0]
        h_scr[...] = h0_ref[0]

    xf_scr[tail:tail + Lc, :] = xbc_ref[0]
    conv = cb_ref[...]
    for k in range(conv_w):
        conv = conv + xf_scr[pl.ds(tail - (conv_w - 1) + k, Lc), :] * cw_ref[k:k + 1, :]
    conv = _silu(conv)
    xf_scr[0:tail, :] = xf_scr[pl.ds(Lc, tail), :]

    xs = conv[:, :DI]
    Bm = conv[:, DI:DI + G * N]
    Cm = conv[:, DI + G * N:]

    lane = lax.broadcasted_iota(I32, (Lc, LANES), 1)
    rowi = lax.broadcasted_iota(I32, (Lc, LANES), 0)
    dt = jnp.where((lane < H) & (rowi < n_valid), _softplus(dtr_ref[0] + dtb_ref[...]), 0.0)
    a = dt * (-jnp.exp(alog_ref[...]))
    ti = lax.broadcasted_iota(I32, (Lc, Lc), 0)
    si = lax.broadcasted_iota(I32, (Lc, Lc), 1)
    causal = ti >= si
    tril = jnp.where(causal, 1.0, 0.0).astype(F32)
    eye = jnp.where(lax.broadcasted_iota(I32, (LANES, LANES), 0) ==
                    lax.broadcasted_iota(I32, (LANES, LANES), 1), 1.0, 0.0).astype(F32)
    acum = jnp.dot(tril, a, precision=_HI, preferred_element_type=F32)
    acum_t = lax.dot_general(eye, acum, _NT, precision=_HI, preferred_element_type=F32)
    dt_t = lax.dot_general(eye, dt, _NT, precision=_HI, preferred_element_type=F32)
    alast = acum[Lc - 1:Lc, :]
    wend = jnp.exp(alast - acum) * dt
    ea = jnp.exp(acum)
    edec = jnp.exp(alast)
    dsk = dsk_ref[...]
    first_half = lax.broadcasted_iota(I32, (Lc, 2 * P), 1) < P

    def pair(v, h):
        rows = v.shape[0]
        return jnp.where(first_half[:rows], jnp.broadcast_to(v[:, h:h + 1], (rows, 2 * P)),
                         jnp.broadcast_to(v[:, h + 1:h + 2], (rows, 2 * P)))

    y_parts = []
    for g in range(G):
        Bg = Bm[:, g * N:(g + 1) * N].astype(BF16)
        Cg = Cm[:, g * N:(g + 1) * N].astype(BF16)
        cb = lax.dot_general(Cg, Bg, _NT, preferred_element_type=F32)
        Sg = h_scr[g * GP:(g + 1) * GP, :]
        yoff = lax.dot_general(Cg, Sg.astype(BF16), _NT, preferred_element_type=F32)
        xw_parts = []
        for pr in range(HG // 2):
            h = g * HG + 2 * pr
            xs_pair = xs[:, h * P:(h + 2) * P]
            xs_pair_b = xs_pair.astype(BF16)
            outs = []
            for hh in (h, h + 1):
                seg = acum[:, hh:hh + 1] - acum_t[hh:hh + 1, :]
                dec = jnp.exp(jnp.where(causal, seg, -jnp.inf))
                mix = cb * dec * dt_t[hh:hh + 1, :]
                outs.append(jnp.dot(mix.astype(BF16), xs_pair_b, preferred_element_type=F32))
            ydiag = jnp.where(first_half, outs[0], outs[1])
            y_pair = ydiag + yoff[:, pr * 2 * P:(pr + 1) * 2 * P] * pair(ea, h) + pair(dsk, h) * xs_pair
            y_parts.append(y_pair)
            xw_parts.append(xs_pair * pair(wend, h))
        xw = jnp.concatenate(xw_parts, axis=1).astype(BF16)
        states = lax.dot_general(xw, Bg, _TN, preferred_element_type=F32)
        for hl in range(HG):
            hh = g * HG + hl
            r0 = g * GP + hl * P
            h_scr[r0:r0 + P, :] = edec[:, hh:hh + 1] * Sg[hl * P:(hl + 1) * P, :] + states[hl * P:(hl + 1) * P, :]

    z = z_ref[0]
    gn = DI // G
    outs = []
    for g in range(G):
        parts = y_parts[g * (HG // 2):(g + 1) * (HG // 2)]
        yg = jnp.concatenate(parts, axis=1) if len(parts) > 1 else parts[0]
        yg = yg * _silu(z[:, g * gn:(g + 1) * gn])
        ms = jnp.mean(yg * yg, axis=-1, keepdims=True)
        outs.append(yg * lax.rsqrt(ms + 1e-5))
    y_ref[0] = jnp.concatenate(outs, axis=1) * nw_ref[...]

    @pl.when(c == pl.num_programs(1) - 1)
    def _():
        hn_ref[0] = h_scr[...]


def _ssd(z, xbc, dtr, cbuf, h0, conv_w, conv_b, dt_bias, a_log, d_skip, norm_w, *, Lc, n_valid, G):
    B, L, DI = z.shape
    CD = xbc.shape[-1]
    _, H, P, N = h0.shape
    KW = conv_w.shape[0]
    assert P * 2 == LANES and (H // G) % 2 == 0 and H <= LANES and L % Lc == 0
    padl = lambda v: jnp.pad(v.astype(F32).reshape(1, -1), ((0, 0), (0, LANES - H)))
    seq = lambda b, c: (b, c, 0)
    per_b = lambda b, c: (b, 0, 0)
    fixed = lambda b, c: (0, 0)
    kern = functools.partial(_ssd_kernel, Lc=Lc, H=H, P=P, N=N, G=G, n_valid=n_valid, conv_w=KW)
    y, hn = pl.pallas_call(
        kern, grid=(B, L // Lc),
        in_specs=[pl.BlockSpec((1, Lc, DI), seq), pl.BlockSpec((1, Lc, CD), seq),
                  pl.BlockSpec((1, Lc, LANES), seq), pl.BlockSpec((1, SUBLANES, CD), per_b),
                  pl.BlockSpec((1, H * P, N), per_b),
                  pl.BlockSpec((KW, CD), fixed), pl.BlockSpec((1, CD), fixed),
                  pl.BlockSpec((1, LANES), fixed), pl.BlockSpec((1, LANES), fixed),
                  pl.BlockSpec((1, LANES), fixed), pl.BlockSpec((1, DI), fixed)],
        out_specs=[pl.BlockSpec((1, Lc, DI), seq), pl.BlockSpec((1, H * P, N), per_b)],
        out_shape=[jax.ShapeDtypeStruct((B, L, DI), F32), jax.ShapeDtypeStruct((B, H * P, N), F32)],
        scratch_shapes=[pltpu.VMEM((SUBLANES + Lc, CD), F32), pltpu.VMEM((H * P, N), F32)],
        compiler_params=_cparams("parallel", "arbitrary"), name="ssd",
    )(z, xbc, dtr, cbuf, h0.reshape(B, H * P, N), conv_w, conv_b.reshape(1, CD),
      padl(dt_bias), padl(a_log), padl(d_skip), norm_w.reshape(1, DI))
    return y, hn.reshape(B, H, P, N)


def _mamba_layer(Xp, Xs, dims, state_conv_i, state_ssm_i, params, ln_g, ln_b, alpha):
    Bp, Lp, Bs, Ls = dims
    w_in, conv_w, conv_b, dt_bias, a_log, d_skip, norm_w, w_out = params
    Tp = Bp * Lp
    DI = norm_w.shape[0]
    CD = conv_w.shape[1]
    _, H, P, N = state_ssm_i.shape
    G = (CD - DI) // (2 * N)
    KW = conv_w.shape[0]
    Ts = Bs * Ls
    wz = w_in[:, :DI].astype(BF16)
    wx = w_in[:, DI:DI + CD].astype(BF16)
    wd = _pad_cols(w_in[:, DI + CD:], LANES).astype(BF16)
    wo = w_out.astype(BF16)
    args = (conv_w, conv_b, dt_bias, a_log, d_skip, norm_w)

    xbc_p = _matmul(Xp, wx).reshape(Bp, Lp, CD)
    Lc = SSD_CHUNK if Lp % SSD_CHUNK == 0 else Lp
    yp, hp = _ssd(_matmul(Xp, wz).reshape(Bp, Lp, DI), xbc_p, _matmul(Xp, wd).reshape(Bp, Lp, LANES),
                  jnp.zeros((Bp, SUBLANES, CD), F32), jnp.zeros((Bp, H, P, N), F32), *args,
                  Lc=Lc, n_valid=Lc, G=G)
    conv_p = xbc_p[:, Lp - (KW - 1):]
    Xp = _matmul_ln(yp.reshape(Tp, DI), wo, Xp, ln_g, ln_b, alpha)

    Lsp = -(-Ls // SUBLANES) * SUBLANES
    pad_rows = lambda v: jnp.pad(v[:Ts].reshape(Bs, Ls, -1), ((0, 0), (0, Lsp - Ls), (0, 0)))
    xbc_s = _matmul(Xs, wx)[:Ts].reshape(Bs, Ls, CD)
    cbuf_s = jnp.pad(state_conv_i, ((0, 0), (SUBLANES - (KW - 1), 0), (0, 0)))
    ys, hs = _ssd(pad_rows(_matmul(Xs, wz)), pad_rows(xbc_s), pad_rows(_matmul(Xs, wd)), cbuf_s, state_ssm_i,
                  *args, Lc=Lsp, n_valid=Ls, G=G)
    conv_s = jnp.concatenate([state_conv_i, xbc_s], axis=1)[:, -(KW - 1):]
    ys = jnp.pad(ys[:, :Ls].reshape(Ts, DI), ((0, Xs.shape[0] - Ts), (0, 0)))
    Xs = _matmul_ln(ys, wo, Xs, ln_g, ln_b, alpha)
    return Xp, Xs, hp, hs, conv_p, conv_s


def _topk_rows(s, k, between_rounds=lambda: None):
    n = float(s.shape[0])
    rows = lax.broadcasted_iota(I32, s.shape, 0).astype(F32)
    vals, idxs = [], []
    for _ in range(k):
        m = jnp.max(s, axis=0, keepdims=True)
        i = jnp.min(jnp.where(s == m, rows, n), axis=0, keepdims=True)
        vals.append(m)
        idxs.append(i)
        s = jnp.where(rows == i, -jnp.inf, s)
        between_rounds()
    return vals, idxs


def _peer_route_kernel(q_ref, k_ref, w_ref, i1_scr, i2_scr, g_scr, *, heads, half, topk, n_keys, nj):
    tb = q_ref.shape[0]

    @pl.when(pl.program_id(0) == 0)
    def _():
        i1_scr[...] = jnp.zeros(i1_scr.shape, I32)
        i2_scr[...] = jnp.zeros(i2_scr.shape, I32)
        g_scr[...] = jnp.zeros(g_scr.shape, F32)

    sub = lax.broadcasted_iota(I32, (n_keys, i1_scr.shape[1]), 0)
    pending = list(range(tb))
    n_rounds = heads * 3 * topk
    every = max(1, n_rounds // tb - 1)
    calls = [0]

    def w_token():
        t = pending.pop(0)
        onehot1 = jnp.where(sub == i1_scr[t:t + 1, :], 1.0, 0.0).astype(BF16)
        gated2 = jnp.where(sub == i2_scr[t:t + 1, :], g_scr[t:t + 1, :], 0.0).astype(BF16)
        w_t = lax.dot_general(onehot1, gated2, _NT, preferred_element_type=F32)
        for k in range(n_keys // nj):
            w_ref[0, k, t * nj:(t + 1) * nj, :] = w_t[k * nj:(k + 1) * nj, :]

    def between_rounds():
        calls[0] += 1
        if pending and calls[0] % every == 0:
            w_token()

    i1, i2, gate = _peer_topk_block(q_ref, k_ref, heads=heads, half=half, topk=topk, n_keys=n_keys,
                                    between_rounds=between_rounds)
    while pending:
        w_token()
    i1_scr[...] = i1
    i2_scr[...] = i2
    g_scr[...] = gate


def _peer_topk_block(q_ref, k_ref, *, heads, half, topk, n_keys, between_rounds):
    k1 = k_ref[0]
    k2 = k_ref[1]
    tb = q_ref.shape[0]
    cand_ab = [(a, b) for a in range(topk) for b in range(topk // (a + 1))]
    n_pad = -len(cand_ab) % SUBLANES
    codes, gates = [], []
    for h in range(heads):
        q1 = q_ref[:, (2 * h) * half:(2 * h + 1) * half]
        q2 = q_ref[:, (2 * h + 1) * half:(2 * h + 2) * half]
        s1 = lax.dot_general(k1, q1, _NT, preferred_element_type=F32)
        s2 = lax.dot_general(k2, q2, _NT, preferred_element_type=F32)
        v1, i1 = _topk_rows(s1, topk, between_rounds)
        v2, i2 = _topk_rows(s2, topk, between_rounds)
        cand = jnp.concatenate([v1[a] + v2[b] for a, b in cand_ab] +
                               [jnp.full((n_pad, tb), -jnp.inf, F32)], axis=0)
        code = jnp.concatenate([i1[a] * float(n_keys) + i2[b] for a, b in cand_ab] +
                               [jnp.zeros((n_pad, tb), F32)], axis=0)
        sc, pos = _topk_rows(cand, topk, between_rounds)
        crow = lax.broadcasted_iota(I32, cand.shape, 0).astype(F32)
        codes += [jnp.max(jnp.where(crow == p, code, -1.0), axis=0, keepdims=True) for p in pos]
        ex = jnp.exp(jnp.concatenate(sc, axis=0) - sc[0])
        gates.append(ex / jnp.sum(ex, axis=0, keepdims=True))
    code_t = jnp.concatenate(codes, axis=0).T.astype(I32)
    return code_t // n_keys, code_t % n_keys, jnp.concatenate(gates, axis=0).T


def _gelu_tanh(x):
    hx = 0.5 * x
    inner = x * (GELU_C0 + GELU_C1 * (x * x))
    return hx * jnp.tanh(inner) + hx


def _peer_dense_kernel(x_ref, u_ref, v_ref, w_ref, g_ref, b_ref, o_ref, acc_ref, xb_ref, *, alpha, nj):
    k = pl.program_id(1)
    td = x_ref.shape[0]

    @pl.when(k == 0)
    def _():
        acc_ref[...] = jnp.zeros_like(acc_ref)
        xb_ref[...] = x_ref[...].astype(BF16)

    act = lax.dot_general(xb_ref[...], u_ref[...], _NT, preferred_element_type=F32)
    w = jnp.concatenate([w_ref[0, 0, pl.ds(j, td, stride=nj), :] for j in range(nj)], axis=1)
    coef = (_gelu_tanh(act) * w).astype(BF16)
    acc_ref[...] += jnp.dot(coef, v_ref[...], preferred_element_type=F32)

    @pl.when(k == pl.num_programs(1) - 1)
    def _():
        o_ref[...] = _deepnorm_ln(alpha * x_ref[...] + acc_ref[...], g_ref[...], b_ref[...])


def _peer_layer(X, w_q, sub_keys, u_tab, v_tab, ln_g, ln_b, alpha):
    T, D = X.shape
    _, n_keys, half = sub_keys.shape
    heads = w_q.shape[1] // (2 * half)
    nsel = heads * PEER_TOPK
    assert n_keys == LANES and nsel == LANES and half % LANES == 0
    q = _matmul(X, w_q, out_dtype=BF16)
    sub_keys = sub_keys.astype(BF16)

    tb = LANES
    td = _pick(T, (1024, 768, 512, 256, 128))
    nj = SUBLANES
    et = nj * n_keys
    sub_blocks = td // tb
    n_blocks = T // tb
    prev = lambda i: jnp.maximum(i - 1, 0)
    w = pl.pallas_call(
        functools.partial(_peer_route_kernel, heads=heads, half=half, topk=PEER_TOPK, n_keys=n_keys, nj=nj),
        grid=(n_blocks + 1,),
        in_specs=[pl.BlockSpec((tb, q.shape[1]), lambda i: (jnp.minimum(i, n_blocks - 1), 0)),
                  pl.BlockSpec((2, n_keys, half), lambda i: (0, 0, 0))],
        out_specs=pl.BlockSpec((1, n_keys // nj, tb * nj, n_keys),
                               lambda i: (prev(i) // sub_blocks, 0, prev(i) % sub_blocks, 0)),
        out_shape=jax.ShapeDtypeStruct((T // td, n_keys // nj, td * nj, n_keys), F32),
        scratch_shapes=[pltpu.VMEM((tb, nsel), I32), pltpu.VMEM((tb, nsel), I32), pltpu.VMEM((tb, nsel), F32)],
        compiler_params=_cparams("arbitrary"), name="peer_route")(q, sub_keys)

    return pl.pallas_call(
        functools.partial(_peer_dense_kernel, alpha=alpha, nj=nj),
        grid=(T // td, n_keys // nj),
        in_specs=[pl.BlockSpec((td, D), lambda i, k: (i, 0)),
                  pl.BlockSpec((et, D), lambda i, k: (k, 0)),
                  pl.BlockSpec((et, D), lambda i, k: (k, 0)),
                  pl.BlockSpec((1, 1, td * nj, n_keys), lambda i, k: (i, k, 0, 0)),
                  pl.BlockSpec((1, D), lambda i, k: (0, 0)), pl.BlockSpec((1, D), lambda i, k: (0, 0))],
        out_specs=pl.BlockSpec((td, D), lambda i, k: (i, 0)),
        out_shape=jax.ShapeDtypeStruct((T, D), F32),
        scratch_shapes=[pltpu.VMEM((td, D), F32), pltpu.VMEM((td, D), BF16)],
        compiler_params=_cparams("parallel", "arbitrary"), name="peer_dense",
    )(X, u_tab, v_tab, w, ln_g.reshape(1, D), ln_b.reshape(1, D))


def _gather_kernel(pt_ref, pa_ref, pb_ref, new_ref, kc_ref, vc_ref, ks_ref, vs_ref, col_scr, *, n_steps, cw, blk):
    del pt_ref
    p = pl.program_id(1)
    ps = pa_ref.shape[1]
    per_page = ps // blk

    @pl.when(p < n_steps)
    def _():
        for j, page in enumerate((pa_ref, pb_ref)):
            n_tiles = 2 * cw // LANES
            for c in range(n_tiles):
                col_scr[j * n_tiles + c] = page[0, :, c * LANES:(c + 1) * LANES]
            for r, ref in enumerate((kc_ref, vc_ref)):
                for l in range(blk):
                    for c in range(cw // LANES):
                        ref[0, j * per_page:(j + 1) * per_page, l * cw + c * LANES:l * cw + (c + 1) * LANES] = \
                            col_scr[j * n_tiles + r * (cw // LANES) + c, pl.ds(l, per_page, stride=blk), :]
            ks_ref[0, j * ps:(j + 1) * ps, :] = page[0, :, 2 * cw:3 * cw].astype(BF16)
            vs_ref[0, j * ps:(j + 1) * ps, :] = page[0, :, 3 * cw:4 * cw].astype(BF16)

    @pl.when(p == n_steps)
    def _():
        new = new_ref[0]
        n_new = new.shape[0]
        for r, ref in enumerate((kc_ref, vc_ref)):
            ref[0] = jnp.zeros(ref.shape[1:], F32)
            for l in range(n_new):
                ref[0, 0:1, l * cw:(l + 1) * cw] = new[l:l + 1, r * cw:(r + 1) * cw]
        for r, ref in ((2, ks_ref), (3, vs_ref)):
            rows = jnp.concatenate([new[:, r * cw:(r + 1) * cw], jnp.zeros((2 * ps - n_new, cw), F32)], axis=0)
            ref[0] = rows.astype(BF16)


def _gather_pages(pages, page_table, new_rows, blk):
    NP, PS, C = pages.shape
    B, n_pages = page_table.shape
    assert n_pages % 2 == 0 and PS % blk == 0 and (2 * PS // blk) % SUBLANES == 0 and new_rows.shape[1] <= blk
    cw = C // 4
    n_steps = n_pages // 2
    L = (n_pages + 2) * PS
    nb = 2 * PS // blk
    page_spec = lambda j: pl.BlockSpec(
        (1, PS, C), lambda b, p, pt: (pt[b, jnp.minimum(2 * p + j, n_pages - 2 + j)], 0, 0))
    cmp_spec = pl.BlockSpec((1, nb, blk * cw), lambda b, p, pt: (b, p, 0))
    sel_spec = pl.BlockSpec((1, 2 * PS, cw), lambda b, p, pt: (b, p, 0))
    return pl.pallas_call(
        functools.partial(_gather_kernel, n_steps=n_steps, cw=cw, blk=blk),
        grid_spec=pltpu.PrefetchScalarGridSpec(
            num_scalar_prefetch=1, grid=(B, n_steps + 1),
            in_specs=[page_spec(0), page_spec(1),
                      pl.BlockSpec((1, new_rows.shape[1], C), lambda b, p, pt: (b, 0, 0))],
            out_specs=[cmp_spec, cmp_spec, sel_spec, sel_spec],
            scratch_shapes=[pltpu.VMEM((2 * (2 * cw // LANES), PS, LANES), F32)]),
        out_shape=[jax.ShapeDtypeStruct((B, L // blk, blk * cw), F32)] * 2 +
                  [jax.ShapeDtypeStruct((B, L, cw), BF16)] * 2,
        compiler_params=_cparams("parallel", "arbitrary"), name="kv_gather")(page_table, pages, pages, new_rows)


def _compress_blocks(xb, w_c, pos_c, nc_pad):
    B, n, K = xb.shape
    blk, dh, _ = w_c.shape
    C = K // blk
    G = C // dh
    w_bd = jnp.einsum('lde,gh->lgdhe', w_c, jnp.eye(G, dtype=w_c.dtype)).reshape(K, C)
    xbias = jnp.broadcast_to(pos_c[:, None, :], (blk, G, dh)).reshape(1, K)
    out = _matmul(xb.reshape(B * n, K), w_bd.astype(BF16), xbias=xbias, tm_cap=128, out_dtype=BF16)
    return jnp.pad(out.reshape(B, n, C), ((0, 0), (0, nc_pad - n), (0, 0)))


def _compress(x, w_c, pos_c, nc_pad):
    B, L, C = x.shape
    blk = w_c.shape[0]
    return _compress_blocks(x.reshape(B, L // blk, blk * C), w_c, pos_c, nc_pad)


def _lane_tiles(x):
    return [x[:, j * LANES:(j + 1) * LANES] for j in range(x.shape[1] // LANES)]


def _row_max(x):
    return jnp.max(functools.reduce(jnp.maximum, _lane_tiles(x)), axis=-1, keepdims=True)


def _row_sum(x):
    return jnp.sum(functools.reduce(jnp.add, _lane_tiles(x)), axis=-1, keepdims=True)


def _split3_dot(x, m):
    x1 = x.astype(BF16)
    r1 = x - x1.astype(F32)
    x2 = r1.astype(BF16)
    x3 = (r1 - x2.astype(F32)).astype(BF16)
    d = lambda a: jnp.dot(a, m, preferred_element_type=F32)
    return d(x1) + d(x2) + d(x3)


def _nsa_staged_kernel(q_ref, gt_ref, kc_ref, vc_ref, ks_ref, vs_ref, kw_ref, vw_ref, ex_ref, o_ref, *,
                       seqs, nq, t0, tk, n_cmp, cmp_blk, n_sel, kvh, hg, dh, slopes, win_rows, w0):
    q0 = t0 + pl.program_id(1) * nq
    R = hg * nq
    NC = kc_ref.shape[1]
    Lk = ks_ref.shape[1]
    NSP = ex_ref.shape[0]
    assert tk == Lk
    qscale = dh ** -0.5 * LOG2E
    slopes2 = tuple(s * LOG2E for s in slopes)
    units = [(s, g) for s in range(seqs) for g in range(kvh)]
    col = lambda g: (g // 2) * 2 * dh

    def tq_like(shape):
        return q0 + lax.broadcasted_iota(I32, shape, 0)

    def attend(keys, vals, distms):
        s_alls = [lax.dot_general(qgs[u], keys[u], _NT, preferred_element_type=F32) for u in range(len(units))]
        s2s = [[s_alls[u][h * nq:(h + 1) * nq] - slopes2[g * hg + h] * distms[u] for h in range(hg)]
               for u, (_, g) in enumerate(units)]
        ms = [[_row_max(x) for x in row] for row in s2s]
        es = [[jnp.exp2(x - jnp.where(m == -jnp.inf, 0.0, m)) for x, m in zip(xr, mr)] for xr, mr in zip(s2s, ms)]
        ls = [[_row_sum(e) for e in row] for row in es]
        ps = [[e / jnp.maximum(l, 1e-30) for e, l in zip(er, lr)] for er, lr in zip(es, ls)]
        os = [jnp.dot(jnp.concatenate(ps[u], axis=0).astype(BF16), vals[u], preferred_element_type=F32)
              for u in range(len(units))]
        return ps, os

    n_idx = lax.broadcasted_iota(I32, (nq, NC), 1)
    dist_c = tq_like((nq, NC)) - (n_idx * cmp_blk + (cmp_blk - 1))
    distm_c = jnp.where((dist_c >= 0) & (n_idx < n_cmp), dist_c.astype(F32), jnp.inf)
    dist_w = tq_like((nq, win_rows)) - (w0 + lax.broadcasted_iota(I32, (nq, win_rows), 1))
    distm_w = jnp.where((dist_w >= 0) & (dist_w < WINDOW), dist_w.astype(F32), jnp.inf)
    dist_s = tq_like((nq, Lk)) - lax.broadcasted_iota(I32, (nq, Lk), 1)
    per_sel = SEL_BLOCK // cmp_blk
    pair = jnp.where(lax.broadcasted_iota(I32, (NC, NSP), 0) // per_sel ==
                     lax.broadcasted_iota(I32, (NC, NSP), 1), 1.0, 0.0).astype(BF16)
    blk = lax.broadcasted_iota(I32, (nq, NSP), 1)
    tqs = tq_like((nq, NSP))
    cur = tqs // SEL_BLOCK
    forced = (blk == 0) | (blk == cur) | (blk == cur - 1)
    sel_valid = blk * SEL_BLOCK <= tqs

    qgs = [(jnp.concatenate([q_ref[s, :, (g * hg + h) * 2 * dh:(g * hg + h + 1) * 2 * dh] for h in range(hg)],
                            axis=0) * qscale).astype(BF16) for s, g in units]
    tile = lambda ref, s, g: ref[s, :, col(g):col(g) + 2 * dh]
    p_cmp, o_cmp = attend([tile(kc_ref, s, g).astype(BF16) for s, g in units],
                          [tile(vc_ref, s, g).astype(BF16) for s, g in units], [distm_c] * len(units))
    _, o_win = attend([tile(kw_ref, s, g) for s, g in units], [tile(vw_ref, s, g) for s, g in units],
                      [distm_w] * len(units))

    psums = [functools.reduce(jnp.add, row) for row in p_cmp]
    imps = [_split3_dot(x, pair) for x in psums]
    scores = [jnp.where(blk < n_sel, jnp.where(forced, FORCE_SCORE, jnp.where(sel_valid, imp, -1.0)), -2.0)
              for imp in imps]
    selm = _top_blocks(jnp.concatenate(scores, axis=0), min(N_SEL, n_sel), NSP)
    selb = selm.astype(BF16)
    expand = ex_ref[...]
    selxs = [jnp.dot(selb[u * nq:(u + 1) * nq], expand, preferred_element_type=F32) for u in range(len(units))]
    distm_s = [jnp.where((dist_s >= 0) & (x > 0.5), dist_s.astype(F32), jnp.inf) for x in selxs]
    _, o_sel = attend([tile(ks_ref, s, g) for s, g in units], [tile(vs_ref, s, g) for s, g in units], distm_s)

    lane = lax.broadcasted_iota(I32, (nq, 2 * dh), 1)
    for s in range(seqs):
        gates = jax.nn.sigmoid(gt_ref[s])
        for g in range(kvh):
            u = s * kvh + g
            lo = (g % 2) * dh
            keep = (lane >= lo) & (lane < lo + dh)
            for h in range(hg):
                hd = g * hg + h
                rows = slice(h * nq, (h + 1) * nq)
                o = (gates[:, 3 * hd:3 * hd + 1] * o_cmp[u][rows] + gates[:, 3 * hd + 1:3 * hd + 2] * o_sel[u][rows] +
                     gates[:, 3 * hd + 2:3 * hd + 3] * o_win[u][rows])
                o_ref[s, :, hd * 2 * dh:(hd + 1) * 2 * dh] = jnp.where(keep, o, 0.0)


def _top_blocks(score, n_top, nsp):
    if score.shape[0] % LANES == 0 and nsp == LANES:
        st = score.T
        brow = lax.broadcasted_iota(I32, st.shape, 0).astype(F32)
        sel_t = jnp.zeros(st.shape, F32)
        for _ in range(n_top):
            m = jnp.max(st, axis=0, keepdims=True)
            idx = jnp.min(jnp.where(st == m, brow, float(nsp)), axis=0, keepdims=True)
            hit = brow == idx
            sel_t = jnp.where(hit, 1.0, sel_t)
            st = jnp.where(hit, -jnp.inf, st)
        return sel_t.T
    bcol = lax.broadcasted_iota(I32, score.shape, 1)
    selm = jnp.zeros(score.shape, F32)
    for _ in range(n_top):
        m = jnp.max(score, axis=-1, keepdims=True)
        idx = jnp.min(jnp.where(score == m, bcol, nsp), axis=-1, keepdims=True)
        hit = bcol == idx
        selm = jnp.where(hit, 1.0, selm)
        score = jnp.where(hit, -jnp.inf, score)
    return selm


def _nsa_tiled_kernel(q_ref, gt_ref, kc_ref, vc_ref, ks_ref, vs_ref, kw_ref, vw_ref, ex_ref, o_ref,
                      m_scr, l_scr, acc_scr, flag_ref, *,
                      seqs, nq, t0, tk, n_cmp, cmp_blk, n_sel, kvh, hg, dh, slopes, win_rows, w0):
    assert seqs == 1
    i = pl.program_id(1)
    q0 = t0 + i * nq
    R = hg * nq
    NC = kc_ref.shape[1]
    Lk = ks_ref.shape[1]
    NSP = ex_ref.shape[0]
    qscale = dh ** -0.5 * LOG2E
    slopes2 = tuple(s * LOG2E for s in slopes)
    gates = jax.nn.sigmoid(gt_ref[0])

    def tq_like(shape):
        return q0 + lax.broadcasted_iota(I32, shape, 0)

    wstart = pl.multiple_of(jnp.maximum(q0 - WINDOW, 0), nq)

    def col(g):
        return (g // 2) * 2 * dh

    groups = range(kvh)

    def attend(keys, vals, distm, want_p):
        s_alls = [lax.dot_general(qgs[g], keys[g], _NT, preferred_element_type=F32) for g in groups]
        s2s = [[s_alls[g][h * nq:(h + 1) * nq] - slopes2[g * hg + h] * distm for h in range(hg)] for g in groups]
        ms = [[_row_max(x) for x in row] for row in s2s]
        es = [[jnp.exp2(x - jnp.where(m == -jnp.inf, 0.0, m)) for x, m in zip(xr, mr)] for xr, mr in zip(s2s, ms)]
        ls = [[jnp.maximum(_row_sum(e), 1e-30) for e in row] for row in es]
        if want_p:
            ps = [[e / l for e, l in zip(er, lr)] for er, lr in zip(es, ls)]
            os = [jnp.dot(jnp.concatenate(ps[g], axis=0).astype(BF16), vals[g], preferred_element_type=F32)
                  for g in groups]
            return ps, os
        os = [jnp.dot(jnp.concatenate(es[g], axis=0).astype(BF16), vals[g], preferred_element_type=F32) /
              jnp.concatenate(ls[g], axis=0) for g in groups]
        return None, os

    n_idx = lax.broadcasted_iota(I32, (nq, NC), 1)
    dist_c = tq_like((nq, NC)) - (n_idx * cmp_blk + (cmp_blk - 1))
    distm_c = jnp.where((dist_c >= 0) & (n_idx < n_cmp), dist_c.astype(F32), jnp.inf)
    dist_w = tq_like((nq, win_rows)) - (w0 + wstart + lax.broadcasted_iota(I32, (nq, win_rows), 1))
    distm_w = jnp.where((dist_w >= 0) & (dist_w < WINDOW), dist_w.astype(F32), jnp.inf)
    per_sel = SEL_BLOCK // cmp_blk
    pair = jnp.where(lax.broadcasted_iota(I32, (NC, NSP), 0) // per_sel ==
                     lax.broadcasted_iota(I32, (NC, NSP), 1), 1.0, 0.0).astype(BF16)
    blk = lax.broadcasted_iota(I32, (nq, NSP), 1)
    tqs = tq_like((nq, NSP))
    cur = tqs // SEL_BLOCK
    forced = (blk == 0) | (blk == cur) | (blk == cur - 1)
    sel_valid = blk * SEL_BLOCK <= tqs

    qgs = [(jnp.concatenate([q_ref[0, :, (g * hg + h) * 2 * dh:(g * hg + h + 1) * 2 * dh] for h in range(hg)],
                            axis=0) * qscale).astype(BF16) for g in groups]
    p_cmps, o_cmps = attend([kc_ref[0, :, col(g):col(g) + 2 * dh].astype(BF16) for g in groups],
                            [vc_ref[0, :, col(g):col(g) + 2 * dh].astype(BF16) for g in groups], distm_c, True)
    _, o_wins = attend([kw_ref[0, pl.ds(wstart, win_rows), col(g):col(g) + 2 * dh] for g in groups],
                       [vw_ref[0, pl.ds(wstart, win_rows), col(g):col(g) + 2 * dh] for g in groups], distm_w, False)
    imps = [_split3_dot(functools.reduce(jnp.add, p_cmps[g]), pair) for g in groups]
    scores = [jnp.where(blk < n_sel, jnp.where(forced, FORCE_SCORE, jnp.where(sel_valid, imp, -1.0)), -2.0)
              for imp in imps]

    selm = _top_blocks(jnp.concatenate(scores, axis=0), min(N_SEL, n_sel), NSP)
    selb = selm.astype(BF16)

    def sel_tile_group(kt, g):
        k0 = pl.multiple_of(kt * tk, tk)
        c0 = col(g)
        dist = tq_like((nq, tk)) - (k0 + lax.broadcasted_iota(I32, (nq, tk), 1))
        selx = jnp.dot(selb[g * nq:(g + 1) * nq], ex_ref[:, pl.ds(k0, tk)], preferred_element_type=F32)
        distm = jnp.where((dist >= 0) & (selx > 0.5), dist.astype(F32), jnp.inf)
        s_t = lax.dot_general(qgs[g], ks_ref[0, pl.ds(k0, tk), c0:c0 + 2 * dh], _NT,
                              preferred_element_type=F32)
        m_run = m_scr[g]
        l_run = l_scr[g]
        heads = range(hg)
        rows = [slice(h * nq, (h + 1) * nq) for h in heads]
        s2s = [s_t[rows[h]] - slopes2[g * hg + h] * distm for h in heads]
        tmax = [_row_max(x) for x in s2s]
        m_new = [jnp.maximum(m_run[rows[h]], tmax[h]) for h in heads]
        m_safe = [jnp.where(m == -jnp.inf, 0.0, m) for m in m_new]
        alphas = [jnp.exp2(m_run[rows[h]] - m_safe[h]) for h in heads]
        ps = [jnp.exp2(s2s[h] - jnp.concatenate([m_safe[h]] * (tk // LANES), axis=1)) for h in heads]
        l_new = [alphas[h] * l_run[rows[h]] + _row_sum(ps[h]) for h in heads]
        pv = jnp.dot(jnp.concatenate(ps, axis=0).astype(BF16), vs_ref[0, pl.ds(k0, tk), c0:c0 + 2 * dh],
                     preferred_element_type=F32)
        m_scr[g] = jnp.concatenate(m_new, axis=0)
        l_scr[g] = jnp.concatenate(l_new, axis=0)
        acc_scr[g] = jnp.concatenate(alphas, axis=0) * acc_scr[g] + pv

    n_flag = Lk // tk
    blocks_per_tile = tk // SEL_BLOCK
    lane_b = lax.broadcasted_iota(I32, (1, NSP), 1)
    for g in range(kvh):
        any_q = jnp.max(selm[g * nq:(g + 1) * nq], axis=0, keepdims=True)
        for kt in range(n_flag):
            in_tile = (lane_b >= kt * blocks_per_tile) & (lane_b < (kt + 1) * blocks_per_tile)
            flag_ref[g * n_flag + kt] = (jnp.max(jnp.where(in_tile, any_q, 0.0)) > 0.5).astype(I32)
    m_scr[...] = jnp.full(m_scr.shape, -jnp.inf, F32)
    l_scr[...] = jnp.zeros(l_scr.shape, F32)
    acc_scr[...] = jnp.zeros(acc_scr.shape, F32)

    def visit(kt, carry):
        for g in range(kvh):
            @pl.when(flag_ref[g * n_flag + kt] > 0)
            def _():
                sel_tile_group(kt, g)
        return carry

    lax.fori_loop(0, (q0 + nq + tk - 1) // tk, visit, 0)

    lane = lax.broadcasted_iota(I32, (nq, 2 * dh), 1)
    for g in range(kvh):
        lo = (g % 2) * dh
        keep = (lane >= lo) & (lane < lo + dh)
        o_sel = acc_scr[g] / jnp.maximum(l_scr[g], 1e-30)
        for h in range(hg):
            hd = g * hg + h
            rows = slice(h * nq, (h + 1) * nq)
            o = (gates[:, 3 * hd:3 * hd + 1] * o_cmps[g][rows] + gates[:, 3 * hd + 1:3 * hd + 2] * o_sel[rows] +
                 gates[:, 3 * hd + 2:3 * hd + 3] * o_wins[g][rows])
            o_ref[0, :, hd * 2 * dh:(hd + 1) * 2 * dh] = jnp.where(keep, o, 0.0)


def _nsa_attend(q, gt, kc, vc, ks, vs, kw, vw, cols, *, nq, t0, tk, n_cmp, cmp_blk, n_sel, kvh, dh,
                win_rows, w0, causal_tiles, seqs=1):
    B, Lq, QW = q.shape
    H = QW // (2 * dh)
    hg = H // kvh
    cw = kvh * dh
    Lk = ks.shape[1]
    NC = kc.shape[1]
    nsp = LANES
    assert n_sel <= nsp and Lk % tk == 0 and Lq % nq == 0 and dh * 2 == LANES
    expand = (lax.broadcasted_iota(I32, (nsp, Lk), 1) // SEL_BLOCK ==
              lax.broadcasted_iota(I32, (nsp, Lk), 0)).astype(BF16)
    slopes = tuple(2.0 ** (-8.0 * (h + 1) / H) for h in range(H))
    assert B % seqs == 0
    cfg = dict(seqs=seqs, nq=nq, t0=t0, tk=tk, n_cmp=n_cmp, cmp_blk=cmp_blk, n_sel=n_sel, kvh=kvh, hg=hg, dh=dh,
               slopes=slopes, win_rows=win_rows, w0=w0)
    R = hg * nq
    if causal_tiles:
        kern = functools.partial(_nsa_tiled_kernel, **cfg)
        scratch = [pltpu.VMEM((kvh, R, LANES), F32), pltpu.VMEM((kvh, R, LANES), F32),
                   pltpu.VMEM((kvh, R, 2 * dh), F32), pltpu.SMEM((kvh * (Lk // tk),), I32)]
    else:
        assert win_rows == kw.shape[1]
        kern = functools.partial(_nsa_staged_kernel, **cfg)
        scratch = []
    qmap = lambda b, i: (b, i, 0)
    kv_spec = lambda arr, c: pl.BlockSpec((seqs, arr.shape[1], cw), lambda b, i, c=c: (b, 0, c))
    return pl.pallas_call(
        kern, grid=(B // seqs, Lq // nq),
        in_specs=[pl.BlockSpec((seqs, nq, QW), qmap), pl.BlockSpec((seqs, nq, LANES), qmap),
                  kv_spec(kc, cols[0]), kv_spec(vc, cols[1]), kv_spec(ks, cols[2]), kv_spec(vs, cols[3]),
                  kv_spec(kw, cols[4]), kv_spec(vw, cols[5]),
                  pl.BlockSpec((nsp, Lk), lambda b, i: (0, 0))],
        out_specs=pl.BlockSpec((seqs, nq, QW), qmap),
        out_shape=jax.ShapeDtypeStruct((B, Lq, QW), F32),
        scratch_shapes=scratch,
        compiler_params=_cparams("parallel", "arbitrary"), name="nsa_attend")(q, gt, kc, vc, ks, vs, kw, vw, expand)


def _head_tiles(w, H, kvh, dh, axis):
    hg = H // kvh
    parts = []
    for h in range(H):
        sl = [slice(None)] * w.ndim
        sl[axis] = slice(h * dh, (h + 1) * dh)
        piece = w[tuple(sl)]
        zero = jnp.zeros_like(piece)
        parts += [zero, piece] if ((h // hg) % 2) else [piece, zero]
    return jnp.concatenate(parts, axis=axis)


def kernel(x_prompt, x_sample, cache_kv_pages, cache_win, state_ssm, state_conv, page_table, ln_g, ln_b, m_w_in, m_conv_w, m_conv_b, m_dt_bias, m_a_log, m_d_skip, m_norm_w, m_w_out, w_kv_shared, w_cmp, pos_cmp, nsa_w_in, nsa_w_out, peer_w_q, peer_sub_keys, peer_u, peer_v):
    Bp, Lp, D = x_prompt.shape
    Bs, Ls, _ = x_sample.shape
    depth = ln_g.shape[0]
    n_a = m_w_in.shape[0]
    alpha = (2 * depth) ** DEPTH_ALPHA_POW
    Tp, Ts = Bp * Lp, Bs * Ls
    assert Tp % LANES == 0
    Tsp = -(-Ts // LANES) * LANES
    Xp = x_prompt.reshape(Tp, D)
    Xs = jnp.pad(x_sample.reshape(Ts, D), ((0, Tsp - Ts), (0, 0)))
    dims = (Bp, Lp, Bs, Ls)

    NP, PS, n_rows, kvh, dh = cache_kv_pages.shape
    cw = kvh * dh
    n_pages = page_table.shape[1]
    past_len = n_pages * PS
    win_cache = cache_win.shape[1]
    cmp_blk = w_cmp.shape[1]
    H = nsa_w_in.shape[2] // (dh + 3)

    ssm_p, ssm_s, conv_p, conv_s = [], [], [], []
    for i in range(depth):
        if i < n_a:
            mp = (m_w_in[i], m_conv_w[i], m_conv_b[i], m_dt_bias[i], m_a_log[i], m_d_skip[i], m_norm_w[i], m_w_out[i])
            Xp, Xs, hp, hs, cp, cs = _mamba_layer(Xp, Xs, dims, state_conv[i], state_ssm[i], mp,
                                                  ln_g[i, 0], ln_b[i, 0], alpha)
            ssm_p.append(hp)
            ssm_s.append(hs)
            conv_p.append(cp)
            conv_s.append(cs)
        else:
            if i == n_a:
                wkv = w_kv_shared.astype(BF16)
                kv_p = _matmul(Xp, wkv).reshape(Bp, Lp, 6 * cw)
                kv_s = _matmul(Xs, wkv)[:Ts].reshape(Bs, Ls, 6 * cw)
                kvb_p = kv_p.astype(BF16)
                n_sel_p = -(-Lp // SEL_BLOCK)
                Lpp = n_sel_p * SEL_BLOCK
                padp = lambda v: jnp.pad(v, ((0, 0), (0, Lpp - Lp), (0, 0)))
                nc_p = -(-(Lpp // cmp_blk) // LANES) * LANES
                kc_p = _compress(padp(kv_p[:, :, 0:cw]), w_cmp[0], pos_cmp[0], nc_p)
                vc_p = _compress(padp(kv_p[:, :, cw:2 * cw]), w_cmp[1], pos_cmp[1], nc_p)
                tk_p = _pick(Lp, (512, 256, 128))
                new_rows = jnp.pad(kv_s[:, :, :4 * cw], ((0, 0), (0, SUBLANES - Ls), (0, 0)))
                kcb_s, vcb_s, ks_s, vs_s = _gather_pages(cache_kv_pages.reshape(NP, PS, n_rows * cw), page_table,
                                                         new_rows, cmp_blk)
                Lk_s = ks_s.shape[1]
                n_sel_s = -(-(past_len + Ls) // SEL_BLOCK)
                nc_s = -(-(Lk_s // cmp_blk) // LANES) * LANES
                kc_s = _compress_blocks(kcb_s, w_cmp[0], pos_cmp[0], nc_s)
                vc_s = _compress_blocks(vcb_s, w_cmp[1], pos_cmp[1], nc_s)
                win_full_s = jnp.concatenate([cache_win, kv_s[:, :, 4 * cw:].reshape(Bs, Ls, 2, kvh, dh)], axis=1)
                Lw = win_cache + Ls
                Lwp = -(-Lw // LANES) * LANES
                wflat = jnp.pad(win_full_s.reshape(Bs, Lw, 2 * cw), ((0, 0), (0, Lwp - Lw), (0, 0))).astype(BF16)
            j = i - n_a
            wq = _head_tiles(nsa_w_in[j][:, :H * dh], H, kvh, dh, axis=1).astype(BF16)
            wg = _pad_cols(nsa_w_in[j][:, H * dh:], LANES).astype(BF16)
            wo = _head_tiles(nsa_w_out[j], H, kvh, dh, axis=0).astype(BF16)
            QW = wq.shape[1]
            o_p = _nsa_attend(_matmul(Xp, wq).reshape(Bp, Lp, QW), _matmul(Xp, wg).reshape(Bp, Lp, LANES),
                              kc_p, vc_p, kvb_p, kvb_p, kvb_p, kvb_p, (0, 0, 2, 3, 4, 5),
                              nq=Q_BLOCK, t0=0, tk=tk_p, n_cmp=Lpp // cmp_blk, cmp_blk=cmp_blk, n_sel=n_sel_p,
                              kvh=kvh, dh=dh, win_rows=min(WINDOW + Q_BLOCK, Lp), w0=0, causal_tiles=True)
            Xp = _matmul_ln(o_p.reshape(Tp, QW), wo, Xp, ln_g[i, 0], ln_b[i, 0], alpha)
            pad_q = lambda v: jnp.pad(v[:Ts].reshape(Bs, Ls, -1), ((0, 0), (0, SUBLANES - Ls), (0, 0)))
            o_s = _nsa_attend(pad_q(_matmul(Xs, wq)), pad_q(_matmul(Xs, wg)), kc_s, vc_s, ks_s, vs_s, wflat, wflat,
                              (0, 0, 0, 0, 0, 1),
                              nq=SUBLANES, t0=past_len, tk=Lk_s, n_cmp=-(-(past_len + Ls) // SEL_BLOCK) * (SEL_BLOCK // cmp_blk),
                              cmp_blk=cmp_blk, n_sel=n_sel_s, kvh=kvh, dh=dh, win_rows=Lwp,
                              w0=past_len - win_cache, causal_tiles=False, seqs=_pick(Bs, (4, 2, 1)))
            o_s = jnp.pad(o_s[:, :Ls].reshape(Ts, QW), ((0, Tsp - Ts), (0, 0)))
            Xs = _matmul_ln(o_s, wo, Xs, ln_g[i, 0], ln_b[i, 0], alpha)
        peer = (peer_w_q[i].astype(BF16), peer_sub_keys[i], peer_u[i].astype(BF16), peer_v[i].astype(BF16),
                ln_g[i, 1], ln_b[i, 1], alpha)
        Xp = _peer_layer(Xp, *peer)
        Xs = _peer_layer(Xs, *peer)

    rows_p = kv_p[:, :, :4 * cw].reshape(Bp, Lp, n_rows, kvh, dh)
    rows_s = kv_s[:, :, :4 * cw].reshape(Bs, Ls, n_rows, kvh, dh)
    win_p = kv_p[:, :, 4 * cw:].reshape(Bp, Lp, 2, kvh, dh)[:, -min(WINDOW, Lp):]
    return (Xp.reshape(Bp, Lp, D), Xs[:Ts].reshape(Bs, Ls, D), rows_p, rows_s, win_p,
            win_full_s[:, -win_cache:], jnp.stack(ssm_p), jnp.stack(ssm_s), jnp.stack(conv_p), jnp.stack(conv_s))
```
